```python
import math
import jax, jax.numpy as jnp
from jax import lax
import numpy as np

D_MODEL = 1024
BATCH = 16
SEQ = 2048
DEPTH = 4

MEM_LEN = 256
EPS = 1e-6
ROPE_THETA = 10000.0
A_HEADS = D_MODEL // 128
A_HEAD_DIM = 64
A_WIDTH = A_HEADS * A_HEAD_DIM
MOBA_BLOCK = 256
MOBA_TOPK = 3
MOBA_QCHUNK = 32
B_HEADS = D_MODEL // 128
B_NOPE = 64
B_ROPE = 32
B_VDIM = 64
B_Q_LORA = 3 * D_MODEL // 8
B_KV_LORA = D_MODEL // 4
B_WIDTH = B_HEADS * B_VDIM
ATTN_QBLOCK = 128
X_HEADS = 4
X_HEAD_DIM = 128
X_WIDTH = X_HEADS * X_HEAD_DIM
D_FF = ((8 * D_MODEL // 3 + 255) // 256) * 256
IN_WIDTH = 3 * A_WIDTH + B_Q_LORA + B_KV_LORA + B_ROPE + 2 * D_MODEL

kernel_name = "hybrid_moba_mla_gated_trunk"

F32 = jnp.float32


def rmsnorm(t, g):
    tf = t.astype(F32)
    y = tf * lax.rsqrt(jnp.mean(tf * tf, axis=-1, keepdims=True) + EPS)
    return (y * g.astype(F32)).astype(t.dtype)


def rope(t, pos):
    dim = t.shape[-1]
    half = dim // 2
    inv = jnp.exp(-math.log(ROPE_THETA) * (2.0 / dim) * jnp.arange(half, dtype=F32))
    ang = pos.astype(F32)[:, None, :, None] * inv
    cos, sin = jnp.cos(ang), jnp.sin(ang)
    tf = t.astype(F32)
    t1, t2 = tf[..., :half], tf[..., half:]
    return jnp.concatenate([t1 * cos - t2 * sin, t2 * cos + t1 * sin], axis=-1).astype(t.dtype)


def _heads(t, n):
    b, s, _ = t.shape
    return t.reshape(b, s, n, -1).transpose(0, 2, 1, 3)


def _merge(o):
    b, h, s, d = o.shape
    return o.transpose(0, 2, 1, 3).reshape(b, s, h * d)


def moba_attention(q, k, v):
    b, h, s, d = q.shape
    nb = -(-s // MOBA_BLOCK)
    pad = nb * MOBA_BLOCK - s
    kp = jnp.pad(k, ((0, 0), (0, 0), (0, pad), (0, 0)))
    vp = jnp.pad(v, ((0, 0), (0, 0), (0, pad), (0, 0)))
    kb = kp.reshape(b, h, nb, MOBA_BLOCK, d)
    vb = vp.reshape(b, h, nb, MOBA_BLOCK, d)
    kbar = jnp.mean(kb.astype(F32), axis=3)
    n_sel = min(MOBA_TOPK, nb)
    scale = d ** -0.5
    bi = jnp.arange(b)[:, None, None, None]
    hi = jnp.arange(h)[None, :, None, None]
    blk_ids = jnp.arange(nb)
    slot_ids = jnp.arange(n_sel)
    own_offsets = jnp.arange(MOBA_BLOCK)

    def chunk(c):
        start = c * MOBA_QCHUNK
        qc = lax.dynamic_slice_in_dim(q, start, MOBA_QCHUNK, axis=2)
        tq = start + jnp.arange(MOBA_QCHUNK)
        cur = start // MOBA_BLOCK
        gate = jnp.einsum('bhcd,bhnd->bhcn', qc.astype(F32), kbar)
        gate = jnp.where(blk_ids < cur, gate, -jnp.inf)
        _, sel = lax.top_k(gate, n_sel)
        ks = kb[bi, hi, sel]
        vs = vb[bi, hi, sel]
        s_sel = jnp.einsum('bhcd,bhcnkd->bhcnk', qc, ks, preferred_element_type=F32) * scale
        s_sel = jnp.where((slot_ids < cur)[:, None], s_sel, -jnp.inf)
        s_sel = s_sel.reshape(b, h, MOBA_QCHUNK, n_sel * MOBA_BLOCK)
        k_own = lax.dynamic_slice_in_dim(kp, cur * MOBA_BLOCK, MOBA_BLOCK, axis=2)
        v_own = lax.dynamic_slice_in_dim(vp, cur * MOBA_BLOCK, MOBA_BLOCK, axis=2)
        kpos = cur * MOBA_BLOCK + own_offsets
        s_own = jnp.einsum('bhcd,bhkd->bhck', qc, k_own, preferred_element_type=F32) * scale
        s_own = jnp.where(kpos[None, :] <= tq[:, None], s_own, -jnp.inf)
        p = jax.nn.softmax(jnp.concatenate([s_sel, s_own], axis=-1), axis=-1)
        p_sel = p[..., :n_sel * MOBA_BLOCK].reshape(b, h, MOBA_QCHUNK, n_sel, MOBA_BLOCK).astype(v.dtype)
        p_own = p[..., n_sel * MOBA_BLOCK:].astype(v.dtype)
        return (jnp.einsum('bhcnk,bhcnkd->bhcd', p_sel, vs)
                + jnp.einsum('bhck,bhkd->bhcd', p_own, v_own))

    out = lax.map(chunk, jnp.arange(s // MOBA_QCHUNK))
    return out.transpose(1, 2, 0, 3, 4).reshape(b, h, s, d)


def mla_attention(q_nope, q_pe, k_nope, k_pe, v):
    b, h, s, _ = q_nope.shape
    scale = (B_NOPE + B_ROPE) ** -0.5
    kpos = jnp.arange(s)

    def blk(i):
        st = i * ATTN_QBLOCK
        qn = lax.dynamic_slice_in_dim(q_nope, st, ATTN_QBLOCK, axis=2)
        qr = lax.dynamic_slice_in_dim(q_pe, st, ATTN_QBLOCK, axis=2)
        sc = (jnp.einsum('bhqd,bhkd->bhqk', qn, k_nope, preferred_element_type=F32)
              + jnp.einsum('bhqr,bkr->bhqk', qr, k_pe, preferred_element_type=F32)) * scale
        qpos = st + jnp.arange(ATTN_QBLOCK)
        sc = jnp.where(kpos[None, :] <= qpos[:, None], sc, -jnp.inf)
        p = jax.nn.softmax(sc, axis=-1).astype(v.dtype)
        return jnp.einsum('bhqk,bhkd->bhqd', p, v)

    out = lax.map(blk, jnp.arange(s // ATTN_QBLOCK))
    return out.transpose(1, 2, 0, 3, 4).reshape(b, h, s, B_VDIM)


def parallel_mixer(h, pos, w_in, q_lat_norm, w_uq, kv_lat_norm, w_ukv, w_branch_a, w_branch_b, w_out):
    splits = list(np.cumsum([A_WIDTH, A_WIDTH, A_WIDTH, B_Q_LORA, B_KV_LORA, B_ROPE, D_MODEL]))
    qa, ka, va, q_lat, kv_lat, k_pe, g_a, g_b = jnp.split(h @ w_in, splits, axis=-1)
    o_a = moba_attention(rope(_heads(qa, A_HEADS), pos), rope(_heads(ka, A_HEADS), pos), _heads(va, A_HEADS))
    q = _heads(rmsnorm(q_lat, q_lat_norm) @ w_uq, B_HEADS)
    q_nope, q_pe = q[..., :B_NOPE], rope(q[..., B_NOPE:], pos)
    kv = _heads(rmsnorm(kv_lat, kv_lat_norm) @ w_ukv, B_HEADS)
    k_nope, v_b = kv[..., :B_NOPE], kv[..., B_NOPE:]
    k_pe = rope(k_pe[:, None], pos)[:, 0]
    o_b = mla_attention(q_nope, q_pe, k_nope, k_pe, v_b)
    merged = (jax.nn.sigmoid(g_a) * (_merge(o_a) @ w_branch_a)
              + jax.nn.sigmoid(g_b) * (_merge(o_b) @ w_branch_b))
    return merged @ w_out


def memory_cross_attention(h, m, w_xq, w_xkv, w_xo):
    q = _heads(h @ w_xq, X_HEADS)
    k, v = jnp.split(m @ w_xkv, 2, axis=-1)
    k, v = _heads(k, X_HEADS), _heads(v, X_HEADS)
    sc = jnp.einsum('bhqd,bhkd->bhqk', q, k, preferred_element_type=F32) * (X_HEAD_DIM ** -0.5)
    p = jax.nn.softmax(sc, axis=-1).astype(v.dtype)
    return _merge(jnp.einsum('bhqk,bhkd->bhqd', p, v)) @ w_xo


def swiglu(h, w_gate_up, w_down):
    gt, up = jnp.split(h @ w_gate_up, 2, axis=-1)
    return (jax.nn.silu(gt) * up) @ w_down


def setup_inputs(seed: int = 0) -> dict:
    key = jax.random.key(seed)
    ks = jax.random.split(key, 24)
    out_scale = (2 * DEPTH) ** -0.5

    def dense(k, shape, fan_in, scale=1.0):
        return jax.random.normal(k, shape, F32) * (scale * fan_in ** -0.5)

    def gain(k, shape):
        return 1.0 + 0.05 * jax.random.normal(k, shape, F32)

    L = DEPTH
    offsets = jax.random.randint(ks[2], (BATCH,), 0, 1024, dtype=jnp.int32)
    positions = offsets[:, None] + jnp.arange(SEQ, dtype=jnp.int32)[None, :]
    return {
        "x": jax.random.normal(ks[0], (BATCH, SEQ, D_MODEL), F32),
        "mem": jax.random.normal(ks[1], (BATCH, MEM_LEN, D_MODEL), F32),
        "positions": positions,
        "norm_mix": gain(ks[3], (L, D_MODEL)),
        "w_in": dense(ks[4], (L, D_MODEL, IN_WIDTH), D_MODEL),
        "q_lat_norm": gain(ks[5], (L, B_Q_LORA)),
        "w_uq": dense(ks[6], (L, B_Q_LORA, B_HEADS * (B_NOPE + B_ROPE)), B_Q_LORA),
        "kv_lat_norm": gain(ks[7], (L, B_KV_LORA)),
        "w_ukv": dense(ks[8], (L, B_KV_LORA, B_HEADS * (B_NOPE + B_VDIM)), B_KV_LORA),
        "w_branch_a": dense(ks[9], (L, A_WIDTH, D_MODEL), A_WIDTH),
        "w_branch_b": dense(ks[10], (L, B_WIDTH, D_MODEL), B_WIDTH),
        "w_out": dense(ks[11], (L, D_MODEL, D_MODEL), D_MODEL, out_scale),
        "norm_xattn": gain(ks[12], (L, D_MODEL)),
        "norm_mem": gain(ks[13], (L, D_MODEL)),
        "w_xq": dense(ks[14], (L, D_MODEL, X_WIDTH), D_MODEL),
        "w_xkv": dense(ks[15], (L, D_MODEL, 2 * X_WIDTH), D_MODEL),
        "w_xo": dense(ks[16], (L, X_WIDTH, D_MODEL), X_WIDTH, out_scale),
        "norm_ffn": gain(ks[17], (L, D_MODEL)),
        "w_gate_up": dense(ks[18], (L, D_MODEL, 2 * D_FF), D_MODEL),
        "w_down": dense(ks[19], (L, D_FF, D_MODEL), D_FF, out_scale),
        "norm_final": gain(ks[20], (D_MODEL,)),
    }


def reference(x, mem, positions, norm_mix, w_in, q_lat_norm, w_uq, kv_lat_norm, w_ukv,
              w_branch_a, w_branch_b, w_out, norm_xattn, norm_mem, w_xq, w_xkv, w_xo,
              norm_ffn, w_gate_up, w_down, norm_final):
    for l in range(DEPTH):
        x = x + parallel_mixer(rmsnorm(x, norm_mix[l]), positions, w_in[l], q_lat_norm[l], w_uq[l],
                               kv_lat_norm[l], w_ukv[l], w_branch_a[l], w_branch_b[l], w_out[l])
        x = x + memory_cross_attention(rmsnorm(x, norm_xattn[l]), rmsnorm(mem, norm_mem[l]),
                                       w_xq[l], w_xkv[l], w_xo[l])
        x = x + swiglu(rmsnorm(x, norm_ffn[l]), w_gate_up[l], w_down[l])
    return rmsnorm(x, norm_final)
```

```python
import functools
import math

import numpy as np
import jax
import jax.numpy as jnp
from jax import lax
from jax.experimental import pallas as pl
from jax.experimental.pallas import tpu as pltpu

F32 = jnp.float32
BF16 = jnp.bfloat16

EPS = 1e-6
ROPE_THETA = 10000.0
LANES = 128
HEAD_DIM = 64
N_HEADS = 8
N_PAIRS = N_HEADS // 2
A_WIDTH = N_HEADS * HEAD_DIM
MLA_ROPE = 32
MLA_QK = HEAD_DIM + MLA_ROPE
Q_LORA = 384
KV_LORA = 256
MOBA_BLOCK = 256
MOBA_TOPK = 3
X_HEADS = 4
X_HEAD_DIM = 128
X_WIDTH = X_HEADS * X_HEAD_DIM
NEG_BIG = -1e30
VMEM_LIMIT = 56 * 1024 * 1024
TOK_TILE = 512
ATT_BLOCK = 256
TAB_W = 5 * LANES


def _resident(shape, index_map):
    return pl.BlockSpec(shape, index_map, pipeline_mode=pl.Buffered(1))


def _params(n_axes):
    return pltpu.CompilerParams(dimension_semantics=("arbitrary",) * n_axes,
                                vmem_limit_bytes=VMEM_LIMIT)


def _rms(x, g):
    return x * lax.rsqrt(jnp.mean(x * x, axis=-1, keepdims=True) + EPS) * g


def _dot(a, b):
    return jnp.dot(a, b, preferred_element_type=F32)


def _dot_nt(a, b):
    return lax.dot_general(a, b, (((1,), (1,)), ((), ())), preferred_element_type=F32)


def _rope_table_body(pos_ref, tab_ref):
    pos = pos_ref[...]
    lane = lax.broadcasted_iota(jnp.int32, pos.shape, 1)
    inv_a = jnp.exp((-math.log(ROPE_THETA) * (2.0 / HEAD_DIM)) * (lane % 32).astype(F32))
    ang_a = pos * inv_a
    tab_ref[:, 0:LANES] = jnp.cos(ang_a)
    tab_ref[:, LANES:2 * LANES] = jnp.where(lane < 64, -jnp.sin(ang_a), jnp.sin(ang_a))
    inv_b = jnp.exp((-math.log(ROPE_THETA) * (2.0 / MLA_ROPE)) * (lane % 16).astype(F32))
    ang_b = pos * inv_b
    cos_b, sin_b = jnp.cos(ang_b), jnp.sin(ang_b)
    tab_ref[:, 2 * LANES:3 * LANES] = jnp.where(lane < 64, 1.0, jnp.where(lane < 96, cos_b, 0.0))
    tab_ref[:, 3 * LANES:4 * LANES] = jnp.where((lane >= 64) & (lane < 80), -sin_b, 0.0)
    tab_ref[:, 4 * LANES:5 * LANES] = jnp.where((lane >= 80) & (lane < 96), sin_b, 0.0)


def _rope_tables(pos_lanes):
    t = pos_lanes.shape[0]
    tm = min(1024, t)
    return pl.pallas_call(
        _rope_table_body,
        out_shape=jax.ShapeDtypeStruct((t, TAB_W), F32),
        grid=(t // tm,),
        in_specs=[pl.BlockSpec((tm, LANES), lambda i: (i, 0))],
        out_specs=pl.BlockSpec((tm, TAB_W), lambda i: (i, 0)),
        compiler_params=_params(1),
        name="rope_tables",
    )(pos_lanes)


def _mem_kv_body(mem_ref, g_ref, w_ref, o_ref):
    h = _rms(mem_ref[...], g_ref[...]).astype(BF16)
    o_ref[...] = _dot(h, w_ref[...]).astype(BF16)


def _mem_kv(mem, norm_mem, w_xkv):
    b, m, d = mem.shape
    n_layers = w_xkv.shape[0]
    return pl.pallas_call(
        _mem_kv_body,
        out_shape=jax.ShapeDtypeStruct((n_layers, b, m, 2 * X_WIDTH), BF16),
        grid=(n_layers, b),
        in_specs=[pl.BlockSpec((None, m, d), lambda l, i: (i, 0, 0)),
                  pl.BlockSpec((None, 1, d), lambda l, i: (l, 0, 0)),
                  pl.BlockSpec((None, d, 2 * X_WIDTH), lambda l, i: (l, 0, 0))],
        out_specs=pl.BlockSpec((None, None, m, 2 * X_WIDTH), lambda l, i: (l, i, 0, 0)),
        compiler_params=_params(2),
        name="mem_kv",
    )(mem, norm_mem, w_xkv)


_C_QK = 0
_C_VA = 2 * A_WIDTH
_C_LAT = 3 * A_WIDTH
_C_G = _C_LAT + Q_LORA + KV_LORA + LANES
_KB_W = N_HEADS * LANES


def _mixer_in_body(x_ref, g_ref, win_ref, qg_ref, wuq_ref, kvg_ref, wukv_ref, tab_ref,
                   qa_ref, ka_ref, va_ref, qb_ref, kb_ref, vb_ref, ga_ref, gb_ref):
    d = x_ref.shape[-1]
    h = _rms(x_ref[...], g_ref[...]).astype(BF16)
    cos_a, sin_a = tab_ref[:, 0:LANES], tab_ref[:, LANES:2 * LANES]
    cos_b = tab_ref[:, 2 * LANES:3 * LANES]
    sin_lo, sin_hi = tab_ref[:, 3 * LANES:4 * LANES], tab_ref[:, 4 * LANES:5 * LANES]

    def rope_a(y):
        return y * cos_a + pltpu.roll(y, 64, axis=1) * sin_a

    def rope_b(y):
        return (y * cos_b + pltpu.roll(y, LANES - 16, axis=1) * sin_lo
                + pltpu.roll(y, 16, axis=1) * sin_hi)

    y = _dot(h, win_ref[:, _C_QK:_C_QK + 2 * A_WIDTH])
    scale_a = HEAD_DIM ** -0.5
    for p in range(N_PAIRS):
        sl = slice(p * LANES, (p + 1) * LANES)
        qa_ref[:, sl] = (rope_a(y[:, sl]) * scale_a).astype(BF16)
        ka_ref[:, sl] = rope_a(y[:, A_WIDTH + p * LANES:A_WIDTH + (p + 1) * LANES]).astype(BF16)

    va_ref[...] = _dot(h, win_ref[:, _C_VA:_C_VA + A_WIDTH]).astype(BF16)

    lat = _dot(h, win_ref[:, _C_LAT:_C_G])
    q_lat = _rms(lat[:, 0:Q_LORA], qg_ref[...]).astype(BF16)
    kv_lat = _rms(lat[:, Q_LORA:Q_LORA + KV_LORA], kvg_ref[...]).astype(BF16)
    k_pe = rope_b(lat[:, Q_LORA + KV_LORA:])
    yq = _dot(q_lat, wuq_ref[...])
    ykv = _dot(kv_lat, wukv_ref[...])
    for hd in range(N_HEADS):
        sl = slice(hd * LANES, (hd + 1) * LANES)
        qb_ref[:, sl] = rope_b(yq[:, sl]).astype(BF16)
        kb_ref[:, sl] = (ykv[:, sl] + k_pe).astype(BF16)
    vb_ref[...] = ykv[:, _KB_W:].astype(BF16)

    ga_ref[...] = jax.nn.sigmoid(_dot(h, win_ref[:, _C_G:_C_G + d])).astype(BF16)
    gb_ref[...] = jax.nn.sigmoid(_dot(h, win_ref[:, _C_G + d:_C_G + 2 * d])).astype(BF16)


def _mixer_in(x, layer, norm_mix, w_in, q_lat_norm, w_uq, kv_lat_norm, w_ukv, tabs):
    t, d = x.shape
    tm = min(TOK_TILE, t)
    in_w = w_in.shape[-1]
    tok = lambda w: pl.BlockSpec((tm, w), lambda i: (i, 0))
    lay = lambda *shape: _resident((None,) + shape, lambda i: (layer,) + (0,) * len(shape))
    widths = (A_WIDTH, A_WIDTH, A_WIDTH, _KB_W, _KB_W, A_WIDTH, d, d)
    return pl.pallas_call(
        _mixer_in_body,
        out_shape=[jax.ShapeDtypeStruct((t, w), BF16) for w in widths],
        grid=(t // tm,),
        in_specs=[tok(d), lay(1, d), lay(d, in_w), lay(1, Q_LORA), lay(Q_LORA, _KB_W),
                  lay(1, KV_LORA), lay(KV_LORA, _KB_W + A_WIDTH), tok(TAB_W)],
        out_specs=[tok(w) for w in widths],
        compiler_params=_params(1),
        name="mixer_in",
    )(x, norm_mix, w_in, q_lat_norm, w_uq, kv_lat_norm, w_ukv, tabs)


def _softmax_step(s, v, m, l, acc, scale):
    m_new = jnp.maximum(m, jnp.max(s, axis=-1, keepdims=True))
    alpha = jnp.exp((m - m_new) * scale)
    p = jnp.exp((s - m_new) * scale)
    l_new = alpha * l + jnp.sum(p, axis=-1, keepdims=True)
    acc_new = alpha * acc + _dot(p.astype(BF16), v)
    return m_new, l_new, acc_new


def _causal(s):
    row = lax.broadcasted_iota(jnp.int32, s.shape, 0)
    col = lax.broadcasted_iota(jnp.int32, s.shape, 1)
    return jnp.where(col <= row, s, NEG_BIG)


def _head_lane_masks(shape, rope_layout):
    lane = lax.broadcasted_iota(jnp.int32, shape, 1)
    first = (lane % 64 < 32) if rope_layout else (lane < 64)
    return first, jnp.logical_not(first)


def _moba_body(q_ref, k_ref, v_ref, e_ref, eye_ref, o_ref, kbar_ref):
    qi = pl.program_id(2)
    blk = q_ref.shape[0]
    n_blocks = k_ref.shape[0] // blk

    @pl.when(qi == 0)
    def _():
        kbar_ref[...] = jnp.zeros_like(kbar_ref)
        for n in range(n_blocks):
            kb = k_ref[n * blk:(n + 1) * blk, :].astype(F32)
            kbar_ref[n:n + 1, :] = jnp.mean(kb, axis=0, keepdims=True)

    q = q_ref[...]
    start = pl.multiple_of(qi * blk, blk)
    k_own = k_ref[pl.ds(start, blk), :]
    v_own = v_ref[pl.ds(start, blk), :]
    kbar = kbar_ref[...].astype(BF16)
    q_masks = _head_lane_masks(q.shape, True)
    v_masks = _head_lane_masks(v_own.shape, False)
    zero = jnp.zeros_like(q)
    row = lax.broadcasted_iota(jnp.int32, (kbar.shape[0], blk), 0)

    out = jnp.zeros((blk, LANES), F32)
    for hd in range(2):
        qh = jnp.where(q_masks[hd], q, zero)
        gate = jnp.where(row < qi, _dot_nt(kbar, qh), -jnp.inf)
        rank = jnp.zeros(gate.shape, jnp.int32)
        for n in range(n_blocks):
            gn = gate[n:n + 1, :]
            beats = (gn > gate) | ((gn == gate) & (row > n))
            rank = rank + beats.astype(jnp.int32)
        dropped = jnp.where((row < qi) & (rank < MOBA_TOPK), 0.0, 1.0).astype(BF16)
        dropped = jnp.concatenate([dropped, jnp.zeros((LANES - dropped.shape[0], blk), BF16)], axis=0)
        dropped_q = _dot_nt(eye_ref[...], dropped).astype(BF16)
        q_aug = jnp.concatenate([qh, dropped_q], axis=1)

        s = _causal(_dot_nt(qh, k_own))
        m = jnp.max(s, axis=-1, keepdims=True)
        p = jnp.exp(s - m)
        l = jnp.sum(p, axis=-1, keepdims=True)
        acc = _dot(p.astype(BF16), jnp.where(v_masks[hd], v_own, jnp.zeros_like(v_own)))

        def past(n, carry):
            m, l, acc = carry
            off = pl.multiple_of(n * blk, blk)
            k_aug = jnp.concatenate([k_ref[pl.ds(off, blk), :], e_ref[pl.ds(off, blk), :]], axis=1)
            vn = v_ref[pl.ds(off, blk), :]
            vn = jnp.where(v_masks[hd], vn, jnp.zeros_like(vn))
            return _softmax_step(_dot_nt(q_aug, k_aug), vn, m, l, acc, 1.0)

        m, l, acc = lax.fori_loop(0, qi, past, (m, l, acc))
        out = out + acc * (1.0 / l)
    o_ref[...] = out.astype(o_ref.dtype)


def _moba(qa, ka, va, e_mat, eye):
    b, s, _ = qa.shape
    blk = ATT_BLOCK
    return pl.pallas_call(
        _moba_body,
        out_shape=jax.ShapeDtypeStruct((b, s, A_WIDTH), BF16),
        grid=(b, N_PAIRS, s // blk),
        in_specs=[pl.BlockSpec((None, blk, LANES), lambda i, p, j: (i, j, p)),
                  pl.BlockSpec((None, s, LANES), lambda i, p, j: (i, 0, p)),
                  pl.BlockSpec((None, s, LANES), lambda i, p, j: (i, 0, p)),
                  _resident((s, LANES), lambda i, p, j: (0, 0)),
                  _resident((blk, blk), lambda i, p, j: (0, 0))],
        out_specs=pl.BlockSpec((None, blk, LANES), lambda i, p, j: (i, j, p)),
        scratch_shapes=[pltpu.VMEM((16, LANES), F32)],
        compiler_params=_params(3),
        name="moba_attn",
    )(qa, ka, va, e_mat, eye)


def _mla_body(q_ref, k_ref, v_ref, o_ref):
    qi = pl.program_id(2)
    blk = q_ref.shape[0]
    scale = MLA_QK ** -0.5
    start = pl.multiple_of(qi * blk, blk)
    v_own = v_ref[pl.ds(start, blk), :]
    v_masks = _head_lane_masks(v_own.shape, False)
    out = jnp.zeros((blk, LANES), F32)
    for hd in range(2):
        lanes = slice(hd * LANES, (hd + 1) * LANES)
        qh = q_ref[:, lanes]
        s = _causal(_dot_nt(qh, k_ref[pl.ds(start, blk), lanes]))
        m = jnp.max(s, axis=-1, keepdims=True)
        p = jnp.exp((s - m) * scale)
        l = jnp.sum(p, axis=-1, keepdims=True)
        acc = _dot(p.astype(BF16), jnp.where(v_masks[hd], v_own, jnp.zeros_like(v_own)))

        def past(n, carry):
            m, l, acc = carry
            off = pl.multiple_of(n * blk, blk)
            vn = v_ref[pl.ds(off, blk), :]
            vn = jnp.where(v_masks[hd], vn, jnp.zeros_like(vn))
            return _softmax_step(_dot_nt(qh, k_ref[pl.ds(off, blk), lanes]), vn, m, l, acc, scale)

        m, l, acc = lax.fori_loop(0, qi, past, (m, l, acc))
        out = out + acc * (1.0 / l)
    o_ref[...] = out.astype(o_ref.dtype)


def _mla(qb, kb, vb):
    b, s, _ = qb.shape
    blk = ATT_BLOCK
    return pl.pallas_call(
        _mla_body,
        out_shape=jax.ShapeDtypeStruct((b, s, A_WIDTH), BF16),
        grid=(b, N_PAIRS, s // blk),
        in_specs=[pl.BlockSpec((None, blk, 2 * LANES), lambda i, p, j: (i, j, p)),
                  pl.BlockSpec((None, s, 2 * LANES), lambda i, p, j: (i, 0, p)),
                  pl.BlockSpec((None, s, LANES), lambda i, p, j: (i, 0, p))],
        out_specs=pl.BlockSpec((None, blk, LANES), lambda i, p, j: (i, j, p)),
        compiler_params=_params(3),
        name="mla_attn",
    )(qb, kb, vb)


def _mixer_out_body(x_ref, oa_ref, ob_ref, ga_ref, gb_ref, wa_ref, wb_ref, wo_ref,
                    gx_ref, wxq_ref, kv_ref, wxo_ref, o_ref):
    merged = (ga_ref[...].astype(F32) * _dot(oa_ref[...], wa_ref[...])
              + gb_ref[...].astype(F32) * _dot(ob_ref[...], wb_ref[...]))
    x1 = x_ref[...] + _dot(merged.astype(BF16), wo_ref[...])

    q = _dot(_rms(x1, gx_ref[...]).astype(BF16), wxq_ref[...]).astype(BF16)
    scale = X_HEAD_DIM ** -0.5
    heads = []
    for hd in range(X_HEADS):
        lanes = slice(hd * X_HEAD_DIM, (hd + 1) * X_HEAD_DIM)
        s = _dot_nt(q[:, lanes], kv_ref[:, lanes])
        m = jnp.max(s, axis=-1, keepdims=True)
        p = jnp.exp((s - m) * scale)
        l = jnp.sum(p, axis=-1, keepdims=True)
        v = kv_ref[:, X_WIDTH + hd * X_HEAD_DIM:X_WIDTH + (hd + 1) * X_HEAD_DIM]
        heads.append((_dot(p.astype(BF16), v) * (1.0 / l)).astype(BF16))
    o_ref[...] = x1 + _dot(jnp.concatenate(heads, axis=1), wxo_ref[...])


def _mixer_out(x, layer, seq, oa, ob, ga, gb, w_a, w_b, w_out, norm_xattn, w_xq, mem_kv, w_xo):
    t, d = x.shape
    tm = min(TOK_TILE, seq)
    tiles_per_seq = seq // tm
    m = mem_kv.shape[2]
    tok = lambda w: pl.BlockSpec((tm, w), lambda i: (i, 0))
    lay = lambda *shape: _resident((None,) + shape, lambda i: (layer,) + (0,) * len(shape))
    return pl.pallas_call(
        _mixer_out_body,
        out_shape=jax.ShapeDtypeStruct((t, d), F32),
        grid=(t // tm,),
        in_specs=[tok(d), tok(A_WIDTH), tok(A_WIDTH), tok(d), tok(d),
                  lay(A_WIDTH, d), lay(A_WIDTH, d), lay(d, d),
                  lay(1, d), lay(d, X_WIDTH),
                  pl.BlockSpec((None, None, m, 2 * X_WIDTH), lambda i: (layer, i // tiles_per_seq, 0, 0)),
                  lay(X_WIDTH, d)],
        out_specs=tok(d),
        compiler_params=_params(1),
        name="mixer_out",
    )(x, oa, ob, ga, gb, w_a, w_b, w_out, norm_xattn, w_xq, mem_kv, w_xo)


FF_CHUNK = 1024


def _ffn_body(x_ref, g_ref, wgu_ref, wd_ref, gf_ref, o_ref, *, final):
    x = x_ref[...]
    h = _rms(x, g_ref[...]).astype(BF16)
    d_ff = wd_ref.shape[0]
    y = x
    for c0 in range(0, d_ff, FF_CHUNK):
        c1 = min(c0 + FF_CHUNK, d_ff)
        gt = _dot(h, wgu_ref[:, c0:c1])
        up = _dot(h, wgu_ref[:, d_ff + c0:d_ff + c1])
        a = (gt * jax.nn.sigmoid(gt) * up).astype(BF16)
        y = y + _dot(a, wd_ref[c0:c1, :])
    o_ref[...] = _rms(y, gf_ref[...]) if final else y


def _ffn(x, layer, norm_ffn, w_gate_up, w_down, norm_final, final):
    t, d = x.shape
    tm = min(TOK_TILE, t)
    d_ff = w_down.shape[1]
    tok = pl.BlockSpec((tm, d), lambda i: (i, 0))
    lay = lambda *shape: _resident((None,) + shape, lambda i: (layer,) + (0,) * len(shape))
    return pl.pallas_call(
        functools.partial(_ffn_body, final=final),
        out_shape=jax.ShapeDtypeStruct((t, d), F32),
        grid=(t // tm,),
        in_specs=[tok, lay(1, d), lay(d, 2 * d_ff), lay(d_ff, d),
                  _resident((1, d), lambda i: (0, 0))],
        out_specs=tok,
        compiler_params=_params(1),
        name="ffn_final" if final else "ffn",
    )(x, norm_ffn, w_gate_up, w_down, norm_final)


def _pair_rope_perm():
    perm = []
    for p in range(N_PAIRS):
        for half in range(2):
            for hd in (2 * p, 2 * p + 1):
                perm.extend(hd * HEAD_DIM + half * 32 + np.arange(32))
    return np.asarray(perm, np.int32)


def _prep_weights(w_in, w_uq, w_ukv):
    n_layers, d, _ = w_in.shape
    perm = _pair_rope_perm()
    c = np.cumsum([A_WIDTH, A_WIDTH, A_WIDTH, Q_LORA, KV_LORA, MLA_ROPE, d])
    w_kpe = jnp.pad(w_in[..., c[4]:c[5]], ((0, 0), (0, 0), (HEAD_DIM, LANES - MLA_QK)))
    w_in_r = jnp.concatenate(
        [w_in[..., 0:c[0]][..., perm], w_in[..., c[0]:c[1]][..., perm], w_in[..., c[1]:c[4]],
         w_kpe, w_in[..., c[5]:]], axis=-1).astype(BF16)
    w_uq_r = jnp.pad(w_uq.reshape(n_layers, Q_LORA, N_HEADS, MLA_QK),
                     ((0, 0), (0, 0), (0, 0), (0, LANES - MLA_QK))).reshape(n_layers, Q_LORA, _KB_W).astype(BF16)
    ukv = w_ukv.reshape(n_layers, KV_LORA, N_HEADS, 2 * HEAD_DIM)
    w_k = jnp.pad(ukv[..., :HEAD_DIM], ((0, 0), (0, 0), (0, 0), (0, LANES - HEAD_DIM))).reshape(n_layers, KV_LORA, _KB_W)
    w_v = ukv[..., HEAD_DIM:].reshape(n_layers, KV_LORA, A_WIDTH)
    w_ukv_r = jnp.concatenate([w_k, w_v], axis=-1).astype(BF16)
    return w_in_r, w_uq_r, w_ukv_r


def kernel(x, mem, positions, norm_mix, w_in, q_lat_norm, w_uq, kv_lat_norm, w_ukv, w_branch_a, w_branch_b, w_out, norm_xattn, norm_mem, w_xq, w_xkv, w_xo, norm_ffn, w_gate_up, w_down, norm_final):
    b, s, d = x.shape
    n_layers = w_in.shape[0]
    t = b * s
    assert s % ATT_BLOCK == 0 and s // MOBA_BLOCK <= 16 and ATT_BLOCK == MOBA_BLOCK

    w_in_r, w_uq_r, w_ukv_r = _prep_weights(w_in, w_uq, w_ukv)
    bf = lambda w: w.astype(BF16)
    row = lambda g: g.reshape(g.shape[0], 1, g.shape[1])
    norm_mix, q_lat_norm, kv_lat_norm = row(norm_mix), row(q_lat_norm), row(kv_lat_norm)
    norm_xattn, norm_mem, norm_ffn = row(norm_xattn), row(norm_mem), row(norm_ffn)
    w_a, w_b, w_o, w_q, w_kv, w_xo_b = bf(w_branch_a), bf(w_branch_b), bf(w_out), bf(w_xq), bf(w_xkv), bf(w_xo)
    w_gu, w_dn = bf(w_gate_up), bf(w_down)

    pos_lanes = jnp.broadcast_to(positions.reshape(t, 1).astype(F32), (t, LANES))
    tabs = _rope_tables(pos_lanes)
    mem_kv = _mem_kv(mem, norm_mem, w_kv)
    blk_of_row = np.arange(s) // MOBA_BLOCK
    e_mat = jnp.asarray(np.where(blk_of_row[:, None] == np.arange(LANES)[None, :], NEG_BIG, 0.0), BF16)
    eye = jnp.asarray(np.eye(ATT_BLOCK), BF16)

    xt = x.reshape(t, d)
    for layer in range(n_layers):
        qa, ka, va, qb, kb, vb, ga, gb = _mixer_in(
            xt, layer, norm_mix, w_in_r, q_lat_norm, w_uq_r, kv_lat_norm, w_ukv_r, tabs)
        seq3 = lambda a: a.reshape(b, s, a.shape[-1])
        oa = _moba(seq3(qa), seq3(ka), seq3(va), e_mat, eye).reshape(t, A_WIDTH)
        ob = _mla(seq3(qb), seq3(kb), seq3(vb)).reshape(t, A_WIDTH)
        xt = _mixer_out(xt, layer, s, oa, ob, ga, gb, w_a, w_b, w_o, norm_xattn, w_q, mem_kv, w_xo_b)
        xt = _ffn(xt, layer, norm_ffn, w_gu, w_dn, norm_final.reshape(1, d), layer == n_layers - 1)
    return xt.reshape(b, s, d)
```

```python
import functools
import math

import numpy as np
import jax
import jax.numpy as jnp
from jax import lax
from jax.experimental import pallas as pl
from jax.experimental.pallas import tpu as pltpu

F32 = jnp.float32
BF16 = jnp.bfloat16

EPS = 1e-6
ROPE_THETA = 10000.0
LANES = 128
HEAD_DIM = 64
N_HEADS = 8
N_PAIRS = N_HEADS // 2
A_WIDTH = N_HEADS * HEAD_DIM
MLA_ROPE = 32
MLA_QK = HEAD_DIM + MLA_ROPE
Q_LORA = 384
KV_LORA = 256
MOBA_BLOCK = 256
MOBA_TOPK = 3
X_HEADS = 4
X_HEAD_DIM = 128
X_WIDTH = X_HEADS * X_HEAD_DIM
NEG_BIG = -1e30
VMEM_LIMIT = 56 * 1024 * 1024
TOK_TILE = 512
ATT_TILE = 512
LOG2E = math.log2(math.e)
TAB_W = 5 * LANES


def _resident(shape, index_map):
    return pl.BlockSpec(shape, index_map, pipeline_mode=pl.Buffered(1))


def _params(n_axes):
    return pltpu.CompilerParams(dimension_semantics=("arbitrary",) * n_axes,
                                vmem_limit_bytes=VMEM_LIMIT)


def _rms(x, g):
    return x * lax.rsqrt(jnp.mean(x * x, axis=-1, keepdims=True) + EPS) * g


def _dot(a, b):
    return jnp.dot(a, b, preferred_element_type=F32)


def _dot_nt(a, b):
    return lax.dot_general(a, b, (((1,), (1,)), ((), ())), preferred_element_type=F32)


def _rope_table_body(pos_ref, tab_ref):
    pos = pos_ref[...]
    lane = lax.broadcasted_iota(jnp.int32, pos.shape, 1)
    inv_a = jnp.exp((-math.log(ROPE_THETA) * (2.0 / HEAD_DIM)) * (lane % 32).astype(F32))
    ang_a = pos * inv_a
    tab_ref[:, 0:LANES] = jnp.cos(ang_a)
    tab_ref[:, LANES:2 * LANES] = jnp.where(lane < 64, -jnp.sin(ang_a), jnp.sin(ang_a))
    inv_b = jnp.exp((-math.log(ROPE_THETA) * (2.0 / MLA_ROPE)) * (lane % 16).astype(F32))
    ang_b = pos * inv_b
    cos_b, sin_b = jnp.cos(ang_b), jnp.sin(ang_b)
    tab_ref[:, 2 * LANES:3 * LANES] = jnp.where(lane < 64, 1.0, jnp.where(lane < 96, cos_b, 0.0))
    tab_ref[:, 3 * LANES:4 * LANES] = jnp.where((lane >= 64) & (lane < 80), -sin_b, 0.0)
    tab_ref[:, 4 * LANES:5 * LANES] = jnp.where((lane >= 80) & (lane < 96), sin_b, 0.0)


def _rope_tables(pos_lanes):
    t = pos_lanes.shape[0]
    tm = min(1024, t)
    return pl.pallas_call(
        _rope_table_body,
        out_shape=jax.ShapeDtypeStruct((t, TAB_W), F32),
        grid=(t // tm,),
        in_specs=[pl.BlockSpec((tm, LANES), lambda i: (i, 0))],
        out_specs=pl.BlockSpec((tm, TAB_W), lambda i: (i, 0)),
        compiler_params=_params(1),
        name="rope_tables",
    )(pos_lanes)


def _mem_kv_body(mem_ref, g_ref, w_ref, o_ref):
    h = _rms(mem_ref[...], g_ref[...]).astype(BF16)
    o_ref[...] = _dot(h, w_ref[...]).astype(BF16)


def _mem_kv(mem, norm_mem, w_xkv):
    b, m, d = mem.shape
    n_layers = w_xkv.shape[0]
    return pl.pallas_call(
        _mem_kv_body,
        out_shape=jax.ShapeDtypeStruct((n_layers, b, m, 2 * X_WIDTH), BF16),
        grid=(n_layers, b),
        in_specs=[pl.BlockSpec((None, m, d), lambda l, i: (i, 0, 0)),
                  pl.BlockSpec((None, 1, d), lambda l, i: (l, 0, 0)),
                  pl.BlockSpec((None, d, 2 * X_WIDTH), lambda l, i: (l, 0, 0))],
        out_specs=pl.BlockSpec((None, None, m, 2 * X_WIDTH), lambda l, i: (l, i, 0, 0)),
        compiler_params=_params(2),
        name="mem_kv",
    )(mem, norm_mem, w_xkv)


_C_QK = 0
_C_VA = 2 * A_WIDTH
_C_LAT = 3 * A_WIDTH
_C_G = _C_LAT + Q_LORA + KV_LORA + LANES
_KB_W = N_HEADS * LANES


def _mixer_in_body(x_ref, g_ref, win_ref, qg_ref, wuq_ref, kvg_ref, wukv_ref, tab_ref,
                   qa_ref, ka_ref, va_ref, qb_ref, kb_ref, vb_ref, ga_ref, gb_ref):
    d = x_ref.shape[-1]
    h = _rms(x_ref[...], g_ref[...]).astype(BF16)
    cos_a, sin_a = tab_ref[:, 0:LANES], tab_ref[:, LANES:2 * LANES]
    cos_b = tab_ref[:, 2 * LANES:3 * LANES]
    sin_lo, sin_hi = tab_ref[:, 3 * LANES:4 * LANES], tab_ref[:, 4 * LANES:5 * LANES]

    def rope_a(y):
        return y * cos_a + pltpu.roll(y, 64, axis=1) * sin_a

    def rope_b(y):
        return (y * cos_b + pltpu.roll(y, LANES - 16, axis=1) * sin_lo
                + pltpu.roll(y, 16, axis=1) * sin_hi)

    y = _dot(h, win_ref[:, _C_QK:_C_QK + 2 * A_WIDTH])
    scale_a = HEAD_DIM ** -0.5
    for p in range(N_PAIRS):
        sl = slice(p * LANES, (p + 1) * LANES)
        qa_ref[:, sl] = (rope_a(y[:, sl]) * scale_a).astype(BF16)
        ka_ref[:, sl] = rope_a(y[:, A_WIDTH + p * LANES:A_WIDTH + (p + 1) * LANES]).astype(BF16)

    va_ref[...] = _dot(h, win_ref[:, _C_VA:_C_VA + A_WIDTH]).astype(BF16)

    lat = _dot(h, win_ref[:, _C_LAT:_C_G])
    q_lat = _rms(lat[:, 0:Q_LORA], qg_ref[...]).astype(BF16)
    kv_lat = _rms(lat[:, Q_LORA:Q_LORA + KV_LORA], kvg_ref[...]).astype(BF16)
    k_pe = rope_b(lat[:, Q_LORA + KV_LORA:])
    yq = _dot(q_lat, wuq_ref[...])
    ykv = _dot(kv_lat, wukv_ref[...])
    for hd in range(N_HEADS):
        sl = slice(hd * LANES, (hd + 1) * LANES)
        qb_ref[:, sl] = rope_b(yq[:, sl]).astype(BF16)
        kb_ref[:, sl] = (ykv[:, sl] + k_pe).astype(BF16)
    vb_ref[...] = ykv[:, _KB_W:].astype(BF16)

    ga_ref[...] = jax.nn.sigmoid(_dot(h, win_ref[:, _C_G:_C_G + d])).astype(BF16)
    gb_ref[...] = jax.nn.sigmoid(_dot(h, win_ref[:, _C_G + d:_C_G + 2 * d])).astype(BF16)


def _mixer_in(x, layer, norm_mix, w_in, q_lat_norm, w_uq, kv_lat_norm, w_ukv, tabs):
    t, d = x.shape
    tm = min(TOK_TILE, t)
    in_w = w_in.shape[-1]
    tok = lambda w: pl.BlockSpec((tm, w), lambda i: (i, 0))
    lay = lambda *shape: _resident((None,) + shape, lambda i: (layer,) + (0,) * len(shape))
    widths = (A_WIDTH, A_WIDTH, A_WIDTH, _KB_W, _KB_W, A_WIDTH, d, d)
    return pl.pallas_call(
        _mixer_in_body,
        out_shape=[jax.ShapeDtypeStruct((t, w), BF16) for w in widths],
        grid=(t // tm,),
        in_specs=[tok(d), lay(1, d), lay(d, in_w), lay(1, Q_LORA), lay(Q_LORA, _KB_W),
                  lay(1, KV_LORA), lay(KV_LORA, _KB_W + A_WIDTH), tok(TAB_W)],
        out_specs=[tok(w) for w in widths],
        compiler_params=_params(1),
        name="mixer_in",
    )(x, norm_mix, w_in, q_lat_norm, w_uq, kv_lat_norm, w_ukv, tabs)


def _softmax_init(tq):
    return (jnp.full((tq, 1), NEG_BIG, F32), jnp.zeros((tq, 1), F32), jnp.zeros((tq, LANES), F32))


def _softmax_step(s, v, state, c2):
    m, l, acc = state
    m_new = jnp.maximum(m, jnp.max(s, axis=-1, keepdims=True))
    alpha = jnp.exp2((m - m_new) * c2)
    p = jnp.exp2((s - m_new) * c2)
    l_new = alpha * l + jnp.sum(p, axis=-1, keepdims=True)
    acc_new = alpha * acc + _dot(p.astype(BF16), v)
    return m_new, l_new, acc_new


def _causal(s):
    row = lax.broadcasted_iota(jnp.int32, s.shape, 0)
    col = lax.broadcasted_iota(jnp.int32, s.shape, 1)
    return jnp.where(col <= row, s, NEG_BIG)


def _merge_pair(states):
    (_, l0, acc0), (_, l1, acc1) = states
    lane = lax.broadcasted_iota(jnp.int32, acc0.shape, 1)
    return jnp.where(lane < HEAD_DIM, acc0 * (1.0 / l0), acc1 * (1.0 / l1))


def _moba_body(q_ref, k_ref, v_ref, e_ref, eye_ref, o_ref, kbar_ref):
    qi = pl.program_id(2)
    tq = q_ref.shape[0]
    n_blocks = k_ref.shape[0] // MOBA_BLOCK
    c2 = LOG2E

    @pl.when(qi == 0)
    def _():
        kbar_ref[...] = jnp.zeros_like(kbar_ref)
        for n in range(n_blocks):
            kb = k_ref[n * MOBA_BLOCK:(n + 1) * MOBA_BLOCK, :].astype(F32)
            kbar_ref[n:n + 1, :] = jnp.mean(kb, axis=0, keepdims=True)

    q = q_ref[...]
    kbar = kbar_ref[...].astype(BF16)
    lane = lax.broadcasted_iota(jnp.int32, q.shape, 1)
    row = lax.broadcasted_iota(jnp.int32, (kbar.shape[0], tq), 0)
    col = lax.broadcasted_iota(jnp.int32, (kbar.shape[0], tq), 1)
    cur = qi * (tq // MOBA_BLOCK) + col // MOBA_BLOCK
    q_aug = []
    for hd in range(2):
        qh = jnp.where((lane % 64 < 32) == (hd == 0), q, jnp.zeros_like(q))
        gate = jnp.where(row < cur, _dot_nt(kbar, qh), -jnp.inf)
        rank = jnp.zeros(gate.shape, jnp.int32)
        for n in range(n_blocks):
            gn = gate[n:n + 1, :]
            beats = (gn > gate) | ((gn == gate) & (row > n))
            rank = rank + beats.astype(jnp.int32)
        keep = ((row < cur) & (rank < MOBA_TOPK)) | (row == cur)
        dropped = jnp.where(keep, 0.0, 1.0).astype(BF16)
        dropped = jnp.concatenate([dropped, jnp.zeros((LANES - dropped.shape[0], tq), BF16)], axis=0)
        dropped_q = _dot_nt(eye_ref[...], dropped).astype(BF16)
        q_aug.append(jnp.concatenate([qh, dropped_q], axis=1))

    def tile(off, states, diagonal):
        k_aug = jnp.concatenate([k_ref[pl.ds(off, tq), :], e_ref[pl.ds(off, tq), :]], axis=1)
        vt = v_ref[pl.ds(off, tq), :]
        new = []
        for hd in range(2):
            s = _dot_nt(q_aug[hd], k_aug)
            new.append(_softmax_step(_causal(s) if diagonal else s, vt, states[hd], c2))
        return tuple(new)

    states = tile(pl.multiple_of(qi * tq, tq), (_softmax_init(tq),) * 2, True)
    states = lax.fori_loop(0, qi, lambda j, st: tile(pl.multiple_of(j * tq, tq), st, False), states)
    o_ref[...] = _merge_pair(states).astype(o_ref.dtype)


def _moba(qa, ka, va, e_mat, eye):
    b, s, _ = qa.shape
    tq = ATT_TILE
    return pl.pallas_call(
        _moba_body,
        out_shape=jax.ShapeDtypeStruct((b, s, A_WIDTH), BF16),
        grid=(b, N_PAIRS, s // tq),
        in_specs=[pl.BlockSpec((None, tq, LANES), lambda i, p, j: (i, j, p)),
                  pl.BlockSpec((None, s, LANES), lambda i, p, j: (i, 0, p)),
                  pl.BlockSpec((None, s, LANES), lambda i, p, j: (i, 0, p)),
                  _resident((s, LANES), lambda i, p, j: (0, 0)),
                  _resident((tq, tq), lambda i, p, j: (0, 0))],
        out_specs=pl.BlockSpec((None, tq, LANES), lambda i, p, j: (i, j, p)),
        scratch_shapes=[pltpu.VMEM((16, LANES), F32)],
        compiler_params=_params(3),
        name="moba_attn",
    )(qa, ka, va, e_mat, eye)


def _mla_body(q_ref, k_ref, v_ref, o_ref):
    qi = pl.program_id(2)
    tq = q_ref.shape[0]
    c2 = MLA_QK ** -0.5 * LOG2E
    q = [q_ref[:, hd * LANES:(hd + 1) * LANES] for hd in range(2)]

    def tile(off, states, diagonal):
        vt = v_ref[pl.ds(off, tq), :]
        new = []
        for hd in range(2):
            s = _dot_nt(q[hd], k_ref[pl.ds(off, tq), hd * LANES:(hd + 1) * LANES])
            new.append(_softmax_step(_causal(s) if diagonal else s, vt, states[hd], c2))
        return tuple(new)

    states = tile(pl.multiple_of(qi * tq, tq), (_softmax_init(tq),) * 2, True)
    states = lax.fori_loop(0, qi, lambda j, st: tile(pl.multiple_of(j * tq, tq), st, False), states)
    o_ref[...] = _merge_pair(states).astype(o_ref.dtype)


def _mla(qb, kb, vb):
    b, s, _ = qb.shape
    tq = ATT_TILE
    return pl.pallas_call(
        _mla_body,
        out_shape=jax.ShapeDtypeStruct((b, s, A_WIDTH), BF16),
        grid=(b, N_PAIRS, s // tq),
        in_specs=[pl.BlockSpec((None, tq, 2 * LANES), lambda i, p, j: (i, j, p)),
                  pl.BlockSpec((None, s, 2 * LANES), lambda i, p, j: (i, 0, p)),
                  pl.BlockSpec((None, s, LANES), lambda i, p, j: (i, 0, p))],
        out_specs=pl.BlockSpec((None, tq, LANES), lambda i, p, j: (i, j, p)),
        compiler_params=_params(3),
        name="mla_attn",
    )(qb, kb, vb)


def _mixer_out_body(x_ref, oa_ref, ob_ref, ga_ref, gb_ref, wa_ref, wb_ref, wo_ref,
                    gx_ref, wxq_ref, kv_ref, wxo_ref, o_ref):
    merged = (ga_ref[...].astype(F32) * _dot(oa_ref[...], wa_ref[...])
              + gb_ref[...].astype(F32) * _dot(ob_ref[...], wb_ref[...]))
    x1 = x_ref[...] + _dot(merged.astype(BF16), wo_ref[...])

    q = _dot(_rms(x1, gx_ref[...]).astype(BF16), wxq_ref[...]).astype(BF16)
    scale = X_HEAD_DIM ** -0.5
    heads = []
    for hd in range(X_HEADS):
        lanes = slice(hd * X_HEAD_DIM, (hd + 1) * X_HEAD_DIM)
        s = _dot_nt(q[:, lanes], kv_ref[:, lanes])
        m = jnp.max(s, axis=-1, keepdims=True)
        p = jnp.exp((s - m) * scale)
        l = jnp.sum(p, axis=-1, keepdims=True)
        v = kv_ref[:, X_WIDTH + hd * X_HEAD_DIM:X_WIDTH + (hd + 1) * X_HEAD_DIM]
        heads.append((_dot(p.astype(BF16), v) * (1.0 / l)).astype(BF16))
    o_ref[...] = x1 + _dot(jnp.concatenate(heads, axis=1), wxo_ref[...])


def _mixer_out(x, layer, seq, oa, ob, ga, gb, w_a, w_b, w_out, norm_xattn, w_xq, mem_kv, w_xo):
    t, d = x.shape
    tm = min(TOK_TILE, seq)
    tiles_per_seq = seq // tm
    m = mem_kv.shape[2]
    tok = lambda w: pl.BlockSpec((tm, w), lambda i: (i, 0))
    lay = lambda *shape: _resident((None,) + shape, lambda i: (layer,) + (0,) * len(shape))
    return pl.pallas_call(
        _mixer_out_body,
        out_shape=jax.ShapeDtypeStruct((t, d), F32),
        grid=(t // tm,),
        in_specs=[tok(d), tok(A_WIDTH), tok(A_WIDTH), tok(d), tok(d),
                  lay(A_WIDTH, d), lay(A_WIDTH, d), lay(d, d),
                  lay(1, d), lay(d, X_WIDTH),
                  pl.BlockSpec((None, None, m, 2 * X_WIDTH), lambda i: (layer, i // tiles_per_seq, 0, 0)),
                  lay(X_WIDTH, d)],
        out_specs=tok(d),
        compiler_params=_params(1),
        name="mixer_out",
    )(x, oa, ob, ga, gb, w_a, w_b, w_out, norm_xattn, w_xq, mem_kv, w_xo)


FF_CHUNK = 1024


def _ffn_body(x_ref, g_ref, wgu_ref, wd_ref, gf_ref, o_ref, *, final):
    x = x_ref[...]
    h = _rms(x, g_ref[...]).astype(BF16)
    d_ff = wd_ref.shape[0]
    y = x
    for c0 in range(0, d_ff, FF_CHUNK):
        c1 = min(c0 + FF_CHUNK, d_ff)
        gt = _dot(h, wgu_ref[:, c0:c1])
        up = _dot(h, wgu_ref[:, d_ff + c0:d_ff + c1])
        a = (gt * jax.nn.sigmoid(gt) * up).astype(BF16)
        y = y + _dot(a, wd_ref[c0:c1, :])
    o_ref[...] = _rms(y, gf_ref[...]) if final else y


def _ffn(x, layer, norm_ffn, w_gate_up, w_down, norm_final, final):
    t, d = x.shape
    tm = min(TOK_TILE, t)
    d_ff = w_down.shape[1]
    tok = pl.BlockSpec((tm, d), lambda i: (i, 0))
    lay = lambda *shape: _resident((None,) + shape, lambda i: (layer,) + (0,) * len(shape))
    return pl.pallas_call(
        functools.partial(_ffn_body, final=final),
        out_shape=jax.ShapeDtypeStruct((t, d), F32),
        grid=(t // tm,),
        in_specs=[tok, lay(1, d), lay(d, 2 * d_ff), lay(d_ff, d),
                  _resident((1, d), lambda i: (0, 0))],
        out_specs=tok,
        compiler_params=_params(1),
        name="ffn_final" if final else "ffn",
    )(x, norm_ffn, w_gate_up, w_down, norm_final)


def _pair_rope_perm():
    perm = []
    for p in range(N_PAIRS):
        for half in range(2):
            for hd in (2 * p, 2 * p + 1):
                perm.extend(hd * HEAD_DIM + half * 32 + np.arange(32))
    return np.asarray(perm, np.int32)


def _prep_weights(w_in, w_uq, w_ukv):
    n_layers, d, _ = w_in.shape
    perm = _pair_rope_perm()
    c = np.cumsum([A_WIDTH, A_WIDTH, A_WIDTH, Q_LORA, KV_LORA, MLA_ROPE, d])
    w_kpe = jnp.pad(w_in[..., c[4]:c[5]], ((0, 0), (0, 0), (HEAD_DIM, LANES - MLA_QK)))
    w_in_r = jnp.concatenate(
        [w_in[..., 0:c[0]][..., perm], w_in[..., c[0]:c[1]][..., perm], w_in[..., c[1]:c[4]],
         w_kpe, w_in[..., c[5]:]], axis=-1).astype(BF16)
    w_uq_r = jnp.pad(w_uq.reshape(n_layers, Q_LORA, N_HEADS, MLA_QK),
                     ((0, 0), (0, 0), (0, 0), (0, LANES - MLA_QK))).reshape(n_layers, Q_LORA, _KB_W).astype(BF16)
    ukv = w_ukv.reshape(n_layers, KV_LORA, N_HEADS, 2 * HEAD_DIM)
    w_k = jnp.pad(ukv[..., :HEAD_DIM], ((0, 0), (0, 0), (0, 0), (0, LANES - HEAD_DIM))).reshape(n_layers, KV_LORA, _KB_W)
    w_v = ukv[..., HEAD_DIM:].reshape(n_layers, KV_LORA, A_WIDTH)
    w_ukv_r = jnp.concatenate([w_k, w_v], axis=-1).astype(BF16)
    return w_in_r, w_uq_r, w_ukv_r


def kernel(x, mem, positions, norm_mix, w_in, q_lat_norm, w_uq, kv_lat_norm, w_ukv, w_branch_a, w_branch_b, w_out, norm_xattn, norm_mem, w_xq, w_xkv, w_xo, norm_ffn, w_gate_up, w_down, norm_final):
    b, s, d = x.shape
    n_layers = w_in.shape[0]
    t = b * s
    assert s % ATT_TILE == 0 and ATT_TILE % MOBA_BLOCK == 0 and s // MOBA_BLOCK <= 16

    w_in_r, w_uq_r, w_ukv_r = _prep_weights(w_in, w_uq, w_ukv)
    bf = lambda w: w.astype(BF16)
    row = lambda g: g.reshape(g.shape[0], 1, g.shape[1])
    norm_mix, q_lat_norm, kv_lat_norm = row(norm_mix), row(q_lat_norm), row(kv_lat_norm)
    norm_xattn, norm_mem, norm_ffn = row(norm_xattn), row(norm_mem), row(norm_ffn)
    w_a, w_b, w_o, w_q, w_kv, w_xo_b = bf(w_branch_a), bf(w_branch_b), bf(w_out), bf(w_xq), bf(w_xkv), bf(w_xo)
    w_gu, w_dn = bf(w_gate_up), bf(w_down)

    pos_lanes = jnp.broadcast_to(positions.reshape(t, 1).astype(F32), (t, LANES))
    tabs = _rope_tables(pos_lanes)
    mem_kv = _mem_kv(mem, norm_mem, w_kv)
    blk_of_row = np.arange(s) // MOBA_BLOCK
    e_mat = jnp.asarray(np.where(blk_of_row[:, None] == np.arange(LANES)[None, :], NEG_BIG, 0.0), BF16)
    eye = jnp.asarray(np.eye(ATT_TILE), BF16)

    xt = x.reshape(t, d)
    for layer in range(n_layers):
        qa, ka, va, qb, kb, vb, ga, gb = _mixer_in(
            xt, layer, norm_mix, w_in_r, q_lat_norm, w_uq_r, kv_lat_norm, w_ukv_r, tabs)
        seq3 = lambda a: a.reshape(b, s, a.shape[-1])
        oa = _moba(seq3(qa), seq3(ka), seq3(va), e_mat, eye).reshape(t, A_WIDTH)
        ob = _mla(seq3(qb), seq3(kb), seq3(vb)).reshape(t, A_WIDTH)
        xt = _mixer_out(xt, layer, s, oa, ob, ga, gb, w_a, w_b, w_o, norm_xattn, w_q, mem_kv, w_xo_b)
        xt = _ffn(xt, layer, norm_ffn, w_gu, w_dn, norm_final.reshape(1, d), layer == n_layers - 1)
    return xt.reshape(b, s, d)
```

```python
import functools
import math

import numpy as np
import jax
import jax.numpy as jnp
from jax import lax
from jax.experimental import pallas as pl
from jax.experimental.pallas import tpu as pltpu

F32 = jnp.float32
BF16 = jnp.bfloat16

EPS = 1e-6
ROPE_THETA = 10000.0
LANES = 128
HEAD_DIM = 64
N_HEADS = 8
N_PAIRS = N_HEADS // 2
A_WIDTH = N_HEADS * HEAD_DIM
MLA_ROPE = 32
MLA_QK = HEAD_DIM + MLA_ROPE
Q_LORA = 384
KV_LORA = 256
MOBA_BLOCK = 256
MOBA_TOPK = 3
X_HEADS = 4
X_HEAD_DIM = 128
X_WIDTH = X_HEADS * X_HEAD_DIM
NEG_BIG = -1e30
VMEM_LIMIT = 56 * 1024 * 1024
TOK_TILE = 512
ATT_TILE = 512
LOG2E = math.log2(math.e)
TAB_W = 5 * LANES


def _resident(shape, index_map):
    return pl.BlockSpec(shape, index_map, pipeline_mode=pl.Buffered(1))


def _params(n_axes):
    return pltpu.CompilerParams(dimension_semantics=("arbitrary",) * n_axes,
                                vmem_limit_bytes=VMEM_LIMIT)


def _rms(x, g):
    return x * lax.rsqrt(jnp.mean(x * x, axis=-1, keepdims=True) + EPS) * g


def _dot(a, b):
    return jnp.dot(a, b, preferred_element_type=F32)


def _dot_nt(a, b):
    return lax.dot_general(a, b, (((1,), (1,)), ((), ())), preferred_element_type=F32)


def _rope_table_body(pos_ref, tab_ref):
    pos = pos_ref[...]
    lane = lax.broadcasted_iota(jnp.int32, pos.shape, 1)
    inv_a = jnp.exp((-math.log(ROPE_THETA) * (2.0 / HEAD_DIM)) * (lane % 32).astype(F32))
    ang_a = pos * inv_a
    tab_ref[:, 0:LANES] = jnp.cos(ang_a)
    tab_ref[:, LANES:2 * LANES] = jnp.where(lane < 64, -jnp.sin(ang_a), jnp.sin(ang_a))
    inv_b = jnp.exp((-math.log(ROPE_THETA) * (2.0 / MLA_ROPE)) * (lane % 16).astype(F32))
    ang_b = pos * inv_b
    cos_b, sin_b = jnp.cos(ang_b), jnp.sin(ang_b)
    tab_ref[:, 2 * LANES:3 * LANES] = jnp.where(lane < 64, 1.0, jnp.where(lane < 96, cos_b, 0.0))
    tab_ref[:, 3 * LANES:4 * LANES] = jnp.where((lane >= 64) & (lane < 80), -sin_b, 0.0)
    tab_ref[:, 4 * LANES:5 * LANES] = jnp.where((lane >= 80) & (lane < 96), sin_b, 0.0)


def _rope_tables(pos_lanes):
    t = pos_lanes.shape[0]
    tm = min(1024, t)
    return pl.pallas_call(
        _rope_table_body,
        out_shape=jax.ShapeDtypeStruct((t, TAB_W), F32),
        grid=(t // tm,),
        in_specs=[pl.BlockSpec((tm, LANES), lambda i: (i, 0))],
        out_specs=pl.BlockSpec((tm, TAB_W), lambda i: (i, 0)),
        compiler_params=_params(1),
        name="rope_tables",
    )(pos_lanes)


def _mem_kv_body(mem_ref, g_ref, w_ref, o_ref):
    h = _rms(mem_ref[...], g_ref[...]).astype(BF16)
    o_ref[...] = _dot(h, w_ref[...]).astype(BF16)


def _mem_kv(mem, norm_mem, w_xkv):
    b, m, d = mem.shape
    n_layers = w_xkv.shape[0]
    return pl.pallas_call(
        _mem_kv_body,
        out_shape=jax.ShapeDtypeStruct((n_layers, b, m, 2 * X_WIDTH), BF16),
        grid=(n_layers, b),
        in_specs=[pl.BlockSpec((None, m, d), lambda l, i: (i, 0, 0)),
                  pl.BlockSpec((None, 1, d), lambda l, i: (l, 0, 0)),
                  pl.BlockSpec((None, d, 2 * X_WIDTH), lambda l, i: (l, 0, 0))],
        out_specs=pl.BlockSpec((None, None, m, 2 * X_WIDTH), lambda l, i: (l, i, 0, 0)),
        compiler_params=_params(2),
        name="mem_kv",
    )(mem, norm_mem, w_xkv)


_C_QK = 0
_C_VA = 2 * A_WIDTH
_C_LAT = 3 * A_WIDTH
_C_G = _C_LAT + Q_LORA + KV_LORA + LANES
_KB_W = N_HEADS * LANES


def _mixer_in_body(x_ref, g_ref, win_ref, qg_ref, wuq_ref, kvg_ref, wukv_ref, tab_ref,
                   qa_ref, ka_ref, va_ref, qb_ref, kb_ref, vb_ref, ga_ref, gb_ref):
    d = x_ref.shape[-1]
    h = _rms(x_ref[...], g_ref[...]).astype(BF16)
    cos_a, sin_a = tab_ref[:, 0:LANES], tab_ref[:, LANES:2 * LANES]
    cos_b = tab_ref[:, 2 * LANES:3 * LANES]
    sin_lo, sin_hi = tab_ref[:, 3 * LANES:4 * LANES], tab_ref[:, 4 * LANES:5 * LANES]

    def rope_a(y):
        return y * cos_a + pltpu.roll(y, 64, axis=1) * sin_a

    def rope_b(y):
        return (y * cos_b + pltpu.roll(y, LANES - 16, axis=1) * sin_lo
                + pltpu.roll(y, 16, axis=1) * sin_hi)

    scale_a = HEAD_DIM ** -0.5 * LOG2E
    scale_b = MLA_QK ** -0.5 * LOG2E
    y = _dot(h, win_ref[:, _C_QK:_C_QK + 2 * A_WIDTH])
    for p in range(N_PAIRS):
        sl = slice(p * LANES, (p + 1) * LANES)
        qa_ref[:, sl] = (rope_a(y[:, sl]) * scale_a).astype(BF16)
        ka_ref[:, sl] = rope_a(y[:, A_WIDTH + p * LANES:A_WIDTH + (p + 1) * LANES]).astype(BF16)

    va_ref[...] = _dot(h, win_ref[:, _C_VA:_C_VA + A_WIDTH]).astype(BF16)

    lat = _dot(h, win_ref[:, _C_LAT:_C_G])
    q_lat = _rms(lat[:, 0:Q_LORA], qg_ref[...]).astype(BF16)
    kv_lat = _rms(lat[:, Q_LORA:Q_LORA + KV_LORA], kvg_ref[...]).astype(BF16)
    k_pe = rope_b(lat[:, Q_LORA + KV_LORA:])
    yq = _dot(q_lat, wuq_ref[...])
    ykv = _dot(kv_lat, wukv_ref[...])
    for hd in range(N_HEADS):
        sl = slice(hd * LANES, (hd + 1) * LANES)
        qb_ref[:, sl] = (rope_b(yq[:, sl]) * scale_b).astype(BF16)
        kb_ref[:, sl] = (ykv[:, sl] + k_pe).astype(BF16)
    vb_ref[...] = ykv[:, _KB_W:].astype(BF16)

    ga_ref[...] = jax.nn.sigmoid(_dot(h, win_ref[:, _C_G:_C_G + d])).astype(BF16)
    gb_ref[...] = jax.nn.sigmoid(_dot(h, win_ref[:, _C_G + d:_C_G + 2 * d])).astype(BF16)


def _mixer_in(x, layer, norm_mix, w_in, q_lat_norm, w_uq, kv_lat_norm, w_ukv, tabs):
    t, d = x.shape
    tm = min(TOK_TILE, t)
    in_w = w_in.shape[-1]
    tok = lambda w: pl.BlockSpec((tm, w), lambda i: (i, 0))
    lay = lambda *shape: _resident((None,) + shape, lambda i: (layer,) + (0,) * len(shape))
    widths = (A_WIDTH, A_WIDTH, A_WIDTH, _KB_W, _KB_W, A_WIDTH, d, d)
    return pl.pallas_call(
        _mixer_in_body,
        out_shape=[jax.ShapeDtypeStruct((t, w), BF16) for w in widths],
        grid=(t // tm,),
        in_specs=[tok(d), lay(1, d), lay(d, in_w), lay(1, Q_LORA), lay(Q_LORA, _KB_W),
                  lay(1, KV_LORA), lay(KV_LORA, _KB_W + A_WIDTH), tok(TAB_W)],
        out_specs=[tok(w) for w in widths],
        compiler_params=_params(1),
        name="mixer_in",
    )(x, norm_mix, w_in, q_lat_norm, w_uq, kv_lat_norm, w_ukv, tabs)


def _softmax_init(tq):
    return (jnp.full((tq, 1), NEG_BIG, F32), jnp.zeros((tq, LANES), F32))


def _softmax_step(s, v_ones, state):
    m, acc = state
    m_new = jnp.maximum(m, jnp.max(s, axis=-1, keepdims=True))
    p = jnp.exp2(s - m_new)
    acc_new = jnp.exp2(m - m_new) * acc + _dot(p.astype(BF16), v_ones)
    return m_new, acc_new


def _values_and_ones(v):
    lane = lax.broadcasted_iota(jnp.int32, v.shape, 1)
    one = jnp.ones_like(v)
    return jnp.where(lane < HEAD_DIM, v, one), jnp.where(lane < HEAD_DIM, one, v)


def _causal(s):
    row = lax.broadcasted_iota(jnp.int32, s.shape, 0)
    col = lax.broadcasted_iota(jnp.int32, s.shape, 1)
    return jnp.where(col <= row, s, NEG_BIG)


def _merge_pair(states):
    (_, acc0), (_, acc1) = states
    lane = lax.broadcasted_iota(jnp.int32, acc0.shape, 1)
    num = jnp.where(lane < HEAD_DIM, acc0, acc1)
    den = pltpu.roll(jnp.where(lane < HEAD_DIM, acc1, acc0), HEAD_DIM, axis=1)
    return num / den


def _moba_body(q_ref, k_ref, v_ref, e_ref, o_ref, kbar_ref):
    qi = pl.program_id(2)
    tq = q_ref.shape[0]
    n_blocks = k_ref.shape[0] // MOBA_BLOCK

    @pl.when(qi == 0)
    def _():
        kbar_ref[...] = jnp.zeros_like(kbar_ref)
        for n in range(n_blocks):
            kb = k_ref[n * MOBA_BLOCK:(n + 1) * MOBA_BLOCK, :].astype(F32)
            kbar_ref[n:n + 1, :] = jnp.mean(kb, axis=0, keepdims=True)

    q = q_ref[...]
    kbar = kbar_ref[...].astype(BF16)
    lane = lax.broadcasted_iota(jnp.int32, q.shape, 1)
    row = lax.broadcasted_iota(jnp.int32, (kbar.shape[0], tq), 0)
    col = lax.broadcasted_iota(jnp.int32, (kbar.shape[0], tq), 1)
    cur = qi * (tq // MOBA_BLOCK) + col // MOBA_BLOCK
    q_aug = []
    for hd in range(2):
        qh = jnp.where((lane % 64 < 32) == (hd == 0), q, jnp.zeros_like(q))
        gate = jnp.where(row < cur, _dot_nt(kbar, qh), -jnp.inf)
        rank = jnp.zeros(gate.shape, jnp.int32)
        for n in range(n_blocks):
            gn = gate[n:n + 1, :]
            beats = (gn > gate) | ((gn == gate) & (row > n))
            rank = rank + beats.astype(jnp.int32)
        keep = ((row < cur) & (rank < MOBA_TOPK)) | (row == cur)
        dropped = jnp.where(keep, 0.0, 1.0)
        dropped = jnp.concatenate([dropped, jnp.zeros((LANES - dropped.shape[0], tq), F32)], axis=0)
        q_aug.append(jnp.concatenate([qh, dropped.T.astype(BF16)], axis=1))

    def tile(off, states, diagonal):
        k_aug = jnp.concatenate([k_ref[pl.ds(off, tq), :], e_ref[pl.ds(off, tq), :]], axis=1)
        v_ones = _values_and_ones(v_ref[pl.ds(off, tq), :])
        new = []
        for hd in range(2):
            s = _dot_nt(q_aug[hd], k_aug)
            new.append(_softmax_step(_causal(s) if diagonal else s, v_ones[hd], states[hd]))
        return tuple(new)

    states = tile(pl.multiple_of(qi * tq, tq), (_softmax_init(tq),) * 2, True)
    states = lax.fori_loop(0, qi, lambda j, st: tile(pl.multiple_of(j * tq, tq), st, False), states)
    o_ref[...] = _merge_pair(states).astype(o_ref.dtype)


def _moba(qa, ka, va, e_mat):
    b, s, _ = qa.shape
    tq = ATT_TILE
    return pl.pallas_call(
        _moba_body,
        out_shape=jax.ShapeDtypeStruct((b, s, A_WIDTH), BF16),
        grid=(b, N_PAIRS, s // tq),
        in_specs=[pl.BlockSpec((None, tq, LANES), lambda i, p, j: (i, j, p)),
                  pl.BlockSpec((None, s, LANES), lambda i, p, j: (i, 0, p)),
                  pl.BlockSpec((None, s, LANES), lambda i, p, j: (i, 0, p)),
                  _resident((s, LANES), lambda i, p, j: (0, 0))],
        out_specs=pl.BlockSpec((None, tq, LANES), lambda i, p, j: (i, j, p)),
        scratch_shapes=[pltpu.VMEM((16, LANES), F32)],
        compiler_params=_params(3),
        name="moba_attn",
    )(qa, ka, va, e_mat)


def _mla_body(q_ref, k_ref, v_ref, o_ref):
    qi = pl.program_id(2)
    tq = q_ref.shape[0]
    q = [q_ref[:, hd * LANES:(hd + 1) * LANES] for hd in range(2)]

    def tile(off, states, diagonal):
        v_ones = _values_and_ones(v_ref[pl.ds(off, tq), :])
        new = []
        for hd in range(2):
            s = _dot_nt(q[hd], k_ref[pl.ds(off, tq), hd * LANES:(hd + 1) * LANES])
            new.append(_softmax_step(_causal(s) if diagonal else s, v_ones[hd], states[hd]))
        return tuple(new)

    states = tile(pl.multiple_of(qi * tq, tq), (_softmax_init(tq),) * 2, True)
    states = lax.fori_loop(0, qi, lambda j, st: tile(pl.multiple_of(j * tq, tq), st, False), states)
    o_ref[...] = _merge_pair(states).astype(o_ref.dtype)


def _mla(qb, kb, vb):
    b, s, _ = qb.shape
    tq = ATT_TILE
    return pl.pallas_call(
        _mla_body,
        out_shape=jax.ShapeDtypeStruct((b, s, A_WIDTH), BF16),
        grid=(b, N_PAIRS, s // tq),
        in_specs=[pl.BlockSpec((None, tq, 2 * LANES), lambda i, p, j: (i, j, p)),
                  pl.BlockSpec((None, s, 2 * LANES), lambda i, p, j: (i, 0, p)),
                  pl.BlockSpec((None, s, LANES), lambda i, p, j: (i, 0, p))],
        out_specs=pl.BlockSpec((None, tq, LANES), lambda i, p, j: (i, j, p)),
        compiler_params=_params(3),
        name="mla_attn",
    )(qb, kb, vb)


def _mixer_out_body(x_ref, oa_ref, ob_ref, ga_ref, gb_ref, wa_ref, wb_ref, wo_ref,
                    gx_ref, wxq_ref, kv_ref, wxo_ref, o_ref):
    merged = (ga_ref[...].astype(F32) * _dot(oa_ref[...], wa_ref[...])
              + gb_ref[...].astype(F32) * _dot(ob_ref[...], wb_ref[...]))
    x1 = x_ref[...] + _dot(merged.astype(BF16), wo_ref[...])

    q = _dot(_rms(x1, gx_ref[...]).astype(BF16), wxq_ref[...]).astype(BF16)
    scale = X_HEAD_DIM ** -0.5
    heads = []
    for hd in range(X_HEADS):
        lanes = slice(hd * X_HEAD_DIM, (hd + 1) * X_HEAD_DIM)
        s = _dot_nt(q[:, lanes], kv_ref[:, lanes])
        m = jnp.max(s, axis=-1, keepdims=True)
        p = jnp.exp((s - m) * scale)
        l = jnp.sum(p, axis=-1, keepdims=True)
        v = kv_ref[:, X_WIDTH + hd * X_HEAD_DIM:X_WIDTH + (hd + 1) * X_HEAD_DIM]
        heads.append((_dot(p.astype(BF16), v) * (1.0 / l)).astype(BF16))
    o_ref[...] = x1 + _dot(jnp.concatenate(heads, axis=1), wxo_ref[...])


def _mixer_out(x, layer, seq, oa, ob, ga, gb, w_a, w_b, w_out, norm_xattn, w_xq, mem_kv, w_xo):
    t, d = x.shape
    tm = min(TOK_TILE, seq)
    tiles_per_seq = seq // tm
    m = mem_kv.shape[2]
    tok = lambda w: pl.BlockSpec((tm, w), lambda i: (i, 0))
    lay = lambda *shape: _resident((None,) + shape, lambda i: (layer,) + (0,) * len(shape))
    return pl.pallas_call(
        _mixer_out_body,
        out_shape=jax.ShapeDtypeStruct((t, d), F32),
        grid=(t // tm,),
        in_specs=[tok(d), tok(A_WIDTH), tok(A_WIDTH), tok(d), tok(d),
                  lay(A_WIDTH, d), lay(A_WIDTH, d), lay(d, d),
                  lay(1, d), lay(d, X_WIDTH),
                  pl.BlockSpec((None, None, m, 2 * X_WIDTH), lambda i: (layer, i // tiles_per_seq, 0, 0)),
                  lay(X_WIDTH, d)],
        out_specs=tok(d),
        compiler_params=_params(1),
        name="mixer_out",
    )(x, oa, ob, ga, gb, w_a, w_b, w_out, norm_xattn, w_xq, mem_kv, w_xo)


FF_CHUNK = 1024


def _ffn_body(x_ref, g_ref, wgu_ref, wd_ref, gf_ref, o_ref, *, final):
    x = x_ref[...]
    h = _rms(x, g_ref[...]).astype(BF16)
    d_ff = wd_ref.shape[0]
    y = x
    for c0 in range(0, d_ff, FF_CHUNK):
        c1 = min(c0 + FF_CHUNK, d_ff)
        gt = _dot(h, wgu_ref[:, c0:c1])
        up = _dot(h, wgu_ref[:, d_ff + c0:d_ff + c1])
        a = (gt * jax.nn.sigmoid(gt) * up).astype(BF16)
        y = y + _dot(a, wd_ref[c0:c1, :])
    o_ref[...] = _rms(y, gf_ref[...]) if final else y


def _ffn(x, layer, norm_ffn, w_gate_up, w_down, norm_final, final):
    t, d = x.shape
    tm = min(TOK_TILE, t)
    d_ff = w_down.shape[1]
    tok = pl.BlockSpec((tm, d), lambda i: (i, 0))
    lay = lambda *shape: _resident((None,) + shape, lambda i: (layer,) + (0,) * len(shape))
    return pl.pallas_call(
        functools.partial(_ffn_body, final=final),
        out_shape=jax.ShapeDtypeStruct((t, d), F32),
        grid=(t // tm,),
        in_specs=[tok, lay(1, d), lay(d, 2 * d_ff), lay(d_ff, d),
                  _resident((1, d), lambda i: (0, 0))],
        out_specs=tok,
        compiler_params=_params(1),
        name="ffn_final" if final else "ffn",
    )(x, norm_ffn, w_gate_up, w_down, norm_final)


def _pair_rope_perm():
    perm = []
    for p in range(N_PAIRS):
        for half in range(2):
            for hd in (2 * p, 2 * p + 1):
                perm.extend(hd * HEAD_DIM + half * 32 + np.arange(32))
    return np.asarray(perm, np.int32)


def _prep_weights(w_in, w_uq, w_ukv):
    n_layers, d, _ = w_in.shape
    perm = _pair_rope_perm()
    c = np.cumsum([A_WIDTH, A_WIDTH, A_WIDTH, Q_LORA, KV_LORA, MLA_ROPE, d])
    w_kpe = jnp.pad(w_in[..., c[4]:c[5]], ((0, 0), (0, 0), (HEAD_DIM, LANES - MLA_QK)))
    w_in_r = jnp.concatenate(
        [w_in[..., 0:c[0]][..., perm], w_in[..., c[0]:c[1]][..., perm], w_in[..., c[1]:c[4]],
         w_kpe, w_in[..., c[5]:]], axis=-1).astype(BF16)
    w_uq_r = jnp.pad(w_uq.reshape(n_layers, Q_LORA, N_HEADS, MLA_QK),
                     ((0, 0), (0, 0), (0, 0), (0, LANES - MLA_QK))).reshape(n_layers, Q_LORA, _KB_W).astype(BF16)
    ukv = w_ukv.reshape(n_layers, KV_LORA, N_HEADS, 2 * HEAD_DIM)
    w_k = jnp.pad(ukv[..., :HEAD_DIM], ((0, 0), (0, 0), (0, 0), (0, LANES - HEAD_DIM))).reshape(n_layers, KV_LORA, _KB_W)
    w_v = ukv[..., HEAD_DIM:].reshape(n_layers, KV_LORA, A_WIDTH)
    w_ukv_r = jnp.concatenate([w_k, w_v], axis=-1).astype(BF16)
    return w_in_r, w_uq_r, w_ukv_r


def kernel(x, mem, positions, norm_mix, w_in, q_lat_norm, w_uq, kv_lat_norm, w_ukv, w_branch_a, w_branch_b, w_out, norm_xattn, norm_mem, w_xq, w_xkv, w_xo, norm_ffn, w_gate_up, w_down, norm_final):
    b, s, d = x.shape
    n_layers = w_in.shape[0]
    t = b * s
    assert s % ATT_TILE == 0 and ATT_TILE % MOBA_BLOCK == 0 and s // MOBA_BLOCK <= 16

    w_in_r, w_uq_r, w_ukv_r = _prep_weights(w_in, w_uq, w_ukv)
    bf = lambda w: w.astype(BF16)
    row = lambda g: g.reshape(g.shape[0], 1, g.shape[1])
    norm_mix, q_lat_norm, kv_lat_norm = row(norm_mix), row(q_lat_norm), row(kv_lat_norm)
    norm_xattn, norm_mem, norm_ffn = row(norm_xattn), row(norm_mem), row(norm_ffn)
    w_a, w_b, w_o, w_q, w_kv, w_xo_b = bf(w_branch_a), bf(w_branch_b), bf(w_out), bf(w_xq), bf(w_xkv), bf(w_xo)
    w_gu, w_dn = bf(w_gate_up), bf(w_down)

    pos_lanes = jnp.broadcast_to(positions.reshape(t, 1).astype(F32), (t, LANES))
    tabs = _rope_tables(pos_lanes)
    mem_kv = _mem_kv(mem, norm_mem, w_kv)
    blk_of_row = np.arange(s) // MOBA_BLOCK
    e_mat = jnp.asarray(np.where(blk_of_row[:, None] == np.arange(LANES)[None, :], NEG_BIG, 0.0), BF16)

    xt = x.reshape(t, d)
    for layer in range(n_layers):
        qa, ka, va, qb, kb, vb, ga, gb = _mixer_in(
            xt, layer, norm_mix, w_in_r, q_lat_norm, w_uq_r, kv_lat_norm, w_ukv_r, tabs)
        seq3 = lambda a: a.reshape(b, s, a.shape[-1])
        oa = _moba(seq3(qa), seq3(ka), seq3(va), e_mat).reshape(t, A_WIDTH)
        ob = _mla(seq3(qb), seq3(kb), seq3(vb)).reshape(t, A_WIDTH)
        xt = _mixer_out(xt, layer, s, oa, ob, ga, gb, w_a, w_b, w_o, norm_xattn, w_q, mem_kv, w_xo_b)
        xt = _ffn(xt, layer, norm_ffn, w_gu, w_dn, norm_final.reshape(1, d), layer == n_layers - 1)
    return xt.reshape(b, s, d)
```

```python
import functools
import math

import numpy as np
import jax
import jax.numpy as jnp
from jax import lax
from jax.experimental import pallas as pl
from jax.experimental.pallas import tpu as pltpu

F32 = jnp.float32
BF16 = jnp.bfloat16

EPS = 1e-6
ROPE_THETA = 10000.0
LANES = 128
HEAD_DIM = 64
N_HEADS = 8
N_PAIRS = N_HEADS // 2
A_WIDTH = N_HEADS * HEAD_DIM
MLA_ROPE = 32
MLA_QK = HEAD_DIM + MLA_ROPE
Q_LORA = 384
KV_LORA = 256
MOBA_BLOCK = 256
MOBA_TOPK = 3
MAX_BLOCKS = 16
X_HEADS = 4
X_HEAD_DIM = 128
X_WIDTH = X_HEADS * X_HEAD_DIM
NEG_BIG = -1e30
LOG2E = math.log2(math.e)
VMEM_LIMIT = 56 * 1024 * 1024
TOK_TILE = 512
ATT_TILE = 512
TAB_W = 5 * LANES


def _resident(shape):
    return pl.BlockSpec(shape, lambda *_: (0,) * len(shape), pipeline_mode=pl.Buffered(1))


def _params(n_axes):
    return pltpu.CompilerParams(dimension_semantics=("arbitrary",) * n_axes,
                                vmem_limit_bytes=VMEM_LIMIT)


def _rms(x, g):
    return x * lax.rsqrt(jnp.mean(x * x, axis=-1, keepdims=True) + EPS) * g


def _dot(a, b):
    return jnp.dot(a, b, preferred_element_type=F32)


def _dot_nt(a, b):
    return lax.dot_general(a, b, (((1,), (1,)), ((), ())), preferred_element_type=F32)


def _rope_table_body(pos_ref, tab_ref):
    pos = pos_ref[...]
    lane = lax.broadcasted_iota(jnp.int32, pos.shape, 1)
    inv_a = jnp.exp((-math.log(ROPE_THETA) * (2.0 / HEAD_DIM)) * (lane % 32).astype(F32))
    ang_a = pos * inv_a
    tab_ref[:, 0:LANES] = jnp.cos(ang_a)
    tab_ref[:, LANES:2 * LANES] = jnp.where(lane < 64, -jnp.sin(ang_a), jnp.sin(ang_a))
    inv_b = jnp.exp((-math.log(ROPE_THETA) * (2.0 / MLA_ROPE)) * (lane % 16).astype(F32))
    ang_b = pos * inv_b
    cos_b, sin_b = jnp.cos(ang_b), jnp.sin(ang_b)
    tab_ref[:, 2 * LANES:3 * LANES] = jnp.where(lane < 64, 1.0, jnp.where(lane < 96, cos_b, 0.0))
    tab_ref[:, 3 * LANES:4 * LANES] = jnp.where((lane >= 64) & (lane < 80), -sin_b, 0.0)
    tab_ref[:, 4 * LANES:5 * LANES] = jnp.where((lane >= 80) & (lane < 96), sin_b, 0.0)


def _rope_tables(pos_lanes):
    t = pos_lanes.shape[0]
    tm = min(1024, t)
    return pl.pallas_call(
        _rope_table_body,
        out_shape=jax.ShapeDtypeStruct((t, TAB_W), F32),
        grid=(t // tm,),
        in_specs=[pl.BlockSpec((tm, LANES), lambda i: (i, 0))],
        out_specs=pl.BlockSpec((tm, TAB_W), lambda i: (i, 0)),
        compiler_params=_params(1),
        name="rope_tables",
    )(pos_lanes)


def _mem_kv_body(mem_ref, g_ref, w_ref, o_ref):
    h = _rms(mem_ref[...], g_ref[...]).astype(BF16)
    o_ref[...] = _dot(h, w_ref[...]).astype(BF16)


def _mem_kv(mem, norm_mem, w_xkv):
    b, m, d = mem.shape
    n_layers = w_xkv.shape[0]
    return pl.pallas_call(
        _mem_kv_body,
        out_shape=jax.ShapeDtypeStruct((n_layers, b, m, 2 * X_WIDTH), BF16),
        grid=(n_layers, b),
        in_specs=[pl.BlockSpec((None, m, d), lambda l, i: (i, 0, 0)),
                  pl.BlockSpec((None, 1, d), lambda l, i: (l, 0, 0)),
                  pl.BlockSpec((None, d, 2 * X_WIDTH), lambda l, i: (l, 0, 0))],
        out_specs=pl.BlockSpec((None, None, m, 2 * X_WIDTH), lambda l, i: (l, i, 0, 0)),
        compiler_params=_params(2),
        name="mem_kv",
    )(mem, norm_mem, w_xkv)


_C_QK = 0
_C_VA = 2 * A_WIDTH
_C_LAT = 3 * A_WIDTH
_C_G = _C_LAT + Q_LORA + KV_LORA + LANES
_KB_W = N_HEADS * LANES


def _mixer_in_body(x_ref, g_ref, win_ref, qg_ref, wuq_ref, kvg_ref, wukv_ref, tab_ref,
                   qa_ref, ka_ref, va_ref, qb_ref, kb_ref, vb_ref, ga_ref, gb_ref):
    d = x_ref.shape[-1]
    h = _rms(x_ref[...], g_ref[...]).astype(BF16)
    cos_a, sin_a = tab_ref[:, 0:LANES], tab_ref[:, LANES:2 * LANES]
    cos_b = tab_ref[:, 2 * LANES:3 * LANES]
    sin_lo, sin_hi = tab_ref[:, 3 * LANES:4 * LANES], tab_ref[:, 4 * LANES:5 * LANES]

    def rope_a(y):
        return y * cos_a + pltpu.roll(y, 64, axis=1) * sin_a

    def rope_b(y):
        return (y * cos_b + pltpu.roll(y, LANES - 16, axis=1) * sin_lo
                + pltpu.roll(y, 16, axis=1) * sin_hi)

    scale_a = HEAD_DIM ** -0.5 * LOG2E
    scale_b = MLA_QK ** -0.5 * LOG2E
    y = _dot(h, win_ref[:, _C_QK:_C_QK + 2 * A_WIDTH])
    for p in range(N_PAIRS):
        sl = slice(p * LANES, (p + 1) * LANES)
        qa_ref[:, sl] = (rope_a(y[:, sl]) * scale_a).astype(BF16)
        ka_ref[:, sl] = rope_a(y[:, A_WIDTH + p * LANES:A_WIDTH + (p + 1) * LANES]).astype(BF16)

    va_ref[...] = _dot(h, win_ref[:, _C_VA:_C_VA + A_WIDTH]).astype(BF16)

    lat = _dot(h, win_ref[:, _C_LAT:_C_G])
    q_lat = _rms(lat[:, 0:Q_LORA], qg_ref[...]).astype(BF16)
    kv_lat = _rms(lat[:, Q_LORA:Q_LORA + KV_LORA], kvg_ref[...]).astype(BF16)
    k_pe = rope_b(lat[:, Q_LORA + KV_LORA:])
    yq = _dot(q_lat, wuq_ref[...])
    ykv = _dot(kv_lat, wukv_ref[...])
    for hd in range(N_HEADS):
        sl = slice(hd * LANES, (hd + 1) * LANES)
        qb_ref[:, sl] = (rope_b(yq[:, sl]) * scale_b).astype(BF16)
        kb_ref[:, sl] = (ykv[:, sl] + k_pe).astype(BF16)
    vb_ref[...] = ykv[:, _KB_W:].astype(BF16)

    ga_ref[...] = jax.nn.sigmoid(_dot(h, win_ref[:, _C_G:_C_G + d])).astype(BF16)
    gb_ref[...] = jax.nn.sigmoid(_dot(h, win_ref[:, _C_G + d:_C_G + 2 * d])).astype(BF16)


def _mixer_in(x, norm_mix, w_in, q_lat_norm, w_uq, kv_lat_norm, w_ukv, tabs):
    t, d = x.shape
    tm = min(TOK_TILE, t)
    tok = lambda w: pl.BlockSpec((tm, w), lambda i: (i, 0))
    widths = (A_WIDTH, A_WIDTH, A_WIDTH, _KB_W, _KB_W, A_WIDTH, d, d)
    weights = (norm_mix, w_in, q_lat_norm, w_uq, kv_lat_norm, w_ukv)
    return pl.pallas_call(
        _mixer_in_body,
        out_shape=[jax.ShapeDtypeStruct((t, w), BF16) for w in widths],
        grid=(t // tm,),
        in_specs=[tok(d)] + [_resident(w.shape) for w in weights] + [tok(TAB_W)],
        out_specs=[tok(w) for w in widths],
        compiler_params=_params(1),
        name="mixer_in",
    )(x, *weights, tabs)


def _softmax_init(tq):
    return (jnp.full((tq, 1), NEG_BIG, F32), jnp.zeros((tq, LANES), F32))


def _softmax_step(s, v_ones, state):
    m, acc = state
    m_new = jnp.maximum(m, jnp.max(s, axis=-1, keepdims=True))
    p = jnp.exp2(s - m_new)
    acc_new = jnp.exp2(m - m_new) * acc + _dot(p.astype(BF16), v_ones)
    return m_new, acc_new


def _values_and_ones(v):
    lane = lax.broadcasted_iota(jnp.int32, v.shape, 1)
    one = jnp.ones_like(v)
    return jnp.where(lane < HEAD_DIM, v, one), jnp.where(lane < HEAD_DIM, one, v)


def _causal(s):
    row = lax.broadcasted_iota(jnp.int32, s.shape, 0)
    col = lax.broadcasted_iota(jnp.int32, s.shape, 1)
    return jnp.where(col <= row, s, NEG_BIG)


def _merge_pair(states):
    (_, acc0), (_, acc1) = states
    lane = lax.broadcasted_iota(jnp.int32, acc0.shape, 1)
    num = jnp.where(lane < HEAD_DIM, acc0, acc1)
    den = pltpu.roll(jnp.where(lane < HEAD_DIM, acc1, acc0), HEAD_DIM, axis=1)
    return num / den


def _moba_queries(q, kbar, qi, n_blocks):
    tq = q.shape[0]
    lane = lax.broadcasted_iota(jnp.int32, q.shape, 1)
    row = lax.broadcasted_iota(jnp.int32, (MAX_BLOCKS, tq), 0)
    col = lax.broadcasted_iota(jnp.int32, (MAX_BLOCKS, tq), 1)
    cur = qi * (tq // MOBA_BLOCK) + col // MOBA_BLOCK
    q_aug = []
    for hd in range(2):
        qh = jnp.where((lane % 64 < 32) == (hd == 0), q, jnp.zeros_like(q))
        gate = jnp.where(row < cur, _dot_nt(kbar, qh), -jnp.inf)
        rank = jnp.zeros(gate.shape, jnp.int32)
        for n in range(n_blocks):
            gn = gate[n:n + 1, :]
            beats = (gn > gate) | ((gn == gate) & (row > n))
            rank = rank + beats.astype(jnp.int32)
        keep = ((row < cur) & (rank < MOBA_TOPK)) | (row == cur)
        dropped = jnp.where(keep, 0.0, 1.0)
        dropped = jnp.concatenate([dropped, jnp.zeros((LANES - MAX_BLOCKS, tq), F32)], axis=0)
        q_aug.append(jnp.concatenate([qh, dropped.T.astype(BF16)], axis=1))
    return q_aug


def _attn_body(qa_ref, ka_ref, va_ref, e_ref, qb_ref, kb_ref, vb_ref, oa_ref, ob_ref):
    tq = ATT_TILE
    seq = ka_ref.shape[0]
    n_blocks = seq // MOBA_BLOCK
    kbar = [jnp.mean(ka_ref[n * MOBA_BLOCK:(n + 1) * MOBA_BLOCK, :].astype(F32), axis=0, keepdims=True)
            for n in range(n_blocks)]
    kbar = jnp.concatenate(kbar + [jnp.zeros((MAX_BLOCKS - n_blocks, LANES), F32)], axis=0).astype(BF16)

    def tile(qa, qb, j, states, diagonal):
        mask = _causal if diagonal else (lambda s: s)
        rows = slice(j * tq, (j + 1) * tq)
        ka_aug = jnp.concatenate([ka_ref[rows, :], e_ref[rows, :]], axis=1)
        va_ones = _values_and_ones(va_ref[rows, :])
        vb_ones = _values_and_ones(vb_ref[rows, :])
        new = []
        for hd in range(2):
            new.append(_softmax_step(mask(_dot_nt(qa[hd], ka_aug)), va_ones[hd], states[hd]))
        for hd in range(2):
            s = _dot_nt(qb[hd], kb_ref[rows, hd * LANES:(hd + 1) * LANES])
            new.append(_softmax_step(mask(s), vb_ones[hd], states[2 + hd]))
        return tuple(new)

    for qi in range(seq // tq):
        qrows = slice(qi * tq, (qi + 1) * tq)
        qa = _moba_queries(qa_ref[qrows, :], kbar, qi, n_blocks)
        qb = [qb_ref[qrows, hd * LANES:(hd + 1) * LANES] for hd in range(2)]
        states = tile(qa, qb, qi, (_softmax_init(tq),) * 4, True)
        for j in range(qi):
            states = tile(qa, qb, j, states, False)
        oa_ref[qrows, :] = _merge_pair(states[:2]).astype(oa_ref.dtype)
        ob_ref[qrows, :] = _merge_pair(states[2:]).astype(ob_ref.dtype)


def _attention(qa, ka, va, e_mat, qb, kb, vb):
    b, s, _ = qa.shape
    seq_spec = lambda w: pl.BlockSpec((None, s, w), lambda i, p: (i, 0, p))
    return pl.pallas_call(
        _attn_body,
        out_shape=[jax.ShapeDtypeStruct((b, s, A_WIDTH), BF16)] * 2,
        grid=(b, N_PAIRS),
        in_specs=[seq_spec(LANES), seq_spec(LANES), seq_spec(LANES), _resident(e_mat.shape),
                  seq_spec(2 * LANES), seq_spec(2 * LANES), seq_spec(LANES)],
        out_specs=[seq_spec(LANES)] * 2,
        compiler_params=_params(2),
        name="attention",
    )(qa, ka, va, e_mat, qb, kb, vb)


def _mixer_out_body(x_ref, oa_ref, ob_ref, ga_ref, gb_ref, kv_ref, wa_ref, wb_ref, wo_ref,
                    gx_ref, wxq_ref, wxo_ref, o_ref):
    merged = (ga_ref[...].astype(F32) * _dot(oa_ref[...], wa_ref[...])
              + gb_ref[...].astype(F32) * _dot(ob_ref[...], wb_ref[...]))
    x1 = x_ref[...] + _dot(merged.astype(BF16), wo_ref[...])

    q = _dot(_rms(x1, gx_ref[...]).astype(BF16), wxq_ref[...]).astype(BF16)
    scale = X_HEAD_DIM ** -0.5
    heads = []
    for hd in range(X_HEADS):
        lanes = slice(hd * X_HEAD_DIM, (hd + 1) * X_HEAD_DIM)
        s = _dot_nt(q[:, lanes], kv_ref[:, lanes])
        m = jnp.max(s, axis=-1, keepdims=True)
        p = jnp.exp((s - m) * scale)
        l = jnp.sum(p, axis=-1, keepdims=True)
        v = kv_ref[:, X_WIDTH + hd * X_HEAD_DIM:X_WIDTH + (hd + 1) * X_HEAD_DIM]
        heads.append((_dot(p.astype(BF16), v) * (1.0 / l)).astype(BF16))
    o_ref[...] = x1 + _dot(jnp.concatenate(heads, axis=1), wxo_ref[...])


def _mixer_out(x, seq, oa, ob, ga, gb, mem_kv, w_a, w_b, w_out, norm_xattn, w_xq, w_xo):
    t, d = x.shape
    tm = min(TOK_TILE, seq)
    tiles_per_seq = seq // tm
    m = mem_kv.shape[1]
    tok = lambda w: pl.BlockSpec((tm, w), lambda i: (i, 0))
    weights = (w_a, w_b, w_out, norm_xattn, w_xq, w_xo)
    return pl.pallas_call(
        _mixer_out_body,
        out_shape=jax.ShapeDtypeStruct((t, d), F32),
        grid=(t // tm,),
        in_specs=[tok(d), tok(A_WIDTH), tok(A_WIDTH), tok(d), tok(d),
                  pl.BlockSpec((None, m, 2 * X_WIDTH), lambda i: (i // tiles_per_seq, 0, 0))]
                 + [_resident(w.shape) for w in weights],
        out_specs=tok(d),
        compiler_params=_params(1),
        name="mixer_out",
    )(x, oa, ob, ga, gb, mem_kv, *weights)


FF_CHUNK = 1024


def _ffn_body(x_ref, g_ref, wgu_ref, wd_ref, gf_ref, o_ref, *, final):
    x = x_ref[...]
    h = _rms(x, g_ref[...]).astype(BF16)
    d_ff = wd_ref.shape[0]
    y = x
    for c0 in range(0, d_ff, FF_CHUNK):
        c1 = min(c0 + FF_CHUNK, d_ff)
        gt = _dot(h, wgu_ref[:, c0:c1])
        up = _dot(h, wgu_ref[:, d_ff + c0:d_ff + c1])
        a = (gt * jax.nn.sigmoid(gt) * up).astype(BF16)
        y = y + _dot(a, wd_ref[c0:c1, :])
    o_ref[...] = _rms(y, gf_ref[...]) if final else y


def _ffn(x, norm_ffn, w_gate_up, w_down, norm_final, final):
    t, d = x.shape
    tm = min(TOK_TILE, t)
    tok = pl.BlockSpec((tm, d), lambda i: (i, 0))
    weights = (norm_ffn, w_gate_up, w_down, norm_final)
    return pl.pallas_call(
        functools.partial(_ffn_body, final=final),
        out_shape=jax.ShapeDtypeStruct((t, d), F32),
        grid=(t // tm,),
        in_specs=[tok] + [_resident(w.shape) for w in weights],
        out_specs=tok,
        compiler_params=_params(1),
        name="ffn_final" if final else "ffn",
    )(x, *weights)


def _pair_rope_perm():
    perm = []
    for p in range(N_PAIRS):
        for half in range(2):
            for hd in (2 * p, 2 * p + 1):
                perm.extend(hd * HEAD_DIM + half * 32 + np.arange(32))
    return np.asarray(perm, np.int32)


def _prep_weights(w_in, w_uq, w_ukv):
    n_layers, d, _ = w_in.shape
    perm = _pair_rope_perm()
    c = np.cumsum([A_WIDTH, A_WIDTH, A_WIDTH, Q_LORA, KV_LORA, MLA_ROPE, d])
    w_kpe = jnp.pad(w_in[..., c[4]:c[5]], ((0, 0), (0, 0), (HEAD_DIM, LANES - MLA_QK)))
    w_in_r = jnp.concatenate(
        [w_in[..., 0:c[0]][..., perm], w_in[..., c[0]:c[1]][..., perm], w_in[..., c[1]:c[4]],
         w_kpe, w_in[..., c[5]:]], axis=-1).astype(BF16)
    w_uq_r = jnp.pad(w_uq.reshape(n_layers, Q_LORA, N_HEADS, MLA_QK),
                     ((0, 0), (0, 0), (0, 0), (0, LANES - MLA_QK))).reshape(n_layers, Q_LORA, _KB_W).astype(BF16)
    ukv = w_ukv.reshape(n_layers, KV_LORA, N_HEADS, 2 * HEAD_DIM)
    w_k = jnp.pad(ukv[..., :HEAD_DIM], ((0, 0), (0, 0), (0, 0), (0, LANES - HEAD_DIM))).reshape(n_layers, KV_LORA, _KB_W)
    w_v = ukv[..., HEAD_DIM:].reshape(n_layers, KV_LORA, A_WIDTH)
    w_ukv_r = jnp.concatenate([w_k, w_v], axis=-1).astype(BF16)
    return w_in_r, w_uq_r, w_ukv_r


def kernel(x, mem, positions, norm_mix, w_in, q_lat_norm, w_uq, kv_lat_norm, w_ukv, w_branch_a, w_branch_b, w_out, norm_xattn, norm_mem, w_xq, w_xkv, w_xo, norm_ffn, w_gate_up, w_down, norm_final):
    b, s, d = x.shape
    n_layers = w_in.shape[0]
    t = b * s
    assert s % ATT_TILE == 0 and ATT_TILE % MOBA_BLOCK == 0 and s // MOBA_BLOCK <= MAX_BLOCKS

    w_in_r, w_uq_r, w_ukv_r = _prep_weights(w_in, w_uq, w_ukv)
    bf = lambda w: w.astype(BF16)
    row = lambda g: g.reshape(g.shape[0], 1, g.shape[1])
    pos_lanes = jnp.broadcast_to(positions.reshape(t, 1).astype(F32), (t, LANES))
    tabs = _rope_tables(pos_lanes)
    mem_kv = _mem_kv(mem, row(norm_mem), bf(w_xkv))
    blk_of_row = np.arange(s) // MOBA_BLOCK
    e_mat = jnp.asarray(np.where(blk_of_row[:, None] == np.arange(LANES)[None, :], NEG_BIG, 0.0), BF16)
    norm_final = norm_final.reshape(1, d)

    per_layer = (row(norm_mix), w_in_r, row(q_lat_norm), w_uq_r, row(kv_lat_norm), w_ukv_r,
                 bf(w_branch_a), bf(w_branch_b), bf(w_out), row(norm_xattn), bf(w_xq), bf(w_xo),
                 row(norm_ffn), bf(w_gate_up), bf(w_down), mem_kv)

    def layer(xt, weights, final):
        (g_mix, w_i, g_q, w_q, g_kv, w_kv, w_a, w_b, w_o, g_x, w_xq_l, w_xo_l, g_f, w_gu, w_dn, kv_l) = weights
        qa, ka, va, qb, kb, vb, ga, gb = _mixer_in(xt, g_mix, w_i, g_q, w_q, g_kv, w_kv, tabs)
        seq3 = lambda a: a.reshape(b, s, a.shape[-1])
        oa, ob = _attention(seq3(qa), seq3(ka), seq3(va), e_mat, seq3(qb), seq3(kb), seq3(vb))
        oa, ob = oa.reshape(t, A_WIDTH), ob.reshape(t, A_WIDTH)
        xt = _mixer_out(xt, s, oa, ob, ga, gb, kv_l, w_a, w_b, w_o, g_x, w_xq_l, w_xo_l)
        return _ffn(xt, g_f, w_gu, w_dn, norm_final, final)

    xt = x.reshape(t, d)
    if n_layers > 1:
        xt, _ = lax.scan(lambda c, w: (layer(c, w, False), None), xt,
                         jax.tree.map(lambda w: w[:-1], per_layer))
    xt = layer(xt, jax.tree.map(lambda w: w[-1], per_layer), True)
    return xt.reshape(b, s, d)
```

```python
import functools
import math

import numpy as np
import jax
import jax.numpy as jnp
from jax import lax
from jax.experimental import pallas as pl
from jax.experimental.pallas import tpu as pltpu

F32 = jnp.float32
BF16 = jnp.bfloat16

EPS = 1e-6
ROPE_THETA = 10000.0
LANES = 128
HEAD_DIM = 64
N_HEADS = 8
N_PAIRS = N_HEADS // 2
A_WIDTH = N_HEADS * HEAD_DIM
MLA_ROPE = 32
MLA_QK = HEAD_DIM + MLA_ROPE
Q_LORA = 384
KV_LORA = 256
MOBA_BLOCK = 256
MOBA_TOPK = 3
MAX_BLOCKS = 16
X_HEADS = 4
X_HEAD_DIM = 128
X_WIDTH = X_HEADS * X_HEAD_DIM
NEG_BIG = -1e30
LOG2E = math.log2(math.e)
VMEM_LIMIT = 56 * 1024 * 1024
TOK_TILE = 512
ATT_TILE = 512
TAB_W = 5 * LANES


def _resident(shape):
    return pl.BlockSpec(shape, lambda *_: (0,) * len(shape), pipeline_mode=pl.Buffered(1))


def _params(n_axes):
    return pltpu.CompilerParams(dimension_semantics=("arbitrary",) * n_axes,
                                vmem_limit_bytes=VMEM_LIMIT)


def _rms(x, g):
    return x * lax.rsqrt(jnp.mean(x * x, axis=-1, keepdims=True) + EPS) * g


def _dot(a, b):
    return jnp.dot(a, b, preferred_element_type=F32)


def _dot_nt(a, b):
    return lax.dot_general(a, b, (((1,), (1,)), ((), ())), preferred_element_type=F32)


def _rope_table_body(pos_ref, tab_ref):
    pos = pos_ref[...]
    lane = lax.broadcasted_iota(jnp.int32, pos.shape, 1)
    inv_a = jnp.exp((-math.log(ROPE_THETA) * (2.0 / HEAD_DIM)) * (lane % 32).astype(F32))
    ang_a = pos * inv_a
    tab_ref[:, 0:LANES] = jnp.cos(ang_a)
    tab_ref[:, LANES:2 * LANES] = jnp.where(lane < 64, -jnp.sin(ang_a), jnp.sin(ang_a))
    inv_b = jnp.exp((-math.log(ROPE_THETA) * (2.0 / MLA_ROPE)) * (lane % 16).astype(F32))
    ang_b = pos * inv_b
    cos_b, sin_b = jnp.cos(ang_b), jnp.sin(ang_b)
    tab_ref[:, 2 * LANES:3 * LANES] = jnp.where(lane < 64, 1.0, jnp.where(lane < 96, cos_b, 0.0))
    tab_ref[:, 3 * LANES:4 * LANES] = jnp.where((lane >= 64) & (lane < 80), -sin_b, 0.0)
    tab_ref[:, 4 * LANES:5 * LANES] = jnp.where((lane >= 80) & (lane < 96), sin_b, 0.0)


def _rope_tables(pos_lanes):
    t = pos_lanes.shape[0]
    tm = min(1024, t)
    return pl.pallas_call(
        _rope_table_body,
        out_shape=jax.ShapeDtypeStruct((t, TAB_W), F32),
        grid=(t // tm,),
        in_specs=[pl.BlockSpec((tm, LANES), lambda i: (i, 0))],
        out_specs=pl.BlockSpec((tm, TAB_W), lambda i: (i, 0)),
        compiler_params=_params(1),
        name="rope_tables",
    )(pos_lanes)


def _mem_kv_body(mem_ref, g_ref, w_ref, o_ref):
    h = _rms(mem_ref[...], g_ref[...]).astype(BF16)
    o_ref[...] = _dot(h, w_ref[...]).astype(BF16)


def _mem_kv(mem, norm_mem, w_xkv):
    b, m, d = mem.shape
    n_layers = w_xkv.shape[0]
    return pl.pallas_call(
        _mem_kv_body,
        out_shape=jax.ShapeDtypeStruct((n_layers, b, m, 2 * X_WIDTH), BF16),
        grid=(n_layers, b),
        in_specs=[pl.BlockSpec((None, m, d), lambda l, i: (i, 0, 0)),
                  pl.BlockSpec((None, 1, d), lambda l, i: (l, 0, 0)),
                  pl.BlockSpec((None, d, 2 * X_WIDTH), lambda l, i: (l, 0, 0))],
        out_specs=pl.BlockSpec((None, None, m, 2 * X_WIDTH), lambda l, i: (l, i, 0, 0)),
        compiler_params=_params(2),
        name="mem_kv",
    )(mem, norm_mem, w_xkv)


_C_QK = 0
_C_VA = 2 * A_WIDTH
_C_LAT = 3 * A_WIDTH
_C_G = _C_LAT + Q_LORA + KV_LORA + LANES
_KB_W = N_HEADS * LANES


def _mixer_in_body(x_ref, g_ref, win_ref, qg_ref, wuq_ref, kvg_ref, wukv_ref, tab_ref,
                   qa_ref, ka_ref, va_ref, qb_ref, kb_ref, vb_ref, ga_ref, gb_ref):
    d = x_ref.shape[-1]
    h = _rms(x_ref[...], g_ref[...]).astype(BF16)
    cos_a, sin_a = tab_ref[:, 0:LANES], tab_ref[:, LANES:2 * LANES]
    cos_b = tab_ref[:, 2 * LANES:3 * LANES]
    sin_lo, sin_hi = tab_ref[:, 3 * LANES:4 * LANES], tab_ref[:, 4 * LANES:5 * LANES]

    def rope_a(y):
        return y * cos_a + pltpu.roll(y, 64, axis=1) * sin_a

    def rope_b(y):
        return (y * cos_b + pltpu.roll(y, LANES - 16, axis=1) * sin_lo
                + pltpu.roll(y, 16, axis=1) * sin_hi)

    scale_a = HEAD_DIM ** -0.5 * LOG2E
    scale_b = MLA_QK ** -0.5 * LOG2E
    y = _dot(h, win_ref[:, _C_QK:_C_QK + 2 * A_WIDTH])
    for p in range(N_PAIRS):
        sl = slice(p * LANES, (p + 1) * LANES)
        qa_ref[:, sl] = (rope_a(y[:, sl]) * scale_a).astype(BF16)
        ka_ref[:, sl] = rope_a(y[:, A_WIDTH + p * LANES:A_WIDTH + (p + 1) * LANES]).astype(BF16)

    va_ref[...] = _dot(h, win_ref[:, _C_VA:_C_VA + A_WIDTH]).astype(BF16)

    lat = _dot(h, win_ref[:, _C_LAT:_C_G])
    q_lat = _rms(lat[:, 0:Q_LORA], qg_ref[...]).astype(BF16)
    kv_lat = _rms(lat[:, Q_LORA:Q_LORA + KV_LORA], kvg_ref[...]).astype(BF16)
    k_pe = rope_b(lat[:, Q_LORA + KV_LORA:])
    yq = _dot(q_lat, wuq_ref[...])
    ykv = _dot(kv_lat, wukv_ref[...])
    for hd in range(N_HEADS):
        sl = slice(hd * LANES, (hd + 1) * LANES)
        qb_ref[:, sl] = (rope_b(yq[:, sl]) * scale_b).astype(BF16)
        kb_ref[:, sl] = (ykv[:, sl] + k_pe).astype(BF16)
    vb_ref[...] = ykv[:, _KB_W:].astype(BF16)

    ga_ref[...] = jax.nn.sigmoid(_dot(h, win_ref[:, _C_G:_C_G + d])).astype(BF16)
    gb_ref[...] = jax.nn.sigmoid(_dot(h, win_ref[:, _C_G + d:_C_G + 2 * d])).astype(BF16)


def _mixer_in(x, norm_mix, w_in, q_lat_norm, w_uq, kv_lat_norm, w_ukv, tabs):
    t, d = x.shape
    tm = min(TOK_TILE, t)
    tok = lambda w: pl.BlockSpec((tm, w), lambda i: (i, 0))
    widths = (A_WIDTH, A_WIDTH, A_WIDTH, _KB_W, _KB_W, A_WIDTH, d, d)
    weights = (norm_mix, w_in, q_lat_norm, w_uq, kv_lat_norm, w_ukv)
    return pl.pallas_call(
        _mixer_in_body,
        out_shape=[jax.ShapeDtypeStruct((t, w), BF16) for w in widths],
        grid=(t // tm,),
        in_specs=[tok(d)] + [_resident(w.shape) for w in weights] + [tok(TAB_W)],
        out_specs=[tok(w) for w in widths],
        compiler_params=_params(1),
        name="mixer_in",
    )(x, *weights, tabs)


def _softmax_init(tq):
    return (jnp.full((tq, 1), NEG_BIG, F32), jnp.zeros((tq, LANES), F32))


def _softmax_step(s, v_ones, state):
    m, acc = state
    m_new = jnp.maximum(m, jnp.max(s, axis=-1, keepdims=True))
    p = jnp.exp2(s - m_new)
    acc_new = jnp.exp2(m - m_new) * acc + _dot(p.astype(BF16), v_ones)
    return m_new, acc_new


def _values_and_ones(v):
    lane = lax.broadcasted_iota(jnp.int32, v.shape, 1)
    one = jnp.ones_like(v)
    return jnp.where(lane < HEAD_DIM, v, one), jnp.where(lane < HEAD_DIM, one, v)


def _causal(s, q0, k0):
    row = lax.broadcasted_iota(jnp.int32, s.shape, 0) + q0
    col = lax.broadcasted_iota(jnp.int32, s.shape, 1) + k0
    return jnp.where(col <= row, s, NEG_BIG)


def _merge_pair(states):
    (_, acc0), (_, acc1) = states
    lane = lax.broadcasted_iota(jnp.int32, acc0.shape, 1)
    num = jnp.where(lane < HEAD_DIM, acc0, acc1)
    den = pltpu.roll(jnp.where(lane < HEAD_DIM, acc1, acc0), HEAD_DIM, axis=1)
    return num / den


def _moba_queries(q, kbar, qi, n_blocks):
    tq = q.shape[0]
    lane = lax.broadcasted_iota(jnp.int32, q.shape, 1)
    row = lax.broadcasted_iota(jnp.int32, (MAX_BLOCKS, tq), 0)
    col = lax.broadcasted_iota(jnp.int32, (MAX_BLOCKS, tq), 1)
    cur = qi * (tq // MOBA_BLOCK) + col // MOBA_BLOCK
    q_aug = []
    for hd in range(2):
        qh = jnp.where((lane % 64 < 32) == (hd == 0), q, jnp.zeros_like(q))
        gate = jnp.where(row < cur, _dot_nt(kbar, qh), -jnp.inf)
        rank = jnp.zeros(gate.shape, jnp.int32)
        for n in range(n_blocks):
            gn = gate[n:n + 1, :]
            beats = (gn > gate) | ((gn == gate) & (row > n))
            rank = rank + beats.astype(jnp.int32)
        keep = ((row < cur) & (rank < MOBA_TOPK)) | (row == cur)
        dropped = jnp.where(keep, 0.0, 1.0)
        dropped = jnp.concatenate([dropped, jnp.zeros((LANES - MAX_BLOCKS, tq), F32)], axis=0)
        q_aug.append(jnp.concatenate([qh, dropped.T.astype(BF16)], axis=1))
    return q_aug


def _attn_body(qa_ref, ka_ref, va_ref, e_ref, qb_ref, kb_ref, vb_ref, oa_ref, ob_ref):
    tq = ATT_TILE
    seq = ka_ref.shape[0]
    n_blocks = seq // MOBA_BLOCK
    kbar = [jnp.mean(ka_ref[n * MOBA_BLOCK:(n + 1) * MOBA_BLOCK, :].astype(F32), axis=0, keepdims=True)
            for n in range(n_blocks)]
    kbar = jnp.concatenate(kbar + [jnp.zeros((MAX_BLOCKS - n_blocks, LANES), F32)], axis=0).astype(BF16)

    def tile(qa, qb, j, states, q0, causal):
        rows = slice(j * tq, (j + 1) * tq)
        ka_aug = jnp.concatenate([ka_ref[rows, :], e_ref[rows, :]], axis=1)
        va_ones = _values_and_ones(va_ref[rows, :])
        vb_ones = _values_and_ones(vb_ref[rows, :])
        mask = functools.partial(_causal, q0=q0, k0=j * tq) if causal else (lambda s: s)
        new = []
        for hd in range(2):
            new.append(_softmax_step(mask(_dot_nt(qa[hd], ka_aug)), va_ones[hd], states[hd]))
        for hd in range(2):
            s = _dot_nt(qb[hd], kb_ref[rows, hd * LANES:(hd + 1) * LANES])
            new.append(_softmax_step(mask(s), vb_ones[hd], states[2 + hd]))
        return tuple(new)

    tq2 = 2 * tq
    for t in range(seq // tq2):
        qrows = slice(t * tq2, (t + 1) * tq2)
        qa = _moba_queries(qa_ref[qrows, :], kbar, t, n_blocks)
        qb = [qb_ref[qrows, hd * LANES:(hd + 1) * LANES] for hd in range(2)]
        states = (_softmax_init(tq2),) * 4
        for j in range(2 * t + 1):
            states = tile(qa, qb, j, states, t * tq2, j == 2 * t)
        low = tuple((m[tq:], acc[tq:]) for m, acc in states)
        low = tile([q[tq:] for q in qa], [q[tq:] for q in qb], 2 * t + 1, low, t * tq2 + tq, True)
        states = tuple((jnp.concatenate([m[:tq], ml], axis=0), jnp.concatenate([acc[:tq], accl], axis=0))
                       for (m, acc), (ml, accl) in zip(states, low))
        oa_ref[qrows, :] = _merge_pair(states[:2]).astype(oa_ref.dtype)
        ob_ref[qrows, :] = _merge_pair(states[2:]).astype(ob_ref.dtype)


def _attention(qa, ka, va, e_mat, qb, kb, vb):
    b, s, _ = qa.shape
    seq_spec = lambda w: pl.BlockSpec((None, s, w), lambda i, p: (i, 0, p))
    return pl.pallas_call(
        _attn_body,
        out_shape=[jax.ShapeDtypeStruct((b, s, A_WIDTH), BF16)] * 2,
        grid=(b, N_PAIRS),
        in_specs=[seq_spec(LANES), seq_spec(LANES), seq_spec(LANES), _resident(e_mat.shape),
                  seq_spec(2 * LANES), seq_spec(2 * LANES), seq_spec(LANES)],
        out_specs=[seq_spec(LANES)] * 2,
        compiler_params=_params(2),
        name="attention",
    )(qa, ka, va, e_mat, qb, kb, vb)


def _mixer_out_body(x_ref, oa_ref, ob_ref, ga_ref, gb_ref, kv_ref, wa_ref, wb_ref, wo_ref,
                    gx_ref, wxq_ref, wxo_ref, o_ref):
    merged = (ga_ref[...].astype(F32) * _dot(oa_ref[...], wa_ref[...])
              + gb_ref[...].astype(F32) * _dot(ob_ref[...], wb_ref[...]))
    x1 = x_ref[...] + _dot(merged.astype(BF16), wo_ref[...])

    q = _dot(_rms(x1, gx_ref[...]).astype(BF16), wxq_ref[...]).astype(BF16)
    scale = X_HEAD_DIM ** -0.5
    heads = []
    for hd in range(X_HEADS):
        lanes = slice(hd * X_HEAD_DIM, (hd + 1) * X_HEAD_DIM)
        s = _dot_nt(q[:, lanes], kv_ref[:, lanes])
        m = jnp.max(s, axis=-1, keepdims=True)
        p = jnp.exp((s - m) * scale)
        l = jnp.sum(p, axis=-1, keepdims=True)
        v = kv_ref[:, X_WIDTH + hd * X_HEAD_DIM:X_WIDTH + (hd + 1) * X_HEAD_DIM]
        heads.append((_dot(p.astype(BF16), v) * (1.0 / l)).astype(BF16))
    o_ref[...] = x1 + _dot(jnp.concatenate(heads, axis=1), wxo_ref[...])


def _mixer_out(x, seq, oa, ob, ga, gb, mem_kv, w_a, w_b, w_out, norm_xattn, w_xq, w_xo):
    t, d = x.shape
    tm = min(TOK_TILE, seq)
    tiles_per_seq = seq // tm
    m = mem_kv.shape[1]
    tok = lambda w: pl.BlockSpec((tm, w), lambda i: (i, 0))
    weights = (w_a, w_b, w_out, norm_xattn, w_xq, w_xo)
    return pl.pallas_call(
        _mixer_out_body,
        out_shape=jax.ShapeDtypeStruct((t, d), F32),
        grid=(t // tm,),
        in_specs=[tok(d), tok(A_WIDTH), tok(A_WIDTH), tok(d), tok(d),
                  pl.BlockSpec((None, m, 2 * X_WIDTH), lambda i: (i // tiles_per_seq, 0, 0))]
                 + [_resident(w.shape) for w in weights],
        out_specs=tok(d),
        compiler_params=_params(1),
        name="mixer_out",
    )(x, oa, ob, ga, gb, mem_kv, *weights)


FF_CHUNK = 1024


def _ffn_body(x_ref, g_ref, wgu_ref, wd_ref, gf_ref, o_ref, *, final):
    x = x_ref[...]
    h = _rms(x, g_ref[...]).astype(BF16)
    d_ff = wd_ref.shape[0]
    y = x
    for c0 in range(0, d_ff, FF_CHUNK):
        c1 = min(c0 + FF_CHUNK, d_ff)
        gt = _dot(h, wgu_ref[:, c0:c1])
        up = _dot(h, wgu_ref[:, d_ff + c0:d_ff + c1])
        a = (gt * jax.nn.sigmoid(gt) * up).astype(BF16)
        y = y + _dot(a, wd_ref[c0:c1, :])
    o_ref[...] = _rms(y, gf_ref[...]) if final else y


def _ffn(x, norm_ffn, w_gate_up, w_down, norm_final, final):
    t, d = x.shape
    tm = min(TOK_TILE, t)
    tok = pl.BlockSpec((tm, d), lambda i: (i, 0))
    weights = (norm_ffn, w_gate_up, w_down, norm_final)
    return pl.pallas_call(
        functools.partial(_ffn_body, final=final),
        out_shape=jax.ShapeDtypeStruct((t, d), F32),
        grid=(t // tm,),
        in_specs=[tok] + [_resident(w.shape) for w in weights],
        out_specs=tok,
        compiler_params=_params(1),
        name="ffn_final" if final else "ffn",
    )(x, *weights)


def _pair_rope_perm():
    perm = []
    for p in range(N_PAIRS):
        for half in range(2):
            for hd in (2 * p, 2 * p + 1):
                perm.extend(hd * HEAD_DIM + half * 32 + np.arange(32))
    return np.asarray(perm, np.int32)


def _prep_weights(w_in, w_uq, w_ukv):
    n_layers, d, _ = w_in.shape
    perm = _pair_rope_perm()
    c = np.cumsum([A_WIDTH, A_WIDTH, A_WIDTH, Q_LORA, KV_LORA, MLA_ROPE, d])
    w_kpe = jnp.pad(w_in[..., c[4]:c[5]], ((0, 0), (0, 0), (HEAD_DIM, LANES - MLA_QK)))
    w_in_r = jnp.concatenate(
        [w_in[..., 0:c[0]][..., perm], w_in[..., c[0]:c[1]][..., perm], w_in[..., c[1]:c[4]],
         w_kpe, w_in[..., c[5]:]], axis=-1).astype(BF16)
    w_uq_r = jnp.pad(w_uq.reshape(n_layers, Q_LORA, N_HEADS, MLA_QK),
                     ((0, 0), (0, 0), (0, 0), (0, LANES - MLA_QK))).reshape(n_layers, Q_LORA, _KB_W).astype(BF16)
    ukv = w_ukv.reshape(n_layers, KV_LORA, N_HEADS, 2 * HEAD_DIM)
    w_k = jnp.pad(ukv[..., :HEAD_DIM], ((0, 0), (0, 0), (0, 0), (0, LANES - HEAD_DIM))).reshape(n_layers, KV_LORA, _KB_W)
    w_v = ukv[..., HEAD_DIM:].reshape(n_layers, KV_LORA, A_WIDTH)
    w_ukv_r = jnp.concatenate([w_k, w_v], axis=-1).astype(BF16)
    return w_in_r, w_uq_r, w_ukv_r


def kernel(x, mem, positions, norm_mix, w_in, q_lat_norm, w_uq, kv_lat_norm, w_ukv, w_branch_a, w_branch_b, w_out, norm_xattn, norm_mem, w_xq, w_xkv, w_xo, norm_ffn, w_gate_up, w_down, norm_final):
    b, s, d = x.shape
    n_layers = w_in.shape[0]
    t = b * s
    assert s % (2 * ATT_TILE) == 0 and ATT_TILE % MOBA_BLOCK == 0 and s // MOBA_BLOCK <= MAX_BLOCKS

    w_in_r, w_uq_r, w_ukv_r = _prep_weights(w_in, w_uq, w_ukv)
    bf = lambda w: w.astype(BF16)
    row = lambda g: g.reshape(g.shape[0], 1, g.shape[1])
    pos_lanes = jnp.broadcast_to(positions.reshape(t, 1).astype(F32), (t, LANES))
    tabs = _rope_tables(pos_lanes)
    mem_kv = _mem_kv(mem, row(norm_mem), bf(w_xkv))
    blk_of_row = np.arange(s) // MOBA_BLOCK
    e_mat = jnp.asarray(np.where(blk_of_row[:, None] == np.arange(LANES)[None, :], NEG_BIG, 0.0), BF16)
    norm_final = norm_final.reshape(1, d)

    per_layer = (row(norm_mix), w_in_r, row(q_lat_norm), w_uq_r, row(kv_lat_norm), w_ukv_r,
                 bf(w_branch_a), bf(w_branch_b), bf(w_out), row(norm_xattn), bf(w_xq), bf(w_xo),
                 row(norm_ffn), bf(w_gate_up), bf(w_down), mem_kv)

    def layer(xt, weights, final):
        (g_mix, w_i, g_q, w_q, g_kv, w_kv, w_a, w_b, w_o, g_x, w_xq_l, w_xo_l, g_f, w_gu, w_dn, kv_l) = weights
        qa, ka, va, qb, kb, vb, ga, gb = _mixer_in(xt, g_mix, w_i, g_q, w_q, g_kv, w_kv, tabs)
        seq3 = lambda a: a.reshape(b, s, a.shape[-1])
        oa, ob = _attention(seq3(qa), seq3(ka), seq3(va), e_mat, seq3(qb), seq3(kb), seq3(vb))
        oa, ob = oa.reshape(t, A_WIDTH), ob.reshape(t, A_WIDTH)
        xt = _mixer_out(xt, s, oa, ob, ga, gb, kv_l, w_a, w_b, w_o, g_x, w_xq_l, w_xo_l)
        return _ffn(xt, g_f, w_gu, w_dn, norm_final, final)

    xt = x.reshape(t, d)
    if n_layers > 1:
        xt, _ = lax.scan(lambda c, w: (layer(c, w, False), None), xt,
                         jax.tree.map(lambda w: w[:-1], per_layer))
    xt = layer(xt, jax.tree.map(lambda w: w[-1], per_layer), True)
    return xt.reshape(b, s, d)
```

```python
import functools
import math

import numpy as np
import jax
import jax.numpy as jnp
from jax import lax
from jax.experimental import pallas as pl
from jax.experimental.pallas import tpu as pltpu

F32 = jnp.float32
BF16 = jnp.bfloat16

EPS = 1e-6
ROPE_THETA = 10000.0
LANES = 128
HEAD_DIM = 64
N_HEADS = 8
N_PAIRS = N_HEADS // 2
A_WIDTH = N_HEADS * HEAD_DIM
MLA_ROPE = 32
MLA_QK = HEAD_DIM + MLA_ROPE
Q_LORA = 384
KV_LORA = 256
MOBA_BLOCK = 256
MOBA_TOPK = 3
MAX_BLOCKS = 16
X_HEADS = 4
X_HEAD_DIM = 128
X_WIDTH = X_HEADS * X_HEAD_DIM
NEG_BIG = -1e30
LOG2E = math.log2(math.e)
VMEM_LIMIT = 56 * 1024 * 1024
TOK_TILE = 512
ATT_TILE = 512
TAB_W = 5 * LANES


def _resident(shape):
    return pl.BlockSpec(shape, lambda *_: (0,) * len(shape), pipeline_mode=pl.Buffered(1))


def _layer_resident(shape):
    return pl.BlockSpec((None,) + tuple(shape[1:]), lambda *a: (a[-1][0],) + (0,) * (len(shape) - 1),
                        pipeline_mode=pl.Buffered(1))


def _layer_grid(grid, in_specs, out_specs):
    return pltpu.PrefetchScalarGridSpec(num_scalar_prefetch=1, grid=grid, in_specs=in_specs, out_specs=out_specs)


def _params(n_axes):
    return pltpu.CompilerParams(dimension_semantics=("arbitrary",) * n_axes,
                                vmem_limit_bytes=VMEM_LIMIT)


def _rms(x, g):
    return x * lax.rsqrt(jnp.mean(x * x, axis=-1, keepdims=True) + EPS) * g


def _dot(a, b):
    return jnp.dot(a, b, preferred_element_type=F32)


def _dot_nt(a, b):
    return lax.dot_general(a, b, (((1,), (1,)), ((), ())), preferred_element_type=F32)


def _rope_table_body(pos_ref, tab_ref):
    pos = pos_ref[...]
    lane = lax.broadcasted_iota(jnp.int32, pos.shape, 1)
    inv = jnp.where(lane < 32,
                    jnp.exp((-math.log(ROPE_THETA) * (2.0 / HEAD_DIM)) * lane.astype(F32)),
                    jnp.exp((-math.log(ROPE_THETA) * (2.0 / MLA_ROPE)) * (lane - 32).astype(F32)))
    ang = pos * inv
    cos, sin = jnp.cos(ang), jnp.sin(ang)

    def spread(v):
        r32, r48 = pltpu.roll(v, 32, axis=1), pltpu.roll(v, 48, axis=1)
        tiled = jnp.where(lane < 32, v, jnp.where(lane < 64, r32, jnp.where(
            lane < 96, pltpu.roll(v, 64, axis=1), pltpu.roll(v, 96, axis=1))))
        return tiled, r32, r48

    cos_a, cos_lo, cos_hi = spread(cos)
    sin_a, sin_lo, sin_hi = spread(sin)
    tab_ref[:, 0:LANES] = cos_a
    tab_ref[:, LANES:2 * LANES] = jnp.where(lane < 64, -sin_a, sin_a)
    tab_ref[:, 2 * LANES:3 * LANES] = jnp.where(lane < 64, 1.0, jnp.where(lane < 80, cos_lo, jnp.where(lane < 96, cos_hi, 0.0)))
    tab_ref[:, 3 * LANES:4 * LANES] = jnp.where((lane >= 64) & (lane < 80), -sin_lo, 0.0)
    tab_ref[:, 4 * LANES:5 * LANES] = jnp.where((lane >= 80) & (lane < 96), sin_hi, 0.0)


def _rope_tables(pos_lanes):
    t = pos_lanes.shape[0]
    tm = min(1024, t)
    return pl.pallas_call(
        _rope_table_body,
        out_shape=jax.ShapeDtypeStruct((t, TAB_W), F32),
        grid=(t // tm,),
        in_specs=[pl.BlockSpec((tm, LANES), lambda i: (i, 0))],
        out_specs=pl.BlockSpec((tm, TAB_W), lambda i: (i, 0)),
        compiler_params=_params(1),
        name="rope_tables",
    )(pos_lanes)


MEM_BATCH = 4


def _mem_kv_body(mem_ref, g_ref, w_ref, o_ref):
    nb, m, d = mem_ref.shape
    h = _rms(mem_ref[...].reshape(nb * m, d), g_ref[...]).astype(BF16)
    o_ref[...] = _dot(h, w_ref[...]).astype(BF16).reshape(o_ref.shape)


def _mem_kv(mem, norm_mem, w_xkv):
    b, m, d = mem.shape
    n_layers = w_xkv.shape[0]
    nb = math.gcd(b, MEM_BATCH)
    return pl.pallas_call(
        _mem_kv_body,
        out_shape=jax.ShapeDtypeStruct((n_layers, b, m, 2 * X_WIDTH), BF16),
        grid=(n_layers, b // nb),
        in_specs=[pl.BlockSpec((nb, m, d), lambda l, i: (i, 0, 0)),
                  pl.BlockSpec((None, 1, d), lambda l, i: (l, 0, 0)),
                  pl.BlockSpec((None, d, 2 * X_WIDTH), lambda l, i: (l, 0, 0))],
        out_specs=pl.BlockSpec((None, nb, m, 2 * X_WIDTH), lambda l, i: (l, i, 0, 0)),
        compiler_params=_params(2),
        name="mem_kv",
    )(mem, norm_mem, w_xkv)


_C_QK = 0
_C_VA = 2 * A_WIDTH
_C_LAT = 3 * A_WIDTH
_C_G = _C_LAT + Q_LORA + KV_LORA + LANES
_KB_W = N_HEADS * LANES


def _mixer_in_body(layer_ref, x_ref, g_ref, win_ref, qg_ref, wuq_ref, kvg_ref, wukv_ref, tab_ref,
                   qa_ref, ka_ref, va_ref, qb_ref, kb_ref, vb_ref, ga_ref, gb_ref):
    d = x_ref.shape[-1]
    h = _rms(x_ref[...], g_ref[...]).astype(BF16)
    cos_a, sin_a = tab_ref[:, 0:LANES], tab_ref[:, LANES:2 * LANES]
    cos_b = tab_ref[:, 2 * LANES:3 * LANES]
    sin_lo, sin_hi = tab_ref[:, 3 * LANES:4 * LANES], tab_ref[:, 4 * LANES:5 * LANES]

    def rope_a(y):
        return y * cos_a + pltpu.roll(y, 64, axis=1) * sin_a

    def rope_b(y):
        return (y * cos_b + pltpu.roll(y, LANES - 16, axis=1) * sin_lo
                + pltpu.roll(y, 16, axis=1) * sin_hi)

    scale_a = HEAD_DIM ** -0.5 * LOG2E
    scale_b = MLA_QK ** -0.5 * LOG2E
    y = _dot(h, win_ref[:, _C_QK:_C_QK + 2 * A_WIDTH])
    for p in range(N_PAIRS):
        sl = slice(p * LANES, (p + 1) * LANES)
        qa_ref[:, sl] = (rope_a(y[:, sl]) * scale_a).astype(BF16)
        ka_ref[:, sl] = rope_a(y[:, A_WIDTH + p * LANES:A_WIDTH + (p + 1) * LANES]).astype(BF16)

    va_ref[...] = _dot(h, win_ref[:, _C_VA:_C_VA + A_WIDTH]).astype(BF16)

    lat = _dot(h, win_ref[:, _C_LAT:_C_G])
    q_lat = _rms(lat[:, 0:Q_LORA], qg_ref[...]).astype(BF16)
    kv_lat = _rms(lat[:, Q_LORA:Q_LORA + KV_LORA], kvg_ref[...]).astype(BF16)
    k_pe = rope_b(lat[:, Q_LORA + KV_LORA:])
    yq = _dot(q_lat, wuq_ref[...])
    ykv = _dot(kv_lat, wukv_ref[...])
    for hd in range(N_HEADS):
        sl = slice(hd * LANES, (hd + 1) * LANES)
        qb_ref[:, sl] = (rope_b(yq[:, sl]) * scale_b).astype(BF16)
        kb_ref[:, sl] = (ykv[:, sl] + k_pe).astype(BF16)
    vb_ref[...] = ykv[:, _KB_W:].astype(BF16)

    ga_ref[...] = jax.nn.sigmoid(_dot(h, win_ref[:, _C_G:_C_G + d])).astype(BF16)
    gb_ref[...] = jax.nn.sigmoid(_dot(h, win_ref[:, _C_G + d:_C_G + 2 * d])).astype(BF16)


def _mixer_in(layer, x, norm_mix, w_in, q_lat_norm, w_uq, kv_lat_norm, w_ukv, tabs):
    t, d = x.shape
    tm = min(TOK_TILE, t)
    tok = lambda w: pl.BlockSpec((tm, w), lambda i, l: (i, 0))
    widths = (A_WIDTH, A_WIDTH, A_WIDTH, _KB_W, _KB_W, A_WIDTH, d, d)
    weights = (norm_mix, w_in, q_lat_norm, w_uq, kv_lat_norm, w_ukv)
    return pl.pallas_call(
        _mixer_in_body,
        out_shape=[jax.ShapeDtypeStruct((t, w), BF16) for w in widths],
        grid_spec=_layer_grid((t // tm,), [tok(d)] + [_layer_resident(w.shape) for w in weights] + [tok(TAB_W)],
                              [tok(w) for w in widths]),
        compiler_params=_params(1),
        name="mixer_in",
    )(layer, x, *weights, tabs)


def _softmax_init(tq):
    return (jnp.full((tq, 1), NEG_BIG, F32), jnp.zeros((tq, LANES), F32))


def _softmax_step(s, v_ones, state):
    m, acc = state
    m_new = jnp.maximum(m, jnp.max(s, axis=-1, keepdims=True))
    p = jnp.exp2(s - m_new)
    acc_new = jnp.exp2(m - m_new) * acc + _dot(p.astype(BF16), v_ones)
    return m_new, acc_new


def _values_and_ones(v):
    lane = lax.broadcasted_iota(jnp.int32, v.shape, 1)
    one = jnp.ones_like(v)
    return jnp.where(lane < HEAD_DIM, v, one), jnp.where(lane < HEAD_DIM, one, v)


def _causal(s, q0, k0):
    row = lax.broadcasted_iota(jnp.int32, s.shape, 0) + q0
    col = lax.broadcasted_iota(jnp.int32, s.shape, 1) + k0
    return jnp.where(col <= row, s, NEG_BIG)


def _merge_pair(states):
    (_, acc0), (_, acc1) = states
    lane = lax.broadcasted_iota(jnp.int32, acc0.shape, 1)
    num = jnp.where(lane < HEAD_DIM, acc0, acc1)
    den = pltpu.roll(jnp.where(lane < HEAD_DIM, acc1, acc0), HEAD_DIM, axis=1)
    return num / den


def _moba_queries(q, kbar, qi, n_blocks):
    tq = q.shape[0]
    lane = lax.broadcasted_iota(jnp.int32, q.shape, 1)
    row = lax.broadcasted_iota(jnp.int32, (MAX_BLOCKS, tq), 0)
    col = lax.broadcasted_iota(jnp.int32, (MAX_BLOCKS, tq), 1)
    cur = qi * (tq // MOBA_BLOCK) + col // MOBA_BLOCK
    q_aug = []
    for hd in range(2):
        qh = jnp.where((lane % 64 < 32) == (hd == 0), q, jnp.zeros_like(q))
        gate = jnp.where(row < cur, _dot_nt(kbar, qh), -jnp.inf)
        rank = jnp.zeros(gate.shape, jnp.int32)
        for n in range(n_blocks):
            gn = gate[n:n + 1, :]
            beats = (gn > gate) | ((gn == gate) & (row > n))
            rank = rank + beats.astype(jnp.int32)
        keep = ((row < cur) & (rank < MOBA_TOPK)) | (row == cur)
        dropped = jnp.where(keep, 0.0, 1.0)
        dropped = jnp.concatenate([dropped, jnp.zeros((LANES - MAX_BLOCKS, tq), F32)], axis=0)
        q_aug.append(jnp.concatenate([qh, dropped.T.astype(BF16)], axis=1))
    return q_aug


def _attn_body(qa_ref, ka_ref, va_ref, e_ref, qb_ref, kb_ref, vb_ref, oa_ref, ob_ref):
    tq = ATT_TILE
    seq = ka_ref.shape[0]
    n_blocks = seq // MOBA_BLOCK
    kbar = [jnp.mean(ka_ref[n * MOBA_BLOCK:(n + 1) * MOBA_BLOCK, :].astype(F32), axis=0, keepdims=True)
            for n in range(n_blocks)]
    kbar = jnp.concatenate(kbar + [jnp.zeros((MAX_BLOCKS - n_blocks, LANES), F32)], axis=0).astype(BF16)

    def tile(qa, qb, j, states, q0, causal):
        rows = slice(j * tq, (j + 1) * tq)
        ka_aug = jnp.concatenate([ka_ref[rows, :], e_ref[rows, :]], axis=1)
        va_ones = _values_and_ones(va_ref[rows, :])
        vb_ones = _values_and_ones(vb_ref[rows, :])
        mask = functools.partial(_causal, q0=q0, k0=j * tq) if causal else (lambda s: s)
        new = []
        for hd in range(2):
            new.append(_softmax_step(mask(_dot_nt(qa[hd], ka_aug)), va_ones[hd], states[hd]))
        for hd in range(2):
            s = _dot_nt(qb[hd], kb_ref[rows, hd * LANES:(hd + 1) * LANES])
            new.append(_softmax_step(mask(s), vb_ones[hd], states[2 + hd]))
        return tuple(new)

    tq2 = 2 * tq
    for t in range(seq // tq2):
        qrows = slice(t * tq2, (t + 1) * tq2)
        qa = _moba_queries(qa_ref[qrows, :], kbar, t, n_blocks)
        qb = [qb_ref[qrows, hd * LANES:(hd + 1) * LANES] for hd in range(2)]
        states = (_softmax_init(tq2),) * 4
        for j in range(2 * t + 1):
            states = tile(qa, qb, j, states, t * tq2, j == 2 * t)
        low = tuple((m[tq:], acc[tq:]) for m, acc in states)
        low = tile([q[tq:] for q in qa], [q[tq:] for q in qb], 2 * t + 1, low, t * tq2 + tq, True)
        states = tuple((jnp.concatenate([m[:tq], ml], axis=0), jnp.concatenate([acc[:tq], accl], axis=0))
                       for (m, acc), (ml, accl) in zip(states, low))
        oa_ref[qrows, :] = _merge_pair(states[:2]).astype(oa_ref.dtype)
        ob_ref[qrows, :] = _merge_pair(states[2:]).astype(ob_ref.dtype)


def _attention(qa, ka, va, e_mat, qb, kb, vb):
    b, s, _ = qa.shape
    seq_spec = lambda w: pl.BlockSpec((None, s, w), lambda i, p: (i, 0, p))
    return pl.pallas_call(
        _attn_body,
        out_shape=[jax.ShapeDtypeStruct((b, s, A_WIDTH), BF16)] * 2,
        grid=(b, N_PAIRS),
        in_specs=[seq_spec(LANES), seq_spec(LANES), seq_spec(LANES), _resident(e_mat.shape),
                  seq_spec(2 * LANES), seq_spec(2 * LANES), seq_spec(LANES)],
        out_specs=[seq_spec(LANES)] * 2,
        compiler_params=_params(2),
        name="attention",
    )(qa, ka, va, e_mat, qb, kb, vb)


def _mixer_out_body(layer_ref, x_ref, oa_ref, ob_ref, ga_ref, gb_ref, kv_ref, wa_ref, wb_ref, wo_ref,
                    gx_ref, wxq_ref, wxo_ref, o_ref):
    merged = (ga_ref[...].astype(F32) * _dot(oa_ref[...], wa_ref[...])
              + gb_ref[...].astype(F32) * _dot(ob_ref[...], wb_ref[...]))
    x1 = x_ref[...] + _dot(merged.astype(BF16), wo_ref[...])

    q = _dot(_rms(x1, gx_ref[...]).astype(BF16), wxq_ref[...]).astype(BF16)
    scale = X_HEAD_DIM ** -0.5
    heads = []
    for hd in range(X_HEADS):
        lanes = slice(hd * X_HEAD_DIM, (hd + 1) * X_HEAD_DIM)
        s = _dot_nt(q[:, lanes], kv_ref[:, lanes])
        m = jnp.max(s, axis=-1, keepdims=True)
        p = jnp.exp((s - m) * scale)
        l = jnp.sum(p, axis=-1, keepdims=True)
        v = kv_ref[:, X_WIDTH + hd * X_HEAD_DIM:X_WIDTH + (hd + 1) * X_HEAD_DIM]
        heads.append((_dot(p.astype(BF16), v) * (1.0 / l)).astype(BF16))
    o_ref[...] = x1 + _dot(jnp.concatenate(heads, axis=1), wxo_ref[...])


def _mixer_out(layer, x, seq, oa, ob, ga, gb, mem_kv, w_a, w_b, w_out, norm_xattn, w_xq, w_xo):
    t, d = x.shape
    tm = min(TOK_TILE, seq)
    tiles_per_seq = seq // tm
    m = mem_kv.shape[2]
    tok = lambda w: pl.BlockSpec((tm, w), lambda i, l: (i, 0))
    weights = (w_a, w_b, w_out, norm_xattn, w_xq, w_xo)
    return pl.pallas_call(
        _mixer_out_body,
        out_shape=jax.ShapeDtypeStruct((t, d), F32),
        grid_spec=_layer_grid(
            (t // tm,),
            [tok(d), tok(A_WIDTH), tok(A_WIDTH), tok(d), tok(d),
             pl.BlockSpec((None, None, m, 2 * X_WIDTH), lambda i, l: (l[0], i // tiles_per_seq, 0, 0))]
            + [_layer_resident(w.shape) for w in weights],
            tok(d)),
        compiler_params=_params(1),
        name="mixer_out",
    )(layer, x, oa, ob, ga, gb, mem_kv, *weights)


FF_CHUNK = 1024


def _ffn_body(layer_ref, x_ref, g_ref, wgu_ref, wd_ref, gf_ref, o_ref, *, final):
    x = x_ref[...]
    h = _rms(x, g_ref[...]).astype(BF16)
    d_ff = wd_ref.shape[0]
    y = x
    for c0 in range(0, d_ff, FF_CHUNK):
        c1 = min(c0 + FF_CHUNK, d_ff)
        gt = _dot(h, wgu_ref[:, c0:c1])
        up = _dot(h, wgu_ref[:, d_ff + c0:d_ff + c1])
        a = (gt * jax.nn.sigmoid(gt) * up).astype(BF16)
        y = y + _dot(a, wd_ref[c0:c1, :])
    o_ref[...] = _rms(y, gf_ref[...]) if final else y


def _ffn(layer, x, norm_ffn, w_gate_up, w_down, norm_final, final):
    t, d = x.shape
    tm = min(TOK_TILE, t)
    tok = pl.BlockSpec((tm, d), lambda i, l: (i, 0))
    weights = (norm_ffn, w_gate_up, w_down)
    return pl.pallas_call(
        functools.partial(_ffn_body, final=final),
        out_shape=jax.ShapeDtypeStruct((t, d), F32),
        grid_spec=_layer_grid((t // tm,), [tok] + [_layer_resident(w.shape) for w in weights]
                              + [_resident(norm_final.shape)], tok),
        compiler_params=_params(1),
        name="ffn_final" if final else "ffn",
    )(layer, x, *weights, norm_final)


def _pair_rope_perm():
    perm = []
    for p in range(N_PAIRS):
        for half in range(2):
            for hd in (2 * p, 2 * p + 1):
                perm.extend(hd * HEAD_DIM + half * 32 + np.arange(32))
    return np.asarray(perm, np.int32)


def _prep_weights(w_in, w_uq, w_ukv):
    n_layers, d, _ = w_in.shape
    perm = _pair_rope_perm()
    c = np.cumsum([A_WIDTH, A_WIDTH, A_WIDTH, Q_LORA, KV_LORA, MLA_ROPE, d])
    w_kpe = jnp.pad(w_in[..., c[4]:c[5]], ((0, 0), (0, 0), (HEAD_DIM, LANES - MLA_QK)))
    w_in_r = jnp.concatenate(
        [w_in[..., 0:c[0]][..., perm], w_in[..., c[0]:c[1]][..., perm], w_in[..., c[1]:c[4]],
         w_kpe, w_in[..., c[5]:]], axis=-1).astype(BF16)
    w_uq_r = jnp.pad(w_uq.reshape(n_layers, Q_LORA, N_HEADS, MLA_QK),
                     ((0, 0), (0, 0), (0, 0), (0, LANES - MLA_QK))).reshape(n_layers, Q_LORA, _KB_W).astype(BF16)
    ukv = w_ukv.reshape(n_layers, KV_LORA, N_HEADS, 2 * HEAD_DIM)
    w_k = jnp.pad(ukv[..., :HEAD_DIM], ((0, 0), (0, 0), (0, 0), (0, LANES - HEAD_DIM))).reshape(n_layers, KV_LORA, _KB_W)
    w_v = ukv[..., HEAD_DIM:].reshape(n_layers, KV_LORA, A_WIDTH)
    w_ukv_r = jnp.concatenate([w_k, w_v], axis=-1).astype(BF16)
    return w_in_r, w_uq_r, w_ukv_r


def kernel(x, mem, positions, norm_mix, w_in, q_lat_norm, w_uq, kv_lat_norm, w_ukv, w_branch_a, w_branch_b, w_out, norm_xattn, norm_mem, w_xq, w_xkv, w_xo, norm_ffn, w_gate_up, w_down, norm_final):
    b, s, d = x.shape
    n_layers = w_in.shape[0]
    t = b * s
    assert s % (2 * ATT_TILE) == 0 and ATT_TILE % MOBA_BLOCK == 0 and s // MOBA_BLOCK <= MAX_BLOCKS

    w_in_r, w_uq_r, w_ukv_r = _prep_weights(w_in, w_uq, w_ukv)
    bf = lambda w: w.astype(BF16)
    row = lambda g: g.reshape(g.shape[0], 1, g.shape[1])
    pos_lanes = jnp.broadcast_to(positions.reshape(t, 1).astype(F32), (t, LANES))
    tabs = _rope_tables(pos_lanes)
    mem_kv = _mem_kv(mem, row(norm_mem), bf(w_xkv))
    blk_of_row = np.arange(s) // MOBA_BLOCK
    e_mat = jnp.asarray(np.where(blk_of_row[:, None] == np.arange(LANES)[None, :], NEG_BIG, 0.0), BF16)
    norm_final = norm_final.reshape(1, d)

    norm_mix, q_lat_norm, kv_lat_norm = row(norm_mix), row(q_lat_norm), row(kv_lat_norm)
    norm_xattn, norm_ffn = row(norm_xattn), row(norm_ffn)
    w_a, w_b, w_o, w_q, w_xo_b = bf(w_branch_a), bf(w_branch_b), bf(w_out), bf(w_xq), bf(w_xo)
    w_gu, w_dn = bf(w_gate_up), bf(w_down)
    seq3 = lambda a: a.reshape(b, s, a.shape[-1])

    def layer(li, xt, final):
        li = jnp.full((1,), li, jnp.int32)
        qa, ka, va, qb, kb, vb, ga, gb = _mixer_in(li, xt, norm_mix, w_in_r, q_lat_norm, w_uq_r,
                                                   kv_lat_norm, w_ukv_r, tabs)
        oa, ob = _attention(seq3(qa), seq3(ka), seq3(va), e_mat, seq3(qb), seq3(kb), seq3(vb))
        oa, ob = oa.reshape(t, A_WIDTH), ob.reshape(t, A_WIDTH)
        xt = _mixer_out(li, xt, s, oa, ob, ga, gb, mem_kv, w_a, w_b, w_o, norm_xattn, w_q, w_xo_b)
        return _ffn(li, xt, norm_ffn, w_gu, w_dn, norm_final, final)

    xt = x.reshape(t, d)
    if n_layers > 1:
        xt = lax.fori_loop(0, n_layers - 1, lambda li, xt: layer(li, xt, False), xt)
    xt = layer(n_layers - 1, xt, True)
    return xt.reshape(b, s, d)
```

```python
import functools
import math

import numpy as np
import jax
import jax.numpy as jnp
from jax import lax
from jax.experimental import pallas as pl
from jax.experimental.pallas import tpu as pltpu

F32 = jnp.float32
BF16 = jnp.bfloat16

EPS = 1e-6
ROPE_THETA = 10000.0
LANES = 128
HEAD_DIM = 64
N_HEADS = 8
N_PAIRS = N_HEADS // 2
A_WIDTH = N_HEADS * HEAD_DIM
MLA_ROPE = 32
MLA_QK = HEAD_DIM + MLA_ROPE
Q_LORA = 384
KV_LORA = 256
MOBA_BLOCK = 256
MOBA_TOPK = 3
MAX_BLOCKS = 16
X_HEADS = 4
X_HEAD_DIM = 128
X_WIDTH = X_HEADS * X_HEAD_DIM
NEG_BIG = -1e30
LOG2E = math.log2(math.e)
VMEM_LIMIT = 56 * 1024 * 1024
TOK_TILE = 512
OUT_TILE = 1024
ATT_TILE = 512
TAB_W = 6 * LANES


def _resident(shape):
    return pl.BlockSpec(shape, lambda *_: (0,) * len(shape), pipeline_mode=pl.Buffered(1))


def _layer_resident(shape):
    return pl.BlockSpec((None,) + tuple(shape[1:]), lambda *a: (a[-1][0],) + (0,) * (len(shape) - 1),
                        pipeline_mode=pl.Buffered(1))


def _layer_grid(grid, in_specs, out_specs):
    return pltpu.PrefetchScalarGridSpec(num_scalar_prefetch=1, grid=grid, in_specs=in_specs, out_specs=out_specs)


def _params(n_axes):
    return pltpu.CompilerParams(dimension_semantics=("arbitrary",) * n_axes,
                                vmem_limit_bytes=VMEM_LIMIT)


def _rms(x, g):
    return x * lax.rsqrt(jnp.mean(x * x, axis=-1, keepdims=True) + EPS) * g


def _dot(a, b):
    return jnp.dot(a, b, preferred_element_type=F32)


def _dot_nt(a, b):
    return lax.dot_general(a, b, (((1,), (1,)), ((), ())), preferred_element_type=F32)


def _rope_table_body(pos_ref, tab_ref):
    pos = pos_ref[...]
    lane = lax.broadcasted_iota(jnp.int32, pos.shape, 1)
    inv = jnp.where(lane < 32,
                    jnp.exp((-math.log(ROPE_THETA) * (2.0 / HEAD_DIM)) * lane.astype(F32)),
                    jnp.exp((-math.log(ROPE_THETA) * (2.0 / MLA_ROPE)) * (lane - 32).astype(F32)))
    ang = pos * inv
    cos, sin = jnp.cos(ang), jnp.sin(ang)

    def spread(v):
        r32, r48 = pltpu.roll(v, 32, axis=1), pltpu.roll(v, 48, axis=1)
        tiled = jnp.where(lane < 32, v, jnp.where(lane < 64, r32, jnp.where(
            lane < 96, pltpu.roll(v, 64, axis=1), pltpu.roll(v, 96, axis=1))))
        return tiled, r32, r48

    cos_a, cos_lo, cos_hi = spread(cos)
    sin_a, sin_lo, sin_hi = spread(sin)
    first_half = lane % 64 < 32
    tab_ref[:, 0:LANES] = cos_a
    tab_ref[:, LANES:2 * LANES] = jnp.where(first_half, -sin_a, 0.0)
    tab_ref[:, 2 * LANES:3 * LANES] = jnp.where(first_half, 0.0, sin_a)
    tab_ref[:, 3 * LANES:4 * LANES] = jnp.where(lane < 64, 1.0, jnp.where(lane < 80, cos_lo, jnp.where(lane < 96, cos_hi, 0.0)))
    tab_ref[:, 4 * LANES:5 * LANES] = jnp.where((lane >= 64) & (lane < 80), -sin_lo, 0.0)
    tab_ref[:, 5 * LANES:6 * LANES] = jnp.where((lane >= 80) & (lane < 96), sin_hi, 0.0)


def _rope_tables(pos_lanes):
    t = pos_lanes.shape[0]
    tm = min(1024, t)
    return pl.pallas_call(
        _rope_table_body,
        out_shape=jax.ShapeDtypeStruct((t, TAB_W), F32),
        grid=(t // tm,),
        in_specs=[pl.BlockSpec((tm, LANES), lambda i: (i, 0))],
        out_specs=pl.BlockSpec((tm, TAB_W), lambda i: (i, 0)),
        compiler_params=_params(1),
        name="rope_tables",
    )(pos_lanes)


MEM_BATCH = 4


def _mem_kv_body(mem_ref, g_ref, w_ref, o_ref):
    nb, m, d = mem_ref.shape
    h = _rms(mem_ref[...].reshape(nb * m, d), g_ref[...]).astype(BF16)
    o_ref[...] = _dot(h, w_ref[...]).astype(BF16).reshape(o_ref.shape)


def _mem_kv(mem, norm_mem, w_xkv):
    b, m, d = mem.shape
    n_layers = w_xkv.shape[0]
    nb = math.gcd(b, MEM_BATCH)
    return pl.pallas_call(
        _mem_kv_body,
        out_shape=jax.ShapeDtypeStruct((n_layers, b, m, 2 * X_WIDTH), BF16),
        grid=(n_layers, b // nb),
        in_specs=[pl.BlockSpec((nb, m, d), lambda l, i: (i, 0, 0)),
                  pl.BlockSpec((None, 1, d), lambda l, i: (l, 0, 0)),
                  pl.BlockSpec((None, d, 2 * X_WIDTH), lambda l, i: (l, 0, 0))],
        out_specs=pl.BlockSpec((None, nb, m, 2 * X_WIDTH), lambda l, i: (l, i, 0, 0)),
        compiler_params=_params(2),
        name="mem_kv",
    )(mem, norm_mem, w_xkv)


_C_QK = 0
_C_VA = 2 * A_WIDTH
_C_LAT = 3 * A_WIDTH
_C_G = _C_LAT + Q_LORA + KV_LORA + LANES
_KB_W = N_HEADS * LANES


def _mixer_in_body(layer_ref, x_ref, g_ref, win_ref, qg_ref, wuq_ref, kvg_ref, wukv_ref, tab_ref,
                   qa_ref, ka_ref, va_ref, qb_ref, kb_ref, vb_ref, ga_ref, gb_ref):
    d = x_ref.shape[-1]
    h = _rms(x_ref[...], g_ref[...]).astype(BF16)
    cos_a, sa_lo, sa_hi = tab_ref[:, 0:LANES], tab_ref[:, LANES:2 * LANES], tab_ref[:, 2 * LANES:3 * LANES]
    cos_b = tab_ref[:, 3 * LANES:4 * LANES]
    sin_lo, sin_hi = tab_ref[:, 4 * LANES:5 * LANES], tab_ref[:, 5 * LANES:6 * LANES]

    def rope_a(y):
        return (y * cos_a + pltpu.roll(y, LANES - 32, axis=1) * sa_lo
                + pltpu.roll(y, 32, axis=1) * sa_hi)

    def rope_b(y):
        return (y * cos_b + pltpu.roll(y, LANES - 16, axis=1) * sin_lo
                + pltpu.roll(y, 16, axis=1) * sin_hi)

    scale_a = HEAD_DIM ** -0.5 * LOG2E
    scale_b = MLA_QK ** -0.5 * LOG2E
    y = _dot(h, win_ref[:, _C_QK:_C_QK + 2 * A_WIDTH])
    for p in range(N_PAIRS):
        sl = slice(p * LANES, (p + 1) * LANES)
        qa_ref[:, sl] = (rope_a(y[:, sl]) * scale_a).astype(BF16)
        ka_ref[:, sl] = rope_a(y[:, A_WIDTH + p * LANES:A_WIDTH + (p + 1) * LANES]).astype(BF16)

    va_ref[...] = _dot(h, win_ref[:, _C_VA:_C_VA + A_WIDTH]).astype(BF16)

    lat = _dot(h, win_ref[:, _C_LAT:_C_G])
    q_lat = _rms(lat[:, 0:Q_LORA], qg_ref[...]).astype(BF16)
    kv_lat = _rms(lat[:, Q_LORA:Q_LORA + KV_LORA], kvg_ref[...]).astype(BF16)
    k_pe = rope_b(lat[:, Q_LORA + KV_LORA:])
    yq = _dot(q_lat, wuq_ref[...])
    ykv = _dot(kv_lat, wukv_ref[...])
    for hd in range(N_HEADS):
        sl = slice(hd * LANES, (hd + 1) * LANES)
        qb_ref[:, sl] = (rope_b(yq[:, sl]) * scale_b).astype(BF16)
        kb_ref[:, sl] = (ykv[:, sl] + k_pe).astype(BF16)
    vb_ref[...] = ykv[:, _KB_W:].astype(BF16)

    ga_ref[...] = jax.nn.sigmoid(_dot(h, win_ref[:, _C_G:_C_G + d])).astype(BF16)
    gb_ref[...] = jax.nn.sigmoid(_dot(h, win_ref[:, _C_G + d:_C_G + 2 * d])).astype(BF16)


def _mixer_in(layer, x, norm_mix, w_in, q_lat_norm, w_uq, kv_lat_norm, w_ukv, tabs):
    t, d = x.shape
    tm = min(TOK_TILE, t)
    tok = lambda w: pl.BlockSpec((tm, w), lambda i, l: (i, 0))
    widths = (A_WIDTH, A_WIDTH, A_WIDTH, _KB_W, _KB_W, A_WIDTH, d, d)
    weights = (norm_mix, w_in, q_lat_norm, w_uq, kv_lat_norm, w_ukv)
    return pl.pallas_call(
        _mixer_in_body,
        out_shape=[jax.ShapeDtypeStruct((t, w), BF16) for w in widths],
        grid_spec=_layer_grid((t // tm,), [tok(d)] + [_layer_resident(w.shape) for w in weights] + [tok(TAB_W)],
                              [tok(w) for w in widths]),
        compiler_params=_params(1),
        name="mixer_in",
    )(layer, x, *weights, tabs)


def _softmax_init(tq):
    return (jnp.full((tq, 1), NEG_BIG, F32), jnp.zeros((tq, LANES), F32))


def _softmax_step(s, v_ones, state):
    m, acc = state
    m_new = jnp.maximum(m, jnp.max(s, axis=-1, keepdims=True))
    p = jnp.exp2(s - m_new)
    acc_new = jnp.exp2(m - m_new) * acc + _dot(p.astype(BF16), v_ones)
    return m_new, acc_new


def _values_and_ones(v):
    lane = lax.broadcasted_iota(jnp.int32, v.shape, 1)
    one = jnp.ones_like(v)
    return jnp.where(lane < HEAD_DIM, v, one), jnp.where(lane < HEAD_DIM, one, v)


def _causal(s, q0, k0):
    row = lax.broadcasted_iota(jnp.int32, s.shape, 0) + q0
    col = lax.broadcasted_iota(jnp.int32, s.shape, 1) + k0
    return jnp.where(col <= row, s, NEG_BIG)


def _merge_pair(states):
    (_, acc0), (_, acc1) = states
    lane = lax.broadcasted_iota(jnp.int32, acc0.shape, 1)
    num = jnp.where(lane < HEAD_DIM, acc0, acc1)
    den = pltpu.roll(jnp.where(lane < HEAD_DIM, acc1, acc0), HEAD_DIM, axis=1)
    return num / den


def _moba_queries(q, kbar, qi, n_blocks):
    tq = q.shape[0]
    lane = lax.broadcasted_iota(jnp.int32, q.shape, 1)
    row = lax.broadcasted_iota(jnp.int32, (MAX_BLOCKS, tq), 0)
    col = lax.broadcasted_iota(jnp.int32, (MAX_BLOCKS, tq), 1)
    cur = qi * (tq // MOBA_BLOCK) + col // MOBA_BLOCK
    q_aug = []
    for hd in range(2):
        qh = jnp.where((lane < HEAD_DIM) == (hd == 0), q, jnp.zeros_like(q))
        gate = jnp.where(row < cur, _dot_nt(kbar, qh), -jnp.inf)
        rank = jnp.zeros(gate.shape, jnp.int32)
        for n in range(n_blocks):
            gn = gate[n:n + 1, :]
            beats = (gn > gate) | ((gn == gate) & (row > n))
            rank = rank + beats.astype(jnp.int32)
        keep = ((row < cur) & (rank < MOBA_TOPK)) | (row == cur)
        dropped = jnp.where(keep, 0.0, 1.0)
        dropped = jnp.concatenate([dropped, jnp.zeros((LANES - MAX_BLOCKS, tq), F32)], axis=0)
        q_aug.append(jnp.concatenate([qh, dropped.T.astype(BF16)], axis=1))
    return q_aug


def _attn_body(qa_ref, ka_ref, va_ref, e_ref, qb_ref, kb_ref, vb_ref, oa_ref, ob_ref):
    tq = ATT_TILE
    seq = ka_ref.shape[0]
    n_blocks = seq // MOBA_BLOCK
    kbar = [jnp.mean(ka_ref[n * MOBA_BLOCK:(n + 1) * MOBA_BLOCK, :].astype(F32), axis=0, keepdims=True)
            for n in range(n_blocks)]
    kbar = jnp.concatenate(kbar + [jnp.zeros((MAX_BLOCKS - n_blocks, LANES), F32)], axis=0).astype(BF16)

    def tile(qa, qb, j, states, q0, causal):
        rows = slice(j * tq, (j + 1) * tq)
        ka_aug = jnp.concatenate([ka_ref[rows, :], e_ref[rows, :]], axis=1)
        va_ones = _values_and_ones(va_ref[rows, :])
        vb_ones = _values_and_ones(vb_ref[rows, :])
        mask = functools.partial(_causal, q0=q0, k0=j * tq) if causal else (lambda s: s)
        new = []
        for hd in range(2):
            new.append(_softmax_step(mask(_dot_nt(qa[hd], ka_aug)), va_ones[hd], states[hd]))
        for hd in range(2):
            s = _dot_nt(qb[hd], kb_ref[rows, hd * LANES:(hd + 1) * LANES])
            new.append(_softmax_step(mask(s), vb_ones[hd], states[2 + hd]))
        return tuple(new)

    tq2 = 2 * tq
    for t in range(seq // tq2):
        qrows = slice(t * tq2, (t + 1) * tq2)
        qa = _moba_queries(qa_ref[qrows, :], kbar, t, n_blocks)
        qb = [qb_ref[qrows, hd * LANES:(hd + 1) * LANES] for hd in range(2)]
        states = (_softmax_init(tq2),) * 4
        for j in range(2 * t + 1):
            states = tile(qa, qb, j, states, t * tq2, j == 2 * t)
        low = tuple((m[tq:], acc[tq:]) for m, acc in states)
        low = tile([q[tq:] for q in qa], [q[tq:] for q in qb], 2 * t + 1, low, t * tq2 + tq, True)
        states = tuple((jnp.concatenate([m[:tq], ml], axis=0), jnp.concatenate([acc[:tq], accl], axis=0))
                       for (m, acc), (ml, accl) in zip(states, low))
        oa_ref[qrows, :] = _merge_pair(states[:2]).astype(oa_ref.dtype)
        ob_ref[qrows, :] = _merge_pair(states[2:]).astype(ob_ref.dtype)


def _attention(qa, ka, va, e_mat, qb, kb, vb):
    b, s, _ = qa.shape
    seq_spec = lambda w: pl.BlockSpec((None, s, w), lambda i, p: (i, 0, p))
    return pl.pallas_call(
        _attn_body,
        out_shape=[jax.ShapeDtypeStruct((b, s, A_WIDTH), BF16)] * 2,
        grid=(b, N_PAIRS),
        in_specs=[seq_spec(LANES), seq_spec(LANES), seq_spec(LANES), _resident(e_mat.shape),
                  seq_spec(2 * LANES), seq_spec(2 * LANES), seq_spec(LANES)],
        out_specs=[seq_spec(LANES)] * 2,
        compiler_params=_params(2),
        name="attention",
    )(qa, ka, va, e_mat, qb, kb, vb)


def _mixer_out_body(layer_ref, x_ref, oa_ref, ob_ref, ga_ref, gb_ref, kv_ref, wa_ref, wb_ref, wo_ref,
                    gx_ref, wxq_ref, wxo_ref, o_ref):
    merged = (ga_ref[...].astype(F32) * _dot(oa_ref[...], wa_ref[...])
              + gb_ref[...].astype(F32) * _dot(ob_ref[...], wb_ref[...]))
    x1 = x_ref[...] + _dot(merged.astype(BF16), wo_ref[...])

    q = _dot(_rms(x1, gx_ref[...]).astype(BF16), wxq_ref[...]).astype(BF16)
    scale = X_HEAD_DIM ** -0.5
    heads = []
    for hd in range(X_HEADS):
        lanes = slice(hd * X_HEAD_DIM, (hd + 1) * X_HEAD_DIM)
        s = _dot_nt(q[:, lanes], kv_ref[:, lanes])
        m = jnp.max(s, axis=-1, keepdims=True)
        p = jnp.exp((s - m) * scale)
        l = jnp.sum(p, axis=-1, keepdims=True)
        v = kv_ref[:, X_WIDTH + hd * X_HEAD_DIM:X_WIDTH + (hd + 1) * X_HEAD_DIM]
        heads.append((_dot(p.astype(BF16), v) * (1.0 / l)).astype(BF16))
    o_ref[...] = x1 + _dot(jnp.concatenate(heads, axis=1), wxo_ref[...])


def _mixer_out(layer, x, seq, oa, ob, ga, gb, mem_kv, w_a, w_b, w_out, norm_xattn, w_xq, w_xo):
    t, d = x.shape
    tm = min(OUT_TILE, seq)
    tiles_per_seq = seq // tm
    m = mem_kv.shape[2]
    tok = lambda w: pl.BlockSpec((tm, w), lambda i, l: (i, 0))
    weights = (w_a, w_b, w_out, norm_xattn, w_xq, w_xo)
    return pl.pallas_call(
        _mixer_out_body,
        out_shape=jax.ShapeDtypeStruct((t, d), F32),
        grid_spec=_layer_grid(
            (t // tm,),
            [tok(d), tok(A_WIDTH), tok(A_WIDTH), tok(d), tok(d),
             pl.BlockSpec((None, None, m, 2 * X_WIDTH), lambda i, l: (l[0], i // tiles_per_seq, 0, 0))]
            + [_layer_resident(w.shape) for w in weights],
            tok(d)),
        compiler_params=_params(1),
        name="mixer_out",
    )(layer, x, oa, ob, ga, gb, mem_kv, *weights)


FF_CHUNK = 1024


def _ffn_body(layer_ref, x_ref, g_ref, wgu_ref, wd_ref, gf_ref, o_ref, *, final):
    x = x_ref[...]
    h = _rms(x, g_ref[...]).astype(BF16)
    d_ff = wd_ref.shape[0]
    y = x
    for c0 in range(0, d_ff, FF_CHUNK):
        c1 = min(c0 + FF_CHUNK, d_ff)
        gt = _dot(h, wgu_ref[:, c0:c1])
        up = _dot(h, wgu_ref[:, d_ff + c0:d_ff + c1])
        a = (gt * jax.nn.sigmoid(gt) * up).astype(BF16)
        y = y + _dot(a, wd_ref[c0:c1, :])
    o_ref[...] = _rms(y, gf_ref[...]) if final else y


def _ffn(layer, x, norm_ffn, w_gate_up, w_down, norm_final, final):
    t, d = x.shape
    tm = min(TOK_TILE, t)
    tok = pl.BlockSpec((tm, d), lambda i, l: (i, 0))
    weights = (norm_ffn, w_gate_up, w_down)
    return pl.pallas_call(
        functools.partial(_ffn_body, final=final),
        out_shape=jax.ShapeDtypeStruct((t, d), F32),
        grid_spec=_layer_grid((t // tm,), [tok] + [_layer_resident(w.shape) for w in weights]
                              + [_resident(norm_final.shape)], tok),
        compiler_params=_params(1),
        name="ffn_final" if final else "ffn",
    )(layer, x, *weights, norm_final)


def _prep_weights(w_in, w_uq, w_ukv):
    n_layers, d, _ = w_in.shape
    c = np.cumsum([A_WIDTH, A_WIDTH, A_WIDTH, Q_LORA, KV_LORA, MLA_ROPE, d])
    w_kpe = jnp.pad(w_in[..., c[4]:c[5]], ((0, 0), (0, 0), (HEAD_DIM, LANES - MLA_QK)))
    w_in_r = jnp.concatenate([w_in[..., :c[4]], w_kpe, w_in[..., c[5]:]], axis=-1).astype(BF16)
    w_uq_r = jnp.pad(w_uq.reshape(n_layers, Q_LORA, N_HEADS, MLA_QK),
                     ((0, 0), (0, 0), (0, 0), (0, LANES - MLA_QK))).reshape(n_layers, Q_LORA, _KB_W).astype(BF16)
    ukv = w_ukv.reshape(n_layers, KV_LORA, N_HEADS, 2 * HEAD_DIM)
    w_k = jnp.pad(ukv[..., :HEAD_DIM], ((0, 0), (0, 0), (0, 0), (0, LANES - HEAD_DIM))).reshape(n_layers, KV_LORA, _KB_W)
    w_v = ukv[..., HEAD_DIM:].reshape(n_layers, KV_LORA, A_WIDTH)
    w_ukv_r = jnp.concatenate([w_k, w_v], axis=-1).astype(BF16)
    return w_in_r, w_uq_r, w_ukv_r


def kernel(x, mem, positions, norm_mix, w_in, q_lat_norm, w_uq, kv_lat_norm, w_ukv, w_branch_a, w_branch_b, w_out, norm_xattn, norm_mem, w_xq, w_xkv, w_xo, norm_ffn, w_gate_up, w_down, norm_final):
    b, s, d = x.shape
    n_layers = w_in.shape[0]
    t = b * s
    assert s % (2 * ATT_TILE) == 0 and ATT_TILE % MOBA_BLOCK == 0 and s // MOBA_BLOCK <= MAX_BLOCKS

    w_in_r, w_uq_r, w_ukv_r = _prep_weights(w_in, w_uq, w_ukv)
    bf = lambda w: w.astype(BF16)
    row = lambda g: g.reshape(g.shape[0], 1, g.shape[1])
    pos_lanes = jnp.broadcast_to(positions.reshape(t, 1).astype(F32), (t, LANES))
    tabs = _rope_tables(pos_lanes)
    mem_kv = _mem_kv(mem, row(norm_mem), bf(w_xkv))
    blk_of_row = np.arange(s) // MOBA_BLOCK
    e_mat = jnp.asarray(np.where(blk_of_row[:, None] == np.arange(LANES)[None, :], NEG_BIG, 0.0), BF16)
    norm_final = norm_final.reshape(1, d)

    norm_mix, q_lat_norm, kv_lat_norm = row(norm_mix), row(q_lat_norm), row(kv_lat_norm)
    norm_xattn, norm_ffn = row(norm_xattn), row(norm_ffn)
    w_a, w_b, w_o, w_q, w_xo_b = bf(w_branch_a), bf(w_branch_b), bf(w_out), bf(w_xq), bf(w_xo)
    w_gu, w_dn = bf(w_gate_up), bf(w_down)
    seq3 = lambda a: a.reshape(b, s, a.shape[-1])

    def layer(li, xt, final):
        li = jnp.full((1,), li, jnp.int32)
        qa, ka, va, qb, kb, vb, ga, gb = _mixer_in(li, xt, norm_mix, w_in_r, q_lat_norm, w_uq_r,
                                                   kv_lat_norm, w_ukv_r, tabs)
        oa, ob = _attention(seq3(qa), seq3(ka), seq3(va), e_mat, seq3(qb), seq3(kb), seq3(vb))
        oa, ob = oa.reshape(t, A_WIDTH), ob.reshape(t, A_WIDTH)
        xt = _mixer_out(li, xt, s, oa, ob, ga, gb, mem_kv, w_a, w_b, w_o, norm_xattn, w_q, w_xo_b)
        return _ffn(li, xt, norm_ffn, w_gu, w_dn, norm_final, final)

    xt = x.reshape(t, d)
    if n_layers > 1:
        xt = lax.fori_loop(0, n_layers - 1, lambda li, xt: layer(li, xt, False), xt)
    xt = layer(n_layers - 1, xt, True)
    return xt.reshape(b, s, d)
```

```python
import functools
import math

import numpy as np
import jax
import jax.numpy as jnp
from jax import lax
from jax.experimental import pallas as pl
from jax.experimental.pallas import tpu as pltpu

F32 = jnp.float32
BF16 = jnp.bfloat16

EPS = 1e-6
ROPE_THETA = 10000.0
LANES = 128
HEAD_DIM = 64
N_HEADS = 8
N_PAIRS = N_HEADS // 2
A_WIDTH = N_HEADS * HEAD_DIM
MLA_ROPE = 32
MLA_QK = HEAD_DIM + MLA_ROPE
Q_LORA = 384
KV_LORA = 256
MOBA_BLOCK = 256
MOBA_TOPK = 3
MAX_BLOCKS = 16
X_HEADS = 4
X_HEAD_DIM = 128
X_WIDTH = X_HEADS * X_HEAD_DIM
NEG_BIG = -1e30
LOG2E = math.log2(math.e)
VMEM_LIMIT = 56 * 1024 * 1024
TOK_TILE = 512
OUT_TILE = 1024
ATT_TILE = 512
TAB_W = 5 * LANES


def _resident(shape):
    return pl.BlockSpec(shape, lambda *_: (0,) * len(shape), pipeline_mode=pl.Buffered(1))


def _layer_resident(shape):
    return pl.BlockSpec((None,) + tuple(shape[1:]), lambda *a: (a[-1][0],) + (0,) * (len(shape) - 1),
                        pipeline_mode=pl.Buffered(1))


def _layer_grid(grid, in_specs, out_specs):
    return pltpu.PrefetchScalarGridSpec(num_scalar_prefetch=1, grid=grid, in_specs=in_specs, out_specs=out_specs)


def _params(n_axes):
    return pltpu.CompilerParams(dimension_semantics=("arbitrary",) * n_axes,
                                vmem_limit_bytes=VMEM_LIMIT)


def _rms(x, g):
    return x * lax.rsqrt(jnp.mean(x * x, axis=-1, keepdims=True) + EPS) * g


def _dot(a, b):
    return jnp.dot(a, b, preferred_element_type=F32)


def _dot_nt(a, b):
    return lax.dot_general(a, b, (((1,), (1,)), ((), ())), preferred_element_type=F32)


def _rope_table_body(pos_ref, tab_ref):
    pos = pos_ref[...]
    lane = lax.broadcasted_iota(jnp.int32, pos.shape, 1)
    inv = jnp.where(lane < 32,
                    jnp.exp((-math.log(ROPE_THETA) * (2.0 / HEAD_DIM)) * lane.astype(F32)),
                    jnp.exp((-math.log(ROPE_THETA) * (2.0 / MLA_ROPE)) * (lane - 32).astype(F32)))
    ang = pos * inv
    cos, sin = jnp.cos(ang), jnp.sin(ang)

    def spread(v):
        r32, r96 = pltpu.roll(v, 32, axis=1), pltpu.roll(v, 96, axis=1)
        tiled = jnp.where(lane < 32, v, jnp.where(lane < 64, r32, jnp.where(
            lane < 96, pltpu.roll(v, 64, axis=1), r96)))
        return tiled, jnp.where(lane < 64, r96, r32)

    cos_a, cos_b = spread(cos)
    sin_a, sin_b = spread(sin)
    first_half = lane % 64 < 32
    is_rope = lane % 64 < MLA_ROPE // 2
    tab_ref[:, 0:LANES] = cos_a
    tab_ref[:, LANES:2 * LANES] = jnp.where(first_half, -sin_a, 0.0)
    tab_ref[:, 2 * LANES:3 * LANES] = jnp.where(first_half, 0.0, sin_a)
    tab_ref[:, 3 * LANES:4 * LANES] = jnp.where(is_rope, cos_b, 1.0)
    tab_ref[:, 4 * LANES:5 * LANES] = jnp.where(is_rope, jnp.where(lane < 64, -sin_b, sin_b), 0.0)


def _rope_tables(pos_lanes):
    t = pos_lanes.shape[0]
    tm = min(1024, t)
    return pl.pallas_call(
        _rope_table_body,
        out_shape=jax.ShapeDtypeStruct((t, TAB_W), F32),
        grid=(t // tm,),
        in_specs=[pl.BlockSpec((tm, LANES), lambda i: (i, 0))],
        out_specs=pl.BlockSpec((tm, TAB_W), lambda i: (i, 0)),
        compiler_params=_params(1),
        name="rope_tables",
    )(pos_lanes)


MEM_BATCH = 4


def _mem_kv_body(mem_ref, g_ref, w_ref, o_ref):
    nb, m, d = mem_ref.shape
    h = _rms(mem_ref[...].reshape(nb * m, d), g_ref[...]).astype(BF16)
    o_ref[...] = _dot(h, w_ref[...]).astype(BF16).reshape(o_ref.shape)


def _mem_kv(mem, norm_mem, w_xkv):
    b, m, d = mem.shape
    n_layers = w_xkv.shape[0]
    nb = math.gcd(b, MEM_BATCH)
    return pl.pallas_call(
        _mem_kv_body,
        out_shape=jax.ShapeDtypeStruct((n_layers, b, m, 2 * X_WIDTH), BF16),
        grid=(n_layers, b // nb),
        in_specs=[pl.BlockSpec((nb, m, d), lambda l, i: (i, 0, 0)),
                  pl.BlockSpec((None, 1, d), lambda l, i: (l, 0, 0)),
                  pl.BlockSpec((None, d, 2 * X_WIDTH), lambda l, i: (l, 0, 0))],
        out_specs=pl.BlockSpec((None, nb, m, 2 * X_WIDTH), lambda l, i: (l, i, 0, 0)),
        compiler_params=_params(2),
        name="mem_kv",
    )(mem, norm_mem, w_xkv)


_C_QK = 0
_C_VA = 2 * A_WIDTH
_C_LAT = 3 * A_WIDTH
_C_G = _C_LAT + Q_LORA + KV_LORA + LANES
_KB_W = N_HEADS * LANES


def _mixer_in_body(layer_ref, x_ref, g_ref, win_ref, qg_ref, wuq_ref, kvg_ref, wukv_ref, tab_ref,
                   qa_ref, ka_ref, va_ref, qb_ref, kb_ref, vb_ref, ga_ref, gb_ref):
    d = x_ref.shape[-1]
    h = _rms(x_ref[...], g_ref[...]).astype(BF16)
    cos_a, sa_lo, sa_hi = tab_ref[:, 0:LANES], tab_ref[:, LANES:2 * LANES], tab_ref[:, 2 * LANES:3 * LANES]
    cos_b, sin_b = tab_ref[:, 3 * LANES:4 * LANES], tab_ref[:, 4 * LANES:5 * LANES]

    def rope_a(y):
        return (y * cos_a + pltpu.roll(y, LANES - 32, axis=1) * sa_lo
                + pltpu.roll(y, 32, axis=1) * sa_hi)

    def rope_b(y):
        return y * cos_b + pltpu.roll(y, 64, axis=1) * sin_b

    scale_a = HEAD_DIM ** -0.5 * LOG2E
    scale_b = MLA_QK ** -0.5 * LOG2E
    y = _dot(h, win_ref[:, _C_QK:_C_QK + 2 * A_WIDTH])
    for p in range(N_PAIRS):
        sl = slice(p * LANES, (p + 1) * LANES)
        qa_ref[:, sl] = (rope_a(y[:, sl]) * scale_a).astype(BF16)
        ka_ref[:, sl] = rope_a(y[:, A_WIDTH + p * LANES:A_WIDTH + (p + 1) * LANES]).astype(BF16)

    va_ref[...] = _dot(h, win_ref[:, _C_VA:_C_VA + A_WIDTH]).astype(BF16)

    lat = _dot(h, win_ref[:, _C_LAT:_C_G])
    q_lat = _rms(lat[:, 0:Q_LORA], qg_ref[...]).astype(BF16)
    kv_lat = _rms(lat[:, Q_LORA:Q_LORA + KV_LORA], kvg_ref[...]).astype(BF16)
    k_pe = rope_b(lat[:, Q_LORA + KV_LORA:])
    yq = _dot(q_lat, wuq_ref[...])
    ykv = _dot(kv_lat, wukv_ref[...])
    for hd in range(N_HEADS):
        sl = slice(hd * LANES, (hd + 1) * LANES)
        qb_ref[:, sl] = (rope_b(yq[:, sl]) * scale_b).astype(BF16)
        kb_ref[:, sl] = (ykv[:, sl] + k_pe).astype(BF16)
    vb_ref[...] = ykv[:, _KB_W:].astype(BF16)

    ga_ref[...] = jax.nn.sigmoid(_dot(h, win_ref[:, _C_G:_C_G + d])).astype(BF16)
    gb_ref[...] = jax.nn.sigmoid(_dot(h, win_ref[:, _C_G + d:_C_G + 2 * d])).astype(BF16)


def _mixer_in(layer, x, norm_mix, w_in, q_lat_norm, w_uq, kv_lat_norm, w_ukv, tabs):
    t, d = x.shape
    tm = min(TOK_TILE, t)
    tok = lambda w: pl.BlockSpec((tm, w), lambda i, l: (i, 0))
    widths = (A_WIDTH, A_WIDTH, A_WIDTH, _KB_W, _KB_W, A_WIDTH, d, d)
    weights = (norm_mix, w_in, q_lat_norm, w_uq, kv_lat_norm, w_ukv)
    return pl.pallas_call(
        _mixer_in_body,
        out_shape=[jax.ShapeDtypeStruct((t, w), BF16) for w in widths],
        grid_spec=_layer_grid((t // tm,), [tok(d)] + [_layer_resident(w.shape) for w in weights] + [tok(TAB_W)],
                              [tok(w) for w in widths]),
        compiler_params=_params(1),
        name="mixer_in",
    )(layer, x, *weights, tabs)


def _softmax_init(tq):
    return (jnp.full((tq, 1), NEG_BIG, F32), jnp.zeros((tq, LANES), F32))


def _softmax_step(s, v_ones, state):
    m, acc = state
    m_new = jnp.maximum(m, jnp.max(s, axis=-1, keepdims=True))
    p = jnp.exp2(s - m_new)
    acc_new = jnp.exp2(m - m_new) * acc + _dot(p.astype(BF16), v_ones)
    return m_new, acc_new


def _values_and_ones(v):
    lane = lax.broadcasted_iota(jnp.int32, v.shape, 1)
    one = jnp.ones_like(v)
    return jnp.where(lane < HEAD_DIM, v, one), jnp.where(lane < HEAD_DIM, one, v)


def _causal(s, q0, k0):
    row = lax.broadcasted_iota(jnp.int32, s.shape, 0) + q0
    col = lax.broadcasted_iota(jnp.int32, s.shape, 1) + k0
    return jnp.where(col <= row, s, NEG_BIG)


def _merge_pair(states):
    (_, acc0), (_, acc1) = states
    lane = lax.broadcasted_iota(jnp.int32, acc0.shape, 1)
    num = jnp.where(lane < HEAD_DIM, acc0, acc1)
    den = pltpu.roll(jnp.where(lane < HEAD_DIM, acc1, acc0), HEAD_DIM, axis=1)
    return num / den


def _moba_queries(q, kbar, qi, n_blocks):
    tq = q.shape[0]
    lane = lax.broadcasted_iota(jnp.int32, q.shape, 1)
    row = lax.broadcasted_iota(jnp.int32, (MAX_BLOCKS, tq), 0)
    col = lax.broadcasted_iota(jnp.int32, (MAX_BLOCKS, tq), 1)
    cur = qi * (tq // MOBA_BLOCK) + col // MOBA_BLOCK
    q_aug = []
    for hd in range(2):
        qh = jnp.where((lane < HEAD_DIM) == (hd == 0), q, jnp.zeros_like(q))
        gate = jnp.where(row < cur, _dot_nt(kbar, qh), -jnp.inf)
        rank = jnp.zeros(gate.shape, jnp.int32)
        for n in range(n_blocks):
            gn = gate[n:n + 1, :]
            beats = (gn > gate) | ((gn == gate) & (row > n))
            rank = rank + beats.astype(jnp.int32)
        keep = ((row < cur) & (rank < MOBA_TOPK)) | (row == cur)
        dropped = jnp.where(keep, 0.0, 1.0)
        dropped = jnp.concatenate([dropped, jnp.zeros((LANES - MAX_BLOCKS, tq), F32)], axis=0)
        q_aug.append(jnp.concatenate([qh, dropped.T.astype(BF16)], axis=1))
    return q_aug


def _attn_body(qa_ref, ka_ref, va_ref, e_ref, qb_ref, kb_ref, vb_ref, oa_ref, ob_ref):
    tq = ATT_TILE
    seq = ka_ref.shape[0]
    n_blocks = seq // MOBA_BLOCK
    kbar = [jnp.mean(ka_ref[n * MOBA_BLOCK:(n + 1) * MOBA_BLOCK, :].astype(F32), axis=0, keepdims=True)
            for n in range(n_blocks)]
    kbar = jnp.concatenate(kbar + [jnp.zeros((MAX_BLOCKS - n_blocks, LANES), F32)], axis=0).astype(BF16)

    def tile(qa, qb, j, states, q0, causal):
        rows = slice(j * tq, (j + 1) * tq)
        ka_aug = jnp.concatenate([ka_ref[rows, :], e_ref[rows, :]], axis=1)
        va_ones = _values_and_ones(va_ref[rows, :])
        vb_ones = _values_and_ones(vb_ref[rows, :])
        mask = functools.partial(_causal, q0=q0, k0=j * tq) if causal else (lambda s: s)
        new = []
        for hd in range(2):
            new.append(_softmax_step(mask(_dot_nt(qa[hd], ka_aug)), va_ones[hd], states[hd]))
        for hd in range(2):
            s = _dot_nt(qb[hd], kb_ref[rows, hd * LANES:(hd + 1) * LANES])
            new.append(_softmax_step(mask(s), vb_ones[hd], states[2 + hd]))
        return tuple(new)

    tq2 = 2 * tq
    for t in range(seq // tq2):
        qrows = slice(t * tq2, (t + 1) * tq2)
        qa = _moba_queries(qa_ref[qrows, :], kbar, t, n_blocks)
        qb = [qb_ref[qrows, hd * LANES:(hd + 1) * LANES] for hd in range(2)]
        states = (_softmax_init(tq2),) * 4
        for j in range(2 * t + 1):
            states = tile(qa, qb, j, states, t * tq2, j == 2 * t)
        low = tuple((m[tq:], acc[tq:]) for m, acc in states)
        low = tile([q[tq:] for q in qa], [q[tq:] for q in qb], 2 * t + 1, low, t * tq2 + tq, True)
        states = tuple((jnp.concatenate([m[:tq], ml], axis=0), jnp.concatenate([acc[:tq], accl], axis=0))
                       for (m, acc), (ml, accl) in zip(states, low))
        oa_ref[qrows, :] = _merge_pair(states[:2]).astype(oa_ref.dtype)
        ob_ref[qrows, :] = _merge_pair(states[2:]).astype(ob_ref.dtype)


def _attention(qa, ka, va, e_mat, qb, kb, vb):
    b, s, _ = qa.shape
    seq_spec = lambda w: pl.BlockSpec((None, s, w), lambda i, p: (i, 0, p))
    return pl.pallas_call(
        _attn_body,
        out_shape=[jax.ShapeDtypeStruct((b, s, A_WIDTH), BF16)] * 2,
        grid=(b, N_PAIRS),
        in_specs=[seq_spec(LANES), seq_spec(LANES), seq_spec(LANES), _resident(e_mat.shape),
                  seq_spec(2 * LANES), seq_spec(2 * LANES), seq_spec(LANES)],
        out_specs=[seq_spec(LANES)] * 2,
        compiler_params=_params(2),
        name="attention",
    )(qa, ka, va, e_mat, qb, kb, vb)


def _mixer_out_body(layer_ref, x_ref, oa_ref, ob_ref, ga_ref, gb_ref, kv_ref, wa_ref, wb_ref, wo_ref,
                    gx_ref, wxq_ref, wxo_ref, o_ref):
    merged = (ga_ref[...].astype(F32) * _dot(oa_ref[...], wa_ref[...])
              + gb_ref[...].astype(F32) * _dot(ob_ref[...], wb_ref[...]))
    x1 = x_ref[...] + _dot(merged.astype(BF16), wo_ref[...])

    q = _dot(_rms(x1, gx_ref[...]).astype(BF16), wxq_ref[...]).astype(BF16)
    scale = X_HEAD_DIM ** -0.5
    heads = []
    for hd in range(X_HEADS):
        lanes = slice(hd * X_HEAD_DIM, (hd + 1) * X_HEAD_DIM)
        s = _dot_nt(q[:, lanes], kv_ref[:, lanes])
        m = jnp.max(s, axis=-1, keepdims=True)
        p = jnp.exp((s - m) * scale)
        l = jnp.sum(p, axis=-1, keepdims=True)
        v = kv_ref[:, X_WIDTH + hd * X_HEAD_DIM:X_WIDTH + (hd + 1) * X_HEAD_DIM]
        heads.append((_dot(p.astype(BF16), v) * (1.0 / l)).astype(BF16))
    o_ref[...] = x1 + _dot(jnp.concatenate(heads, axis=1), wxo_ref[...])


def _mixer_out(layer, x, seq, oa, ob, ga, gb, mem_kv, w_a, w_b, w_out, norm_xattn, w_xq, w_xo):
    t, d = x.shape
    tm = min(OUT_TILE, seq)
    tiles_per_seq = seq // tm
    m = mem_kv.shape[2]
    tok = lambda w: pl.BlockSpec((tm, w), lambda i, l: (i, 0))
    weights = (w_a, w_b, w_out, norm_xattn, w_xq, w_xo)
    return pl.pallas_call(
        _mixer_out_body,
        out_shape=jax.ShapeDtypeStruct((t, d), F32),
        grid_spec=_layer_grid(
            (t // tm,),
            [tok(d), tok(A_WIDTH), tok(A_WIDTH), tok(d), tok(d),
             pl.BlockSpec((None, None, m, 2 * X_WIDTH), lambda i, l: (l[0], i // tiles_per_seq, 0, 0))]
            + [_layer_resident(w.shape) for w in weights],
            tok(d)),
        compiler_params=_params(1),
        name="mixer_out",
    )(layer, x, oa, ob, ga, gb, mem_kv, *weights)


FF_CHUNK = 1024


def _ffn_body(layer_ref, x_ref, g_ref, wgu_ref, wd_ref, gf_ref, o_ref, *, final):
    x = x_ref[...]
    h = _rms(x, g_ref[...]).astype(BF16)
    d_ff = wd_ref.shape[0]
    y = x
    for c0 in range(0, d_ff, FF_CHUNK):
        c1 = min(c0 + FF_CHUNK, d_ff)
        gt = _dot(h, wgu_ref[:, c0:c1])
        up = _dot(h, wgu_ref[:, d_ff + c0:d_ff + c1])
        a = (gt * jax.nn.sigmoid(gt) * up).astype(BF16)
        y = y + _dot(a, wd_ref[c0:c1, :])
    o_ref[...] = _rms(y, gf_ref[...]) if final else y


def _ffn(layer, x, norm_ffn, w_gate_up, w_down, norm_final, final):
    t, d = x.shape
    tm = min(TOK_TILE, t)
    tok = pl.BlockSpec((tm, d), lambda i, l: (i, 0))
    weights = (norm_ffn, w_gate_up, w_down)
    return pl.pallas_call(
        functools.partial(_ffn_body, final=final),
        out_shape=jax.ShapeDtypeStruct((t, d), F32),
        grid_spec=_layer_grid((t // tm,), [tok] + [_layer_resident(w.shape) for w in weights]
                              + [_resident(norm_final.shape)], tok),
        compiler_params=_params(1),
        name="ffn_final" if final else "ffn",
    )(layer, x, *weights, norm_final)


def _mla_head_cols(nope, rope):
    half = MLA_ROPE // 2
    z = lambda n: jnp.zeros(nope.shape[:-1] + (n,), nope.dtype)
    t1, t2 = (z(half), z(half)) if rope is None else (rope[..., :half], rope[..., half:])
    return jnp.concatenate([t1, nope[..., :HEAD_DIM - half], t2, nope[..., HEAD_DIM - half:], z(LANES - MLA_QK)], axis=-1)


def _prep_weights(w_in, w_uq, w_ukv):
    n_layers, d, _ = w_in.shape
    c = np.cumsum([A_WIDTH, A_WIDTH, A_WIDTH, Q_LORA, KV_LORA, MLA_ROPE, d])
    kpe = w_in[..., c[4]:c[5]]
    w_kpe = _mla_head_cols(jnp.zeros(kpe.shape[:-1] + (HEAD_DIM,), kpe.dtype), kpe)
    w_in_r = jnp.concatenate([w_in[..., :c[4]], w_kpe, w_in[..., c[5]:]], axis=-1).astype(BF16)
    uq = w_uq.reshape(n_layers, Q_LORA, N_HEADS, MLA_QK)
    w_uq_r = _mla_head_cols(uq[..., :HEAD_DIM], uq[..., HEAD_DIM:]).reshape(n_layers, Q_LORA, _KB_W).astype(BF16)
    ukv = w_ukv.reshape(n_layers, KV_LORA, N_HEADS, 2 * HEAD_DIM)
    w_k = _mla_head_cols(ukv[..., :HEAD_DIM], None).reshape(n_layers, KV_LORA, _KB_W)
    w_v = ukv[..., HEAD_DIM:].reshape(n_layers, KV_LORA, A_WIDTH)
    w_ukv_r = jnp.concatenate([w_k, w_v], axis=-1).astype(BF16)
    return w_in_r, w_uq_r, w_ukv_r


def kernel(x, mem, positions, norm_mix, w_in, q_lat_norm, w_uq, kv_lat_norm, w_ukv, w_branch_a, w_branch_b, w_out, norm_xattn, norm_mem, w_xq, w_xkv, w_xo, norm_ffn, w_gate_up, w_down, norm_final):
    b, s, d = x.shape
    n_layers = w_in.shape[0]
    t = b * s
    assert s % (2 * ATT_TILE) == 0 and ATT_TILE % MOBA_BLOCK == 0 and s // MOBA_BLOCK <= MAX_BLOCKS

    w_in_r, w_uq_r, w_ukv_r = _prep_weights(w_in, w_uq, w_ukv)
    bf = lambda w: w.astype(BF16)
    row = lambda g: g.reshape(g.shape[0], 1, g.shape[1])
    pos_lanes = jnp.broadcast_to(positions.reshape(t, 1).astype(F32), (t, LANES))
    tabs = _rope_tables(pos_lanes)
    mem_kv = _mem_kv(mem, row(norm_mem), bf(w_xkv))
    blk_of_row = np.arange(s) // MOBA_BLOCK
    e_mat = jnp.asarray(np.where(blk_of_row[:, None] == np.arange(LANES)[None, :], NEG_BIG, 0.0), BF16)
    norm_final = norm_final.reshape(1, d)

    norm_mix, q_lat_norm, kv_lat_norm = row(norm_mix), row(q_lat_norm), row(kv_lat_norm)
    norm_xattn, norm_ffn = row(norm_xattn), row(norm_ffn)
    w_a, w_b, w_o, w_q, w_xo_b = bf(w_branch_a), bf(w_branch_b), bf(w_out), bf(w_xq), bf(w_xo)
    w_gu, w_dn = bf(w_gate_up), bf(w_down)
    seq3 = lambda a: a.reshape(b, s, a.shape[-1])

    def layer(li, xt, final):
        li = jnp.full((1,), li, jnp.int32)
        qa, ka, va, qb, kb, vb, ga, gb = _mixer_in(li, xt, norm_mix, w_in_r, q_lat_norm, w_uq_r,
                                                   kv_lat_norm, w_ukv_r, tabs)
        oa, ob = _attention(seq3(qa), seq3(ka), seq3(va), e_mat, seq3(qb), seq3(kb), seq3(vb))
        oa, ob = oa.reshape(t, A_WIDTH), ob.reshape(t, A_WIDTH)
        xt = _mixer_out(li, xt, s, oa, ob, ga, gb, mem_kv, w_a, w_b, w_o, norm_xattn, w_q, w_xo_b)
        return _ffn(li, xt, norm_ffn, w_gu, w_dn, norm_final, final)

    xt = x.reshape(t, d)
    if n_layers > 1:
        xt = lax.fori_loop(0, n_layers - 1, lambda li, xt: layer(li, xt, False), xt)
    xt = layer(n_layers - 1, xt, True)
    return xt.reshape(b, s, d)
```

```python
import functools
import math

import numpy as np
import jax
import jax.numpy as jnp
from jax import lax
from jax.experimental import pallas as pl
from jax.experimental.pallas import tpu as pltpu

F32 = jnp.float32
BF16 = jnp.bfloat16

EPS = 1e-6
ROPE_THETA = 10000.0
LANES = 128
HEAD_DIM = 64
N_HEADS = 8
N_PAIRS = N_HEADS // 2
A_WIDTH = N_HEADS * HEAD_DIM
MLA_ROPE = 32
MLA_QK = HEAD_DIM + MLA_ROPE
Q_LORA = 384
KV_LORA = 256
MOBA_BLOCK = 256
MOBA_TOPK = 3
MAX_BLOCKS = 16
X_HEADS = 4
X_HEAD_DIM = 128
X_WIDTH = X_HEADS * X_HEAD_DIM
NEG_BIG = -1e30
LOG2E = math.log2(math.e)
VMEM_LIMIT = 56 * 1024 * 1024
TOK_TILE = 512
OUT_TILE = 1024
ATT_TILE = 512
TAB_W = 5 * LANES


def _resident(shape):
    return pl.BlockSpec(shape, lambda *_: (0,) * len(shape), pipeline_mode=pl.Buffered(1))


def _layer_resident(shape):
    return pl.BlockSpec((None,) + tuple(shape[1:]), lambda *a: (a[-1][0],) + (0,) * (len(shape) - 1),
                        pipeline_mode=pl.Buffered(1))


def _layer_grid(grid, in_specs, out_specs):
    return pltpu.PrefetchScalarGridSpec(num_scalar_prefetch=1, grid=grid, in_specs=in_specs, out_specs=out_specs)


def _params(n_axes):
    return pltpu.CompilerParams(dimension_semantics=("arbitrary",) * n_axes,
                                vmem_limit_bytes=VMEM_LIMIT)


def _rms(x, g):
    return x * lax.rsqrt(jnp.mean(x * x, axis=-1, keepdims=True) + EPS) * g


def _dot(a, b):
    return jnp.dot(a, b, preferred_element_type=F32)


def _dot_nt(a, b):
    return lax.dot_general(a, b, (((1,), (1,)), ((), ())), preferred_element_type=F32)


def _rope_table_body(pos_ref, tab_ref):
    pos = pos_ref[...]
    lane = lax.broadcasted_iota(jnp.int32, pos.shape, 1)
    inv = jnp.where(lane < 32,
                    jnp.exp((-math.log(ROPE_THETA) * (2.0 / HEAD_DIM)) * lane.astype(F32)),
                    jnp.exp((-math.log(ROPE_THETA) * (2.0 / MLA_ROPE)) * (lane - 32).astype(F32)))
    ang = pos * inv
    cos, sin = jnp.cos(ang), jnp.sin(ang)

    def spread(v):
        r32, r96 = pltpu.roll(v, 32, axis=1), pltpu.roll(v, 96, axis=1)
        tiled = jnp.where(lane < 32, v, jnp.where(lane < 64, r32, jnp.where(
            lane < 96, pltpu.roll(v, 64, axis=1), r96)))
        return tiled, jnp.where(lane < 64, r96, r32)

    cos_a, cos_b = spread(cos)
    sin_a, sin_b = spread(sin)
    first_half = lane % 64 < 32
    is_rope = lane % 64 < MLA_ROPE // 2
    tab_ref[:, 0:LANES] = cos_a
    tab_ref[:, LANES:2 * LANES] = jnp.where(first_half, -sin_a, 0.0)
    tab_ref[:, 2 * LANES:3 * LANES] = jnp.where(first_half, 0.0, sin_a)
    tab_ref[:, 3 * LANES:4 * LANES] = jnp.where(is_rope, cos_b, 1.0)
    tab_ref[:, 4 * LANES:5 * LANES] = jnp.where(is_rope, jnp.where(lane < 64, -sin_b, sin_b), 0.0)


def _rope_tables(pos_lanes):
    t = pos_lanes.shape[0]
    tm = min(1024, t)
    return pl.pallas_call(
        _rope_table_body,
        out_shape=jax.ShapeDtypeStruct((t, TAB_W), F32),
        grid=(t // tm,),
        in_specs=[pl.BlockSpec((tm, LANES), lambda i: (i, 0))],
        out_specs=pl.BlockSpec((tm, TAB_W), lambda i: (i, 0)),
        compiler_params=_params(1),
        name="rope_tables",
    )(pos_lanes)


MEM_BATCH = 4


def _mem_kv_body(mem_ref, g_ref, w_ref, o_ref):
    nb, m, d = mem_ref.shape
    h = _rms(mem_ref[...].reshape(nb * m, d), g_ref[...]).astype(BF16)
    o_ref[...] = _dot(h, w_ref[...]).astype(BF16).reshape(o_ref.shape)


def _mem_kv(mem, norm_mem, w_xkv):
    b, m, d = mem.shape
    n_layers = w_xkv.shape[0]
    nb = math.gcd(b, MEM_BATCH)
    return pl.pallas_call(
        _mem_kv_body,
        out_shape=jax.ShapeDtypeStruct((n_layers, b, m, 2 * X_WIDTH), BF16),
        grid=(n_layers, b // nb),
        in_specs=[pl.BlockSpec((nb, m, d), lambda l, i: (i, 0, 0)),
                  pl.BlockSpec((None, 1, d), lambda l, i: (l, 0, 0)),
                  pl.BlockSpec((None, d, 2 * X_WIDTH), lambda l, i: (l, 0, 0))],
        out_specs=pl.BlockSpec((None, nb, m, 2 * X_WIDTH), lambda l, i: (l, i, 0, 0)),
        compiler_params=_params(2),
        name="mem_kv",
    )(mem, norm_mem, w_xkv)


_C_QK = 0
_C_VA = 2 * A_WIDTH
_C_LAT = 3 * A_WIDTH
_C_G = _C_LAT + Q_LORA + KV_LORA + LANES
_KB_W = N_HEADS * LANES


def _mixer_in_body(layer_ref, x_ref, g_ref, win_ref, qg_ref, wuq_ref, kvg_ref, wukv_ref, tab_ref,
                   qa_ref, ka_ref, va_ref, qb_ref, kb_ref, vb_ref, ga_ref, gb_ref):
    d = x_ref.shape[-1]
    h = _rms(x_ref[...], g_ref[...]).astype(BF16)
    cos_a, sa_lo, sa_hi = tab_ref[:, 0:LANES], tab_ref[:, LANES:2 * LANES], tab_ref[:, 2 * LANES:3 * LANES]
    cos_b, sin_b = tab_ref[:, 3 * LANES:4 * LANES], tab_ref[:, 4 * LANES:5 * LANES]

    def rope_a(y):
        return (y * cos_a + pltpu.roll(y, LANES - 32, axis=1) * sa_lo
                + pltpu.roll(y, 32, axis=1) * sa_hi)

    def rope_b(y):
        return y * cos_b + pltpu.roll(y, 64, axis=1) * sin_b

    scale_a = HEAD_DIM ** -0.5 * LOG2E
    scale_b = MLA_QK ** -0.5 * LOG2E
    y = _dot(h, win_ref[:, _C_QK:_C_QK + 2 * A_WIDTH])
    for p in range(N_PAIRS):
        sl = slice(p * LANES, (p + 1) * LANES)
        qa_ref[:, sl] = (rope_a(y[:, sl]) * scale_a).astype(BF16)
        ka_ref[:, sl] = rope_a(y[:, A_WIDTH + p * LANES:A_WIDTH + (p + 1) * LANES]).astype(BF16)

    va_ref[...] = _dot(h, win_ref[:, _C_VA:_C_VA + A_WIDTH]).astype(BF16)

    lat = _dot(h, win_ref[:, _C_LAT:_C_G])
    q_lat = _rms(lat[:, 0:Q_LORA], qg_ref[...]).astype(BF16)
    kv_lat = _rms(lat[:, Q_LORA:Q_LORA + KV_LORA], kvg_ref[...]).astype(BF16)
    k_pe = rope_b(lat[:, Q_LORA + KV_LORA:])
    yq = _dot(q_lat, wuq_ref[...])
    ykv = _dot(kv_lat, wukv_ref[...])
    for hd in range(N_HEADS):
        sl = slice(hd * LANES, (hd + 1) * LANES)
        qb_ref[:, sl] = (rope_b(yq[:, sl]) * scale_b).astype(BF16)
        kb_ref[:, sl] = (ykv[:, sl] + k_pe).astype(BF16)
    vb_ref[...] = ykv[:, _KB_W:].astype(BF16)

    ga_ref[...] = jax.nn.sigmoid(_dot(h, win_ref[:, _C_G:_C_G + d])).astype(BF16)
    gb_ref[...] = jax.nn.sigmoid(_dot(h, win_ref[:, _C_G + d:_C_G + 2 * d])).astype(BF16)


def _mixer_in(layer, x, norm_mix, w_in, q_lat_norm, w_uq, kv_lat_norm, w_ukv, tabs):
    t, d = x.shape
    tm = min(TOK_TILE, t)
    tok = lambda w: pl.BlockSpec((tm, w), lambda i, l: (i, 0))
    widths = (A_WIDTH, A_WIDTH, A_WIDTH, _KB_W, _KB_W, A_WIDTH, d, d)
    weights = (norm_mix, w_in, q_lat_norm, w_uq, kv_lat_norm, w_ukv)
    return pl.pallas_call(
        _mixer_in_body,
        out_shape=[jax.ShapeDtypeStruct((t, w), BF16) for w in widths],
        grid_spec=_layer_grid((t // tm,), [tok(d)] + [_layer_resident(w.shape) for w in weights] + [tok(TAB_W)],
                              [tok(w) for w in widths]),
        compiler_params=_params(1),
        name="mixer_in",
    )(layer, x, *weights, tabs)


def _softmax_init(tq):
    return (jnp.full((tq, 1), NEG_BIG, F32), jnp.zeros((tq, LANES), F32))


def _softmax_step(s, v_ones, state):
    m, acc = state
    m_new = jnp.maximum(m, jnp.max(s, axis=-1, keepdims=True))
    p = jnp.exp2(s - m_new)
    acc_new = jnp.exp2(m - m_new) * acc + _dot(p.astype(BF16), v_ones)
    return m_new, acc_new


def _values_and_ones(v):
    lane = lax.broadcasted_iota(jnp.int32, v.shape, 1)
    one = jnp.ones_like(v)
    return jnp.where(lane < HEAD_DIM, v, one), jnp.where(lane < HEAD_DIM, one, v)


def _causal(s, q0, k0):
    row = lax.broadcasted_iota(jnp.int32, s.shape, 0) + q0
    col = lax.broadcasted_iota(jnp.int32, s.shape, 1) + k0
    return jnp.where(col <= row, s, NEG_BIG)


def _merge_pair(states):
    (_, acc0), (_, acc1) = states
    lane = lax.broadcasted_iota(jnp.int32, acc0.shape, 1)
    num = jnp.where(lane < HEAD_DIM, acc0, acc1)
    den = pltpu.roll(jnp.where(lane < HEAD_DIM, acc1, acc0), HEAD_DIM, axis=1)
    return num / den


def _moba_queries(q, kbar, qi, n_blocks):
    tq = q.shape[0]
    lane = lax.broadcasted_iota(jnp.int32, q.shape, 1)
    row = lax.broadcasted_iota(jnp.int32, (MAX_BLOCKS, tq), 0)
    col = lax.broadcasted_iota(jnp.int32, (MAX_BLOCKS, tq), 1)
    cur = qi * (tq // MOBA_BLOCK) + col // MOBA_BLOCK
    q_aug = []
    for hd in range(2):
        qh = jnp.where((lane < HEAD_DIM) == (hd == 0), q, jnp.zeros_like(q))
        gate = jnp.where(row < cur, _dot_nt(kbar, qh), -jnp.inf)
        rank = jnp.zeros(gate.shape, jnp.int32)
        for n in range(n_blocks):
            gn = gate[n:n + 1, :]
            beats = (gn > gate) | ((gn == gate) & (row > n))
            rank = rank + beats.astype(jnp.int32)
        keep = ((row < cur) & (rank < MOBA_TOPK)) | (row == cur)
        dropped = jnp.where(keep, 0.0, 1.0)
        dropped = jnp.concatenate([dropped, jnp.zeros((LANES - MAX_BLOCKS, tq), F32)], axis=0)
        q_aug.append(jnp.concatenate([qh, dropped.T.astype(BF16)], axis=1))
    return q_aug


def _attn_body(qa_ref, ka_ref, va_ref, e_ref, qb_ref, kb_ref, vb_ref, oa_ref, ob_ref):
    tq = ATT_TILE
    seq = ka_ref.shape[0]
    n_blocks = seq // MOBA_BLOCK
    kbar = [jnp.mean(ka_ref[n * MOBA_BLOCK:(n + 1) * MOBA_BLOCK, :].astype(F32), axis=0, keepdims=True)
            for n in range(n_blocks)]
    kbar = jnp.concatenate(kbar + [jnp.zeros((MAX_BLOCKS - n_blocks, LANES), F32)], axis=0).astype(BF16)

    def tile(qa, qb, j, states, q0, causal):
        rows = slice(j * tq, (j + 1) * tq)
        ka_aug = jnp.concatenate([ka_ref[rows, :], e_ref[rows, :]], axis=1)
        va_ones = _values_and_ones(va_ref[rows, :])
        vb_ones = _values_and_ones(vb_ref[rows, :])
        mask = functools.partial(_causal, q0=q0, k0=j * tq) if causal else (lambda s: s)
        new = []
        for hd in range(2):
            new.append(_softmax_step(mask(_dot_nt(qa[hd], ka_aug)), va_ones[hd], states[hd]))
        for hd in range(2):
            s = _dot_nt(qb[hd], kb_ref[rows, hd * LANES:(hd + 1) * LANES])
            new.append(_softmax_step(mask(s), vb_ones[hd], states[2 + hd]))
        return tuple(new)

    tq2 = 2 * tq
    for t in range(seq // tq2):
        qrows = slice(t * tq2, (t + 1) * tq2)
        qa = _moba_queries(qa_ref[qrows, :], kbar, t, n_blocks)
        qb = [qb_ref[qrows, hd * LANES:(hd + 1) * LANES] for hd in range(2)]
        states = (_softmax_init(tq2),) * 4
        for j in range(2 * t + 1):
            states = tile(qa, qb, j, states, t * tq2, j == 2 * t)
        low = tuple((m[tq:], acc[tq:]) for m, acc in states)
        low = tile([q[tq:] for q in qa], [q[tq:] for q in qb], 2 * t + 1, low, t * tq2 + tq, True)
        states = tuple((jnp.concatenate([m[:tq], ml], axis=0), jnp.concatenate([acc[:tq], accl], axis=0))
                       for (m, acc), (ml, accl) in zip(states, low))
        oa_ref[qrows, :] = _merge_pair(states[:2]).astype(oa_ref.dtype)
        ob_ref[qrows, :] = _merge_pair(states[2:]).astype(ob_ref.dtype)


def _attention(qa, ka, va, e_mat, qb, kb, vb):
    b, s, _ = qa.shape
    seq_spec = lambda w: pl.BlockSpec((None, s, w), lambda i, p: (i, 0, p))
    return pl.pallas_call(
        _attn_body,
        out_shape=[jax.ShapeDtypeStruct((b, s, A_WIDTH), BF16)] * 2,
        grid=(b, N_PAIRS),
        in_specs=[seq_spec(LANES), seq_spec(LANES), seq_spec(LANES), _resident(e_mat.shape),
                  seq_spec(2 * LANES), seq_spec(2 * LANES), seq_spec(LANES)],
        out_specs=[seq_spec(LANES)] * 2,
        compiler_params=_params(2),
        name="attention",
    )(qa, ka, va, e_mat, qb, kb, vb)


def _mixer_out_body(layer_ref, x_ref, oa_ref, ob_ref, ga_ref, gb_ref, kv_ref, wa_ref, wb_ref, wo_ref,
                    gx_ref, wxq_ref, wxo_ref, o_ref):
    merged = (ga_ref[...].astype(F32) * _dot(oa_ref[...], wa_ref[...])
              + gb_ref[...].astype(F32) * _dot(ob_ref[...], wb_ref[...]))
    x1 = x_ref[...] + _dot(merged.astype(BF16), wo_ref[...])

    q = _dot(_rms(x1, gx_ref[...]).astype(BF16), wxq_ref[...]).astype(BF16)
    scale = X_HEAD_DIM ** -0.5
    heads = []
    for hd in range(X_HEADS):
        lanes = slice(hd * X_HEAD_DIM, (hd + 1) * X_HEAD_DIM)
        s = _dot_nt(q[:, lanes], kv_ref[:, lanes])
        m = jnp.max(s, axis=-1, keepdims=True)
        p = jnp.exp((s - m) * scale)
        l = jnp.sum(p, axis=-1, keepdims=True)
        v = kv_ref[:, X_WIDTH + hd * X_HEAD_DIM:X_WIDTH + (hd + 1) * X_HEAD_DIM]
        heads.append((_dot(p.astype(BF16), v) * (1.0 / l)).astype(BF16))
    o_ref[...] = x1 + _dot(jnp.concatenate(heads, axis=1), wxo_ref[...])


def _mixer_out(layer, x, seq, oa, ob, ga, gb, mem_kv, w_a, w_b, w_out, norm_xattn, w_xq, w_xo):
    t, d = x.shape
    tm = min(OUT_TILE, seq)
    tiles_per_seq = seq // tm
    m = mem_kv.shape[2]
    tok = lambda w: pl.BlockSpec((tm, w), lambda i, l: (i, 0))
    weights = (w_a, w_b, w_out, norm_xattn, w_xq, w_xo)
    return pl.pallas_call(
        _mixer_out_body,
        out_shape=jax.ShapeDtypeStruct((t, d), F32),
        grid_spec=_layer_grid(
            (t // tm,),
            [tok(d), tok(A_WIDTH), tok(A_WIDTH), tok(d), tok(d),
             pl.BlockSpec((None, None, m, 2 * X_WIDTH), lambda i, l: (l[0], i // tiles_per_seq, 0, 0))]
            + [_layer_resident(w.shape) for w in weights],
            tok(d)),
        compiler_params=_params(1),
        name="mixer_out",
    )(layer, x, oa, ob, ga, gb, mem_kv, *weights)


FF_CHUNK = 1024


def _ffn_body(layer_ref, x_ref, g_ref, wgu_ref, wd_ref, gf_ref, o_ref, *, final):
    x = x_ref[...]
    h = _rms(x, g_ref[...]).astype(BF16)
    d_ff = wd_ref.shape[0]
    y = x
    for c0 in range(0, d_ff, FF_CHUNK):
        c1 = min(c0 + FF_CHUNK, d_ff)
        gt = _dot(h, wgu_ref[:, c0:c1])
        up = _dot(h, wgu_ref[:, d_ff + c0:d_ff + c1])
        a = (gt * jax.nn.sigmoid(gt) * up).astype(BF16)
        y = y + _dot(a, wd_ref[c0:c1, :])
    o_ref[...] = _rms(y, gf_ref[...]) if final else y


def _ffn(layer, x, norm_ffn, w_gate_up, w_down, norm_final, final):
    t, d = x.shape
    tm = min(OUT_TILE, t)
    tok = pl.BlockSpec((tm, d), lambda i, l: (i, 0))
    weights = (norm_ffn, w_gate_up, w_down)
    return pl.pallas_call(
        functools.partial(_ffn_body, final=final),
        out_shape=jax.ShapeDtypeStruct((t, d), F32),
        grid_spec=_layer_grid((t // tm,), [tok] + [_layer_resident(w.shape) for w in weights]
                              + [_resident(norm_final.shape)], tok),
        compiler_params=_params(1),
        name="ffn_final" if final else "ffn",
    )(layer, x, *weights, norm_final)


def _mla_head_cols(nope, rope):
    half = MLA_ROPE // 2
    z = lambda n: jnp.zeros(nope.shape[:-1] + (n,), nope.dtype)
    t1, t2 = (z(half), z(half)) if rope is None else (rope[..., :half], rope[..., half:])
    return jnp.concatenate([t1, nope[..., :HEAD_DIM - half], t2, nope[..., HEAD_DIM - half:], z(LANES - MLA_QK)], axis=-1)


def _prep_weights(w_in, w_uq, w_ukv):
    n_layers, d, _ = w_in.shape
    c = np.cumsum([A_WIDTH, A_WIDTH, A_WIDTH, Q_LORA, KV_LORA, MLA_ROPE, d])
    kpe = w_in[..., c[4]:c[5]]
    w_kpe = _mla_head_cols(jnp.zeros(kpe.shape[:-1] + (HEAD_DIM,), kpe.dtype), kpe)
    w_in_r = jnp.concatenate([w_in[..., :c[4]], w_kpe, w_in[..., c[5]:]], axis=-1).astype(BF16)
    uq = w_uq.reshape(n_layers, Q_LORA, N_HEADS, MLA_QK)
    w_uq_r = _mla_head_cols(uq[..., :HEAD_DIM], uq[..., HEAD_DIM:]).reshape(n_layers, Q_LORA, _KB_W).astype(BF16)
    ukv = w_ukv.reshape(n_layers, KV_LORA, N_HEADS, 2 * HEAD_DIM)
    w_k = _mla_head_cols(ukv[..., :HEAD_DIM], None).reshape(n_layers, KV_LORA, _KB_W)
    w_v = ukv[..., HEAD_DIM:].reshape(n_layers, KV_LORA, A_WIDTH)
    w_ukv_r = jnp.concatenate([w_k, w_v], axis=-1).astype(BF16)
    return w_in_r, w_uq_r, w_ukv_r


def kernel(x, mem, positions, norm_mix, w_in, q_lat_norm, w_uq, kv_lat_norm, w_ukv, w_branch_a, w_branch_b, w_out, norm_xattn, norm_mem, w_xq, w_xkv, w_xo, norm_ffn, w_gate_up, w_down, norm_final):
    b, s, d = x.shape
    n_layers = w_in.shape[0]
    t = b * s
    assert s % (2 * ATT_TILE) == 0 and ATT_TILE % MOBA_BLOCK == 0 and s // MOBA_BLOCK <= MAX_BLOCKS

    w_in_r, w_uq_r, w_ukv_r = _prep_weights(w_in, w_uq, w_ukv)
    bf = lambda w: w.astype(BF16)
    row = lambda g: g.reshape(g.shape[0], 1, g.shape[1])
    pos_lanes = jnp.broadcast_to(positions.reshape(t, 1).astype(F32), (t, LANES))
    tabs = _rope_tables(pos_lanes)
    mem_kv = _mem_kv(mem, row(norm_mem), bf(w_xkv))
    blk_of_row = np.arange(s) // MOBA_BLOCK
    e_mat = jnp.asarray(np.where(blk_of_row[:, None] == np.arange(LANES)[None, :], NEG_BIG, 0.0), BF16)
    norm_final = norm_final.reshape(1, d)

    norm_mix, q_lat_norm, kv_lat_norm = row(norm_mix), row(q_lat_norm), row(kv_lat_norm)
    norm_xattn, norm_ffn = row(norm_xattn), row(norm_ffn)
    w_a, w_b, w_o, w_q, w_xo_b = bf(w_branch_a), bf(w_branch_b), bf(w_out), bf(w_xq), bf(w_xo)
    w_gu, w_dn = bf(w_gate_up), bf(w_down)
    seq3 = lambda a: a.reshape(b, s, a.shape[-1])

    def layer(li, xt, final):
        li = jnp.full((1,), li, jnp.int32)
        qa, ka, va, qb, kb, vb, ga, gb = _mixer_in(li, xt, norm_mix, w_in_r, q_lat_norm, w_uq_r,
                                                   kv_lat_norm, w_ukv_r, tabs)
        oa, ob = _attention(seq3(qa), seq3(ka), seq3(va), e_mat, seq3(qb), seq3(kb), seq3(vb))
        oa, ob = oa.reshape(t, A_WIDTH), ob.reshape(t, A_WIDTH)
        xt = _mixer_out(li, xt, s, oa, ob, ga, gb, mem_kv, w_a, w_b, w_o, norm_xattn, w_q, w_xo_b)
        return _ffn(li, xt, norm_ffn, w_gu, w_dn, norm_final, final)

    xt = x.reshape(t, d)
    for li in range(n_layers):
        xt = layer(li, xt, li == n_layers - 1)
    return xt.reshape(b, s, d)
```

```python
import functools
import math

import numpy as np
import jax
import jax.numpy as jnp
from jax import lax
from jax.experimental import pallas as pl
from jax.experimental.pallas import tpu as pltpu

F32 = jnp.float32
BF16 = jnp.bfloat16

EPS = 1e-6
ROPE_THETA = 10000.0
LANES = 128
HEAD_DIM = 64
N_HEADS = 8
N_PAIRS = N_HEADS // 2
A_WIDTH = N_HEADS * HEAD_DIM
MLA_ROPE = 32
MLA_QK = HEAD_DIM + MLA_ROPE
Q_LORA = 384
KV_LORA = 256
MOBA_BLOCK = 256
MOBA_TOPK = 3
MAX_BLOCKS = 16
X_HEADS = 4
X_HEAD_DIM = 128
X_WIDTH = X_HEADS * X_HEAD_DIM
NEG_BIG = -1e30
LOG2E = math.log2(math.e)
VMEM_LIMIT = 56 * 1024 * 1024
TOK_TILE = 512
OUT_TILE = 1024
ATT_TILE = 512
TAB_W = 5 * LANES


def _resident(shape):
    return pl.BlockSpec(shape, lambda *_: (0,) * len(shape), pipeline_mode=pl.Buffered(1))


def _layer_resident(shape):
    return pl.BlockSpec((None,) + tuple(shape[1:]), lambda *a: (a[-1][0],) + (0,) * (len(shape) - 1),
                        pipeline_mode=pl.Buffered(1))


def _layer_grid(grid, in_specs, out_specs):
    return pltpu.PrefetchScalarGridSpec(num_scalar_prefetch=1, grid=grid, in_specs=in_specs, out_specs=out_specs)


def _params(n_axes):
    return pltpu.CompilerParams(dimension_semantics=("arbitrary",) * n_axes,
                                vmem_limit_bytes=VMEM_LIMIT)


def _rms(x, g):
    return x * lax.rsqrt(jnp.mean(x * x, axis=-1, keepdims=True) + EPS) * g


def _dot(a, b):
    return jnp.dot(a, b, preferred_element_type=F32)


def _dot_nt(a, b):
    return lax.dot_general(a, b, (((1,), (1,)), ((), ())), preferred_element_type=F32)


def _rope_table_body(pos_ref, tab_ref):
    pos = pos_ref[...]
    lane = lax.broadcasted_iota(jnp.int32, pos.shape, 1)
    sub = lane % 64
    inv = jnp.where(sub < 32,
                    jnp.exp((-math.log(ROPE_THETA) * (2.0 / HEAD_DIM)) * sub.astype(F32)),
                    jnp.exp((-math.log(ROPE_THETA) * (2.0 / MLA_ROPE)) * (sub - 32).astype(F32)))
    ang = pos * inv
    cos2, sin2 = jnp.cos(ang), jnp.sin(ang)

    def spread(v):
        r32, r96 = pltpu.roll(v, 32, axis=1), pltpu.roll(v, 96, axis=1)
        tiled = jnp.where(lane < 32, v, jnp.where(lane < 64, r32, jnp.where(
            lane < 96, pltpu.roll(v, 64, axis=1), r96)))
        return tiled, jnp.where(lane < 64, r96, r32)

    first_half = sub < 32
    is_rope = sub < MLA_ROPE // 2
    for half in range(2):
        cos, sin = (cos2, sin2) if half == 0 else (pltpu.roll(cos2, 64, axis=1), pltpu.roll(sin2, 64, axis=1))
        cos_a, cos_b = spread(cos)
        sin_a, sin_b = spread(sin)
        tab_ref[half, :, 0:LANES] = cos_a
        tab_ref[half, :, LANES:2 * LANES] = jnp.where(first_half, -sin_a, 0.0)
        tab_ref[half, :, 2 * LANES:3 * LANES] = jnp.where(first_half, 0.0, sin_a)
        tab_ref[half, :, 3 * LANES:4 * LANES] = jnp.where(is_rope, cos_b, 1.0)
        tab_ref[half, :, 4 * LANES:5 * LANES] = jnp.where(is_rope, jnp.where(lane < 64, -sin_b, sin_b), 0.0)


def _rope_tables(positions):
    t = positions.size
    half = t // 2
    pos = positions.reshape(2, half, 1).astype(F32)
    pos2 = jnp.concatenate([jnp.broadcast_to(pos[0], (half, 64)), jnp.broadcast_to(pos[1], (half, 64))], axis=1)
    tm = min(1024, half)
    tabs = pl.pallas_call(
        _rope_table_body,
        out_shape=jax.ShapeDtypeStruct((2, half, TAB_W), F32),
        grid=(half // tm,),
        in_specs=[pl.BlockSpec((tm, LANES), lambda i: (i, 0))],
        out_specs=pl.BlockSpec((2, tm, TAB_W), lambda i: (0, i, 0)),
        compiler_params=_params(1),
        name="rope_tables",
    )(pos2)
    return tabs.reshape(t, TAB_W)


MEM_BATCH = 4


def _mem_kv_body(mem_ref, g_ref, w_ref, o_ref):
    nb, m, d = mem_ref.shape
    h = _rms(mem_ref[...].reshape(nb * m, d), g_ref[...]).astype(BF16)
    o_ref[...] = _dot(h, w_ref[...]).astype(BF16).reshape(o_ref.shape)


def _mem_kv(mem, norm_mem, w_xkv):
    b, m, d = mem.shape
    n_layers = w_xkv.shape[0]
    nb = math.gcd(b, MEM_BATCH)
    return pl.pallas_call(
        _mem_kv_body,
        out_shape=jax.ShapeDtypeStruct((n_layers, b, m, 2 * X_WIDTH), BF16),
        grid=(n_layers, b // nb),
        in_specs=[pl.BlockSpec((nb, m, d), lambda l, i: (i, 0, 0)),
                  pl.BlockSpec((None, 1, d), lambda l, i: (l, 0, 0)),
                  pl.BlockSpec((None, d, 2 * X_WIDTH), lambda l, i: (l, 0, 0))],
        out_specs=pl.BlockSpec((None, nb, m, 2 * X_WIDTH), lambda l, i: (l, i, 0, 0)),
        compiler_params=_params(2),
        name="mem_kv",
    )(mem, norm_mem, w_xkv)


_C_QK = 0
_C_VA = 2 * A_WIDTH
_C_LAT = 3 * A_WIDTH
_C_G = _C_LAT + Q_LORA + KV_LORA + LANES
_KB_W = N_HEADS * LANES


def _mixer_in_body(layer_ref, x_ref, g_ref, win_ref, qg_ref, wuq_ref, kvg_ref, wukv_ref, tab_ref,
                   qa_ref, ka_ref, va_ref, qb_ref, kb_ref, vb_ref, ga_ref, gb_ref):
    d = x_ref.shape[-1]
    h = _rms(x_ref[...], g_ref[...]).astype(BF16)
    cos_a, sa_lo, sa_hi = tab_ref[:, 0:LANES], tab_ref[:, LANES:2 * LANES], tab_ref[:, 2 * LANES:3 * LANES]
    cos_b, sin_b = tab_ref[:, 3 * LANES:4 * LANES], tab_ref[:, 4 * LANES:5 * LANES]

    def rope_a(y):
        return (y * cos_a + pltpu.roll(y, LANES - 32, axis=1) * sa_lo
                + pltpu.roll(y, 32, axis=1) * sa_hi)

    def rope_b(y):
        return y * cos_b + pltpu.roll(y, 64, axis=1) * sin_b

    scale_a = HEAD_DIM ** -0.5 * LOG2E
    scale_b = MLA_QK ** -0.5 * LOG2E
    y = _dot(h, win_ref[:, _C_QK:_C_QK + 2 * A_WIDTH])
    for p in range(N_PAIRS):
        sl = slice(p * LANES, (p + 1) * LANES)
        qa_ref[:, sl] = (rope_a(y[:, sl]) * scale_a).astype(BF16)
        ka_ref[:, sl] = rope_a(y[:, A_WIDTH + p * LANES:A_WIDTH + (p + 1) * LANES]).astype(BF16)

    va_ref[...] = _dot(h, win_ref[:, _C_VA:_C_VA + A_WIDTH]).astype(BF16)

    lat = _dot(h, win_ref[:, _C_LAT:_C_G])
    q_lat = _rms(lat[:, 0:Q_LORA], qg_ref[...]).astype(BF16)
    kv_lat = _rms(lat[:, Q_LORA:Q_LORA + KV_LORA], kvg_ref[...]).astype(BF16)
    k_pe = rope_b(lat[:, Q_LORA + KV_LORA:])
    yq = _dot(q_lat, wuq_ref[...])
    ykv = _dot(kv_lat, wukv_ref[...])
    for hd in range(N_HEADS):
        sl = slice(hd * LANES, (hd + 1) * LANES)
        qb_ref[:, sl] = (rope_b(yq[:, sl]) * scale_b).astype(BF16)
        kb_ref[:, sl] = (ykv[:, sl] + k_pe).astype(BF16)
    vb_ref[...] = ykv[:, _KB_W:].astype(BF16)

    ga_ref[...] = jax.nn.sigmoid(_dot(h, win_ref[:, _C_G:_C_G + d])).astype(BF16)
    gb_ref[...] = jax.nn.sigmoid(_dot(h, win_ref[:, _C_G + d:_C_G + 2 * d])).astype(BF16)


def _mixer_in(layer, x, norm_mix, w_in, q_lat_norm, w_uq, kv_lat_norm, w_ukv, tabs):
    t, d = x.shape
    tm = min(TOK_TILE, t)
    tok = lambda w: pl.BlockSpec((tm, w), lambda i, l: (i, 0))
    widths = (A_WIDTH, A_WIDTH, A_WIDTH, _KB_W, _KB_W, A_WIDTH, d, d)
    weights = (norm_mix, w_in, q_lat_norm, w_uq, kv_lat_norm, w_ukv)
    return pl.pallas_call(
        _mixer_in_body,
        out_shape=[jax.ShapeDtypeStruct((t, w), BF16) for w in widths],
        grid_spec=_layer_grid((t // tm,), [tok(d)] + [_layer_resident(w.shape) for w in weights] + [tok(TAB_W)],
                              [tok(w) for w in widths]),
        compiler_params=_params(1),
        name="mixer_in",
    )(layer, x, *weights, tabs)


def _softmax_init(tq):
    return (jnp.full((tq, 1), NEG_BIG, F32), jnp.zeros((tq, LANES), F32))


def _softmax_step(s, v_ones, state):
    m, acc = state
    m_new = jnp.maximum(m, jnp.max(s, axis=-1, keepdims=True))
    p = jnp.exp2(s - m_new)
    acc_new = jnp.exp2(m - m_new) * acc + _dot(p.astype(BF16), v_ones)
    return m_new, acc_new


def _values_and_ones(v):
    lane = lax.broadcasted_iota(jnp.int32, v.shape, 1)
    one = jnp.ones_like(v)
    return jnp.where(lane < HEAD_DIM, v, one), jnp.where(lane < HEAD_DIM, one, v)


def _causal(s, q0, k0):
    row = lax.broadcasted_iota(jnp.int32, s.shape, 0) + q0
    col = lax.broadcasted_iota(jnp.int32, s.shape, 1) + k0
    return jnp.where(col <= row, s, NEG_BIG)


def _merge_pair(states):
    (_, acc0), (_, acc1) = states
    lane = lax.broadcasted_iota(jnp.int32, acc0.shape, 1)
    num = jnp.where(lane < HEAD_DIM, acc0, acc1)
    den = pltpu.roll(jnp.where(lane < HEAD_DIM, acc1, acc0), HEAD_DIM, axis=1)
    return num / den


def _moba_queries(q, kbar, qi, n_blocks):
    tq = q.shape[0]
    lane = lax.broadcasted_iota(jnp.int32, q.shape, 1)
    row = lax.broadcasted_iota(jnp.int32, (MAX_BLOCKS, tq), 0)
    col = lax.broadcasted_iota(jnp.int32, (MAX_BLOCKS, tq), 1)
    cur = qi * (tq // MOBA_BLOCK) + col // MOBA_BLOCK
    q_aug = []
    for hd in range(2):
        qh = jnp.where((lane < HEAD_DIM) == (hd == 0), q, jnp.zeros_like(q))
        gate = jnp.where(row < cur, _dot_nt(kbar, qh), -jnp.inf)
        rank = jnp.zeros(gate.shape, jnp.int32)
        for n in range(n_blocks):
            gn = gate[n:n + 1, :]
            beats = (gn > gate) | ((gn == gate) & (row > n))
            rank = rank + beats.astype(jnp.int32)
        keep = ((row < cur) & (rank < MOBA_TOPK)) | (row == cur)
        dropped = jnp.where(keep, 0.0, 1.0)
        dropped = jnp.concatenate([dropped, jnp.zeros((LANES - MAX_BLOCKS, tq), F32)], axis=0)
        q_aug.append(jnp.concatenate([qh, dropped.T.astype(BF16)], axis=1))
    return q_aug


def _attn_body(qa_ref, ka_ref, va_ref, e_ref, qb_ref, kb_ref, vb_ref, oa_ref, ob_ref):
    tq = ATT_TILE
    seq = ka_ref.shape[0]
    n_blocks = seq // MOBA_BLOCK
    kbar = [jnp.mean(ka_ref[n * MOBA_BLOCK:(n + 1) * MOBA_BLOCK, :].astype(F32), axis=0, keepdims=True)
            for n in range(n_blocks)]
    kbar = jnp.concatenate(kbar + [jnp.zeros((MAX_BLOCKS - n_blocks, LANES), F32)], axis=0).astype(BF16)

    def tile(qa, qb, j, states, q0, causal):
        rows = slice(j * tq, (j + 1) * tq)
        ka_aug = jnp.concatenate([ka_ref[rows, :], e_ref[rows, :]], axis=1)
        va_ones = _values_and_ones(va_ref[rows, :])
        vb_ones = _values_and_ones(vb_ref[rows, :])
        mask = functools.partial(_causal, q0=q0, k0=j * tq) if causal else (lambda s: s)
        new = []
        for hd in range(2):
            new.append(_softmax_step(mask(_dot_nt(qa[hd], ka_aug)), va_ones[hd], states[hd]))
        for hd in range(2):
            s = _dot_nt(qb[hd], kb_ref[rows, hd * LANES:(hd + 1) * LANES])
            new.append(_softmax_step(mask(s), vb_ones[hd], states[2 + hd]))
        return tuple(new)

    tq2 = 2 * tq
    for t in range(seq // tq2):
        qrows = slice(t * tq2, (t + 1) * tq2)
        qa = _moba_queries(qa_ref[qrows, :], kbar, t, n_blocks)
        qb = [qb_ref[qrows, hd * LANES:(hd + 1) * LANES] for hd in range(2)]
        states = (_softmax_init(tq2),) * 4
        for j in range(2 * t + 1):
            states = tile(qa, qb, j, states, t * tq2, j == 2 * t)
        low = tuple((m[tq:], acc[tq:]) for m, acc in states)
        low = tile([q[tq:] for q in qa], [q[tq:] for q in qb], 2 * t + 1, low, t * tq2 + tq, True)
        states = tuple((jnp.concatenate([m[:tq], ml], axis=0), jnp.concatenate([acc[:tq], accl], axis=0))
                       for (m, acc), (ml, accl) in zip(states, low))
        oa_ref[qrows, :] = _merge_pair(states[:2]).astype(oa_ref.dtype)
        ob_ref[qrows, :] = _merge_pair(states[2:]).astype(ob_ref.dtype)


def _attention(qa, ka, va, e_mat, qb, kb, vb):
    b, s, _ = qa.shape
    seq_spec = lambda w: pl.BlockSpec((None, s, w), lambda i, p: (i, 0, p))
    return pl.pallas_call(
        _attn_body,
        out_shape=[jax.ShapeDtypeStruct((b, s, A_WIDTH), BF16)] * 2,
        grid=(b, N_PAIRS),
        in_specs=[seq_spec(LANES), seq_spec(LANES), seq_spec(LANES), _resident(e_mat.shape),
                  seq_spec(2 * LANES), seq_spec(2 * LANES), seq_spec(LANES)],
        out_specs=[seq_spec(LANES)] * 2,
        compiler_params=_params(2),
        name="attention",
    )(qa, ka, va, e_mat, qb, kb, vb)


def _mixer_out_body(layer_ref, x_ref, oa_ref, ob_ref, ga_ref, gb_ref, kv_ref, wa_ref, wb_ref, wo_ref,
                    gx_ref, wxq_ref, wxo_ref, o_ref):
    merged = (ga_ref[...].astype(F32) * _dot(oa_ref[...], wa_ref[...])
              + gb_ref[...].astype(F32) * _dot(ob_ref[...], wb_ref[...]))
    x1 = x_ref[...] + _dot(merged.astype(BF16), wo_ref[...])

    q = _dot(_rms(x1, gx_ref[...]).astype(BF16), wxq_ref[...]).astype(BF16)
    scale = X_HEAD_DIM ** -0.5
    heads = []
    for hd in range(X_HEADS):
        lanes = slice(hd * X_HEAD_DIM, (hd + 1) * X_HEAD_DIM)
        s = _dot_nt(q[:, lanes], kv_ref[:, lanes])
        m = jnp.max(s, axis=-1, keepdims=True)
        p = jnp.exp((s - m) * scale)
        l = jnp.sum(p, axis=-1, keepdims=True)
        v = kv_ref[:, X_WIDTH + hd * X_HEAD_DIM:X_WIDTH + (hd + 1) * X_HEAD_DIM]
        heads.append((_dot(p.astype(BF16), v) * (1.0 / l)).astype(BF16))
    o_ref[...] = x1 + _dot(jnp.concatenate(heads, axis=1), wxo_ref[...])


def _mixer_out(layer, x, seq, oa, ob, ga, gb, mem_kv, w_a, w_b, w_out, norm_xattn, w_xq, w_xo):
    t, d = x.shape
    tm = min(OUT_TILE, seq)
    tiles_per_seq = seq // tm
    m = mem_kv.shape[2]
    tok = lambda w: pl.BlockSpec((tm, w), lambda i, l: (i, 0))
    weights = (w_a, w_b, w_out, norm_xattn, w_xq, w_xo)
    return pl.pallas_call(
        _mixer_out_body,
        out_shape=jax.ShapeDtypeStruct((t, d), F32),
        grid_spec=_layer_grid(
            (t // tm,),
            [tok(d), tok(A_WIDTH), tok(A_WIDTH), tok(d), tok(d),
             pl.BlockSpec((None, None, m, 2 * X_WIDTH), lambda i, l: (l[0], i // tiles_per_seq, 0, 0))]
            + [_layer_resident(w.shape) for w in weights],
            tok(d)),
        compiler_params=_params(1),
        name="mixer_out",
    )(layer, x, oa, ob, ga, gb, mem_kv, *weights)


FF_CHUNK = 1024


def _ffn_body(layer_ref, x_ref, g_ref, wgu_ref, wd_ref, gf_ref, o_ref, *, final):
    x = x_ref[...]
    h = _rms(x, g_ref[...]).astype(BF16)
    d_ff = wd_ref.shape[0]
    y = x
    for c0 in range(0, d_ff, FF_CHUNK):
        c1 = min(c0 + FF_CHUNK, d_ff)
        gt = _dot(h, wgu_ref[:, c0:c1])
        up = _dot(h, wgu_ref[:, d_ff + c0:d_ff + c1])
        a = (gt * jax.nn.sigmoid(gt) * up).astype(BF16)
        y = y + _dot(a, wd_ref[c0:c1, :])
    o_ref[...] = _rms(y, gf_ref[...]) if final else y


def _ffn(layer, x, norm_ffn, w_gate_up, w_down, norm_final, final):
    t, d = x.shape
    tm = min(OUT_TILE, t)
    tok = pl.BlockSpec((tm, d), lambda i, l: (i, 0))
    weights = (norm_ffn, w_gate_up, w_down)
    return pl.pallas_call(
        functools.partial(_ffn_body, final=final),
        out_shape=jax.ShapeDtypeStruct((t, d), F32),
        grid_spec=_layer_grid((t // tm,), [tok] + [_layer_resident(w.shape) for w in weights]
                              + [_resident(norm_final.shape)], tok),
        compiler_params=_params(1),
        name="ffn_final" if final else "ffn",
    )(layer, x, *weights, norm_final)


PREP_ROWS = 256


def _prep_w_in_body(w_ref, o_ref):
    rows = w_ref.shape[0]
    n_gate_groups = (w_ref.shape[1] - _C_LAT - Q_LORA - KV_LORA - MLA_ROPE) // LANES
    k0 = _C_LAT + Q_LORA + KV_LORA
    half = MLA_ROPE // 2
    lane = lax.broadcasted_iota(jnp.int32, (rows, LANES), 1)
    o_ref[:, :k0] = w_ref[:, :k0].astype(BF16)
    g = w_ref[:, k0:k0 + LANES]
    kpe = jnp.where(lane < half, g, jnp.where((lane >= 64) & (lane < 64 + half), pltpu.roll(g, 64 - half, axis=1), 0.0))
    o_ref[:, k0:k0 + LANES] = kpe.astype(BF16)
    for j in range(n_gate_groups):
        a = w_ref[:, k0 + j * LANES:k0 + (j + 1) * LANES]
        if j + 1 < n_gate_groups:
            b = w_ref[:, k0 + (j + 1) * LANES:k0 + (j + 2) * LANES]
        else:
            tail = w_ref[:, k0 + (j + 1) * LANES:]
            b = jnp.concatenate([tail, jnp.zeros((rows, LANES - MLA_ROPE), tail.dtype)], axis=1)
        shifted = pltpu.roll(jnp.where(lane >= MLA_ROPE, a, b), LANES - MLA_ROPE, axis=1)
        o_ref[:, _C_G + j * LANES:_C_G + (j + 1) * LANES] = shifted.astype(BF16)


def _prep_w_in(w_in):
    n_layers, d, in_w = w_in.shape
    out_w = in_w - MLA_ROPE + LANES
    tr = min(PREP_ROWS, d)
    return pl.pallas_call(
        _prep_w_in_body,
        out_shape=jax.ShapeDtypeStruct((n_layers, d, out_w), BF16),
        grid=(n_layers, d // tr),
        in_specs=[pl.BlockSpec((None, tr, in_w), lambda l, i: (l, i, 0))],
        out_specs=pl.BlockSpec((None, tr, out_w), lambda l, i: (l, i, 0)),
        compiler_params=_params(2),
        name="prep_w_in",
    )(w_in)


def _mla_head_cols(nope, rope):
    half = MLA_ROPE // 2
    z = lambda n: jnp.zeros(nope.shape[:-1] + (n,), nope.dtype)
    t1, t2 = (z(half), z(half)) if rope is None else (rope[..., :half], rope[..., half:])
    return jnp.concatenate([t1, nope[..., :HEAD_DIM - half], t2, nope[..., HEAD_DIM - half:], z(LANES - MLA_QK)], axis=-1)


def _prep_weights(w_in, w_uq, w_ukv):
    n_layers = w_in.shape[0]
    w_in_r = _prep_w_in(w_in)
    uq = w_uq.reshape(n_layers, Q_LORA, N_HEADS, MLA_QK)
    w_uq_r = _mla_head_cols(uq[..., :HEAD_DIM], uq[..., HEAD_DIM:]).reshape(n_layers, Q_LORA, _KB_W).astype(BF16)
    ukv = w_ukv.reshape(n_layers, KV_LORA, N_HEADS, 2 * HEAD_DIM)
    w_k = _mla_head_cols(ukv[..., :HEAD_DIM], None).reshape(n_layers, KV_LORA, _KB_W)
    w_v = ukv[..., HEAD_DIM:].reshape(n_layers, KV_LORA, A_WIDTH)
    w_ukv_r = jnp.concatenate([w_k, w_v], axis=-1).astype(BF16)
    return w_in_r, w_uq_r, w_ukv_r


def kernel(x, mem, positions, norm_mix, w_in, q_lat_norm, w_uq, kv_lat_norm, w_ukv, w_branch_a, w_branch_b, w_out, norm_xattn, norm_mem, w_xq, w_xkv, w_xo, norm_ffn, w_gate_up, w_down, norm_final):
    b, s, d = x.shape
    n_layers = w_in.shape[0]
    t = b * s
    assert s % (2 * ATT_TILE) == 0 and ATT_TILE % MOBA_BLOCK == 0 and s // MOBA_BLOCK <= MAX_BLOCKS

    w_in_r, w_uq_r, w_ukv_r = _prep_weights(w_in, w_uq, w_ukv)
    bf = lambda w: w.astype(BF16)
    row = lambda g: g.reshape(g.shape[0], 1, g.shape[1])
    tabs = _rope_tables(positions)
    mem_kv = _mem_kv(mem, row(norm_mem), bf(w_xkv))
    blk_of_row = np.arange(s) // MOBA_BLOCK
    e_mat = jnp.asarray(np.where(blk_of_row[:, None] == np.arange(LANES)[None, :], NEG_BIG, 0.0), BF16)
    norm_final = norm_final.reshape(1, d)

    norm_mix, q_lat_norm, kv_lat_norm = row(norm_mix), row(q_lat_norm), row(kv_lat_norm)
    norm_xattn, norm_ffn = row(norm_xattn), row(norm_ffn)
    w_a, w_b, w_o, w_q, w_xo_b = bf(w_branch_a), bf(w_branch_b), bf(w_out), bf(w_xq), bf(w_xo)
    w_gu, w_dn = bf(w_gate_up), bf(w_down)
    seq3 = lambda a: a.reshape(b, s, a.shape[-1])

    def layer(li, xt, final):
        li = jnp.full((1,), li, jnp.int32)
        qa, ka, va, qb, kb, vb, ga, gb = _mixer_in(li, xt, norm_mix, w_in_r, q_lat_norm, w_uq_r,
                                                   kv_lat_norm, w_ukv_r, tabs)
        oa, ob = _attention(seq3(qa), seq3(ka), seq3(va), e_mat, seq3(qb), seq3(kb), seq3(vb))
        oa, ob = oa.reshape(t, A_WIDTH), ob.reshape(t, A_WIDTH)
        xt = _mixer_out(li, xt, s, oa, ob, ga, gb, mem_kv, w_a, w_b, w_o, norm_xattn, w_q, w_xo_b)
        return _ffn(li, xt, norm_ffn, w_gu, w_dn, norm_final, final)

    xt = x.reshape(t, d)
    for li in range(n_layers):
        xt = layer(li, xt, li == n_layers - 1)
    return xt.reshape(b, s, d)
```

```python
import functools
import math

import numpy as np
import jax
import jax.numpy as jnp
from jax import lax
from jax.experimental import pallas as pl
from jax.experimental.pallas import tpu as pltpu

F32 = jnp.float32
BF16 = jnp.bfloat16

EPS = 1e-6
ROPE_THETA = 10000.0
LANES = 128
HALF_LANES = LANES // 2
HEAD_DIM = 64
N_HEADS = 8
N_PAIRS = N_HEADS // 2
A_WIDTH = N_HEADS * HEAD_DIM
MLA_ROPE = 32
MLA_QK = HEAD_DIM + MLA_ROPE
Q_LORA = 384
KV_LORA = 256
MOBA_BLOCK = 256
MOBA_TOPK = 3
MAX_BLOCKS = 16
X_HEADS = 4
X_HEAD_DIM = 128
X_WIDTH = X_HEADS * X_HEAD_DIM
NEG_BIG = -1e30
LOG2E = math.log2(math.e)
VMEM_LIMIT = 56 * 1024 * 1024
TOK_TILE = 1024
ATT_TILE = 512
TAB_W = 5 * LANES


def _resident(shape):
    return pl.BlockSpec(shape, lambda *_: (0,) * len(shape), pipeline_mode=pl.Buffered(1))


def _layer_resident(shape):
    return pl.BlockSpec((None,) + tuple(shape[1:]), lambda *a: (a[-1][0],) + (0,) * (len(shape) - 1),
                        pipeline_mode=pl.Buffered(1))


def _layer_grid(grid, in_specs, out_specs):
    return pltpu.PrefetchScalarGridSpec(num_scalar_prefetch=1, grid=grid, in_specs=in_specs, out_specs=out_specs)


def _params(n_axes):
    return pltpu.CompilerParams(dimension_semantics=("arbitrary",) * n_axes,
                                vmem_limit_bytes=VMEM_LIMIT)


def _rms(x, g):
    return x * lax.rsqrt(jnp.mean(x * x, axis=-1, keepdims=True) + EPS) * g


def _dot(a, b):
    return jnp.dot(a, b, preferred_element_type=F32)


def _dot_nt(a, b):
    return lax.dot_general(a, b, (((1,), (1,)), ((), ())), preferred_element_type=F32)


def _rope_table_body(pos_ref, tab_ref):
    pos = pos_ref[...]
    lane = lax.broadcasted_iota(jnp.int32, pos.shape, 1)
    sub = lane % HALF_LANES
    inv = jnp.where(sub < 32,
                    jnp.exp((-math.log(ROPE_THETA) * (2.0 / HEAD_DIM)) * sub.astype(F32)),
                    jnp.exp((-math.log(ROPE_THETA) * (2.0 / MLA_ROPE)) * (sub - 32).astype(F32)))
    ang = pos * inv
    cos2, sin2 = jnp.cos(ang), jnp.sin(ang)

    def spread(v):
        r32, r96 = pltpu.roll(v, 32, axis=1), pltpu.roll(v, 96, axis=1)
        tiled = jnp.where(lane < 32, v, jnp.where(lane < 64, r32, jnp.where(
            lane < 96, pltpu.roll(v, 64, axis=1), r96)))
        return tiled, jnp.where(lane < 64, r96, r32)

    first_half = sub < 32
    is_rope = sub < MLA_ROPE // 2
    for half in range(2):
        cos, sin = (cos2, sin2) if half == 0 else (pltpu.roll(cos2, HALF_LANES, axis=1),
                                                               pltpu.roll(sin2, HALF_LANES, axis=1))
        cos_a, cos_b = spread(cos)
        sin_a, sin_b = spread(sin)
        tab_ref[half, :, 0:LANES] = cos_a
        tab_ref[half, :, LANES:2 * LANES] = jnp.where(first_half, -sin_a, 0.0)
        tab_ref[half, :, 2 * LANES:3 * LANES] = jnp.where(first_half, 0.0, sin_a)
        tab_ref[half, :, 3 * LANES:4 * LANES] = jnp.where(is_rope, cos_b, 1.0)
        tab_ref[half, :, 4 * LANES:5 * LANES] = jnp.where(is_rope, jnp.where(lane < 64, -sin_b, sin_b), 0.0)


def _rope_tables(positions):
    t = positions.size
    half = t // 2
    pos = positions.reshape(2, half, 1).astype(F32)
    pos2 = jnp.concatenate([jnp.broadcast_to(pos[0], (half, HALF_LANES)),
                            jnp.broadcast_to(pos[1], (half, HALF_LANES))], axis=1)
    tm = min(1024, half)
    tabs = pl.pallas_call(
        _rope_table_body,
        out_shape=jax.ShapeDtypeStruct((2, half, TAB_W), F32),
        grid=(half // tm,),
        in_specs=[pl.BlockSpec((tm, LANES), lambda i: (i, 0))],
        out_specs=pl.BlockSpec((2, tm, TAB_W), lambda i: (0, i, 0)),
        compiler_params=_params(1),
        name="rope_tables",
    )(pos2)
    return tabs.reshape(t, TAB_W)


MEM_BATCH = 4


def _mem_kv_body(mem_ref, g_ref, w_ref, o_ref):
    nb, m, d = mem_ref.shape
    h = _rms(mem_ref[...].reshape(nb * m, d), g_ref[...]).astype(BF16)
    o_ref[...] = _dot(h, w_ref[...]).astype(BF16).reshape(o_ref.shape)


def _mem_kv(mem, norm_mem, w_xkv):
    b, m, d = mem.shape
    n_layers = w_xkv.shape[0]
    nb = math.gcd(b, MEM_BATCH)
    return pl.pallas_call(
        _mem_kv_body,
        out_shape=jax.ShapeDtypeStruct((n_layers, b, m, 2 * X_WIDTH), BF16),
        grid=(n_layers, b // nb),
        in_specs=[pl.BlockSpec((nb, m, d), lambda l, i: (i, 0, 0)),
                  pl.BlockSpec((None, 1, d), lambda l, i: (l, 0, 0)),
                  pl.BlockSpec((None, d, 2 * X_WIDTH), lambda l, i: (l, 0, 0))],
        out_specs=pl.BlockSpec((None, nb, m, 2 * X_WIDTH), lambda l, i: (l, i, 0, 0)),
        compiler_params=_params(2),
        name="mem_kv",
    )(mem, norm_mem, w_xkv)


_C_QK = 0
_C_VA = 2 * A_WIDTH
_C_LAT = 3 * A_WIDTH
_C_G = _C_LAT + Q_LORA + KV_LORA + LANES
_KB_W = N_HEADS * LANES


def _mixer_in_body(layer_ref, x_ref, g_ref, win_ref, qg_ref, wuq_ref, kvg_ref, wukv_ref, tab_ref,
                   qa_ref, ka_ref, va_ref, qb_ref, kb_ref, vb_ref):
    h = _rms(x_ref[...], g_ref[...]).astype(BF16)
    cos_a, sa_lo, sa_hi = tab_ref[:, 0:LANES], tab_ref[:, LANES:2 * LANES], tab_ref[:, 2 * LANES:3 * LANES]
    cos_b, sin_b = tab_ref[:, 3 * LANES:4 * LANES], tab_ref[:, 4 * LANES:5 * LANES]

    def rope_a(y):
        return (y * cos_a + pltpu.roll(y, LANES - 32, axis=1) * sa_lo
                + pltpu.roll(y, 32, axis=1) * sa_hi)

    def rope_b(y):
        return y * cos_b + pltpu.roll(y, 64, axis=1) * sin_b

    scale_a = HEAD_DIM ** -0.5 * LOG2E
    scale_b = MLA_QK ** -0.5 * LOG2E
    y = _dot(h, win_ref[:, _C_QK:_C_QK + 2 * A_WIDTH])
    for p in range(N_PAIRS):
        sl = slice(p * LANES, (p + 1) * LANES)
        qa_ref[:, sl] = (rope_a(y[:, sl]) * scale_a).astype(BF16)
        ka_ref[:, sl] = rope_a(y[:, A_WIDTH + p * LANES:A_WIDTH + (p + 1) * LANES]).astype(BF16)

    va_ref[...] = _dot(h, win_ref[:, _C_VA:_C_VA + A_WIDTH]).astype(BF16)

    lat = _dot(h, win_ref[:, _C_LAT:_C_G])
    q_lat = _rms(lat[:, 0:Q_LORA], qg_ref[...]).astype(BF16)
    kv_lat = _rms(lat[:, Q_LORA:Q_LORA + KV_LORA], kvg_ref[...]).astype(BF16)
    k_pe = rope_b(lat[:, Q_LORA + KV_LORA:])
    yq = _dot(q_lat, wuq_ref[...])
    ykv = _dot(kv_lat, wukv_ref[...])
    for hd in range(N_HEADS):
        sl = slice(hd * LANES, (hd + 1) * LANES)
        qb_ref[:, sl] = (rope_b(yq[:, sl]) * scale_b).astype(BF16)
        kb_ref[:, sl] = (ykv[:, sl] + k_pe).astype(BF16)
    vb_ref[...] = ykv[:, _KB_W:].astype(BF16)


def _mixer_in(layer, x, norm_mix, w_in, q_lat_norm, w_uq, kv_lat_norm, w_ukv, tabs):
    t, d = x.shape
    tm = min(TOK_TILE, t)
    tok = lambda w: pl.BlockSpec((tm, w), lambda i, l: (i, 0))
    widths = (A_WIDTH, A_WIDTH, A_WIDTH, _KB_W, _KB_W, A_WIDTH)
    weights = (norm_mix, w_in, q_lat_norm, w_uq, kv_lat_norm, w_ukv)
    return pl.pallas_call(
        _mixer_in_body,
        out_shape=[jax.ShapeDtypeStruct((t, w), BF16) for w in widths],
        grid_spec=_layer_grid((t // tm,), [tok(d)] + [_layer_resident(w.shape) for w in weights] + [tok(TAB_W)],
                              [tok(w) for w in widths]),
        compiler_params=_params(1),
        name="mixer_in",
    )(layer, x, *weights, tabs)


def _softmax_init(tq):
    return (jnp.full((tq, 1), NEG_BIG, F32), jnp.zeros((tq, LANES), F32))


def _softmax_step(s, v_ones, state):
    m, acc = state
    m_new = jnp.maximum(m, jnp.max(s, axis=-1, keepdims=True))
    p = jnp.exp2(s - m_new)
    acc_new = jnp.exp2(m - m_new) * acc + _dot(p.astype(BF16), v_ones)
    return m_new, acc_new


def _values_and_ones(v):
    lane = lax.broadcasted_iota(jnp.int32, v.shape, 1)
    one = jnp.ones_like(v)
    return jnp.where(lane < HEAD_DIM, v, one), jnp.where(lane < HEAD_DIM, one, v)


def _causal(s, q0, k0):
    row = lax.broadcasted_iota(jnp.int32, s.shape, 0) + q0
    col = lax.broadcasted_iota(jnp.int32, s.shape, 1) + k0
    return jnp.where(col <= row, s, NEG_BIG)


def _merge_pair(states):
    (_, acc0), (_, acc1) = states
    lane = lax.broadcasted_iota(jnp.int32, acc0.shape, 1)
    num = jnp.where(lane < HEAD_DIM, acc0, acc1)
    den = pltpu.roll(jnp.where(lane < HEAD_DIM, acc1, acc0), HEAD_DIM, axis=1)
    return num / den


def _moba_queries(q, kbar, qi, n_blocks):
    tq = q.shape[0]
    lane = lax.broadcasted_iota(jnp.int32, q.shape, 1)
    row = lax.broadcasted_iota(jnp.int32, (MAX_BLOCKS, tq), 0)
    col = lax.broadcasted_iota(jnp.int32, (MAX_BLOCKS, tq), 1)
    cur = qi * (tq // MOBA_BLOCK) + col // MOBA_BLOCK
    q_aug = []
    for hd in range(2):
        qh = jnp.where((lane < HEAD_DIM) == (hd == 0), q, jnp.zeros_like(q))
        gate = jnp.where(row < cur, _dot_nt(kbar, qh), -jnp.inf)
        rank = jnp.zeros(gate.shape, jnp.int32)
        for n in range(n_blocks):
            gn = gate[n:n + 1, :]
            beats = (gn > gate) | ((gn == gate) & (row > n))
            rank = rank + beats.astype(jnp.int32)
        keep = ((row < cur) & (rank < MOBA_TOPK)) | (row == cur)
        dropped = jnp.where(keep, 0.0, 1.0)
        dropped = jnp.concatenate([dropped, jnp.zeros((LANES - MAX_BLOCKS, tq), F32)], axis=0)
        q_aug.append(jnp.concatenate([qh, dropped.T.astype(BF16)], axis=1))
    return q_aug


def _attn_body(qa_ref, ka_ref, va_ref, e_ref, qb_ref, kb_ref, vb_ref, oa_ref, ob_ref):
    tq = ATT_TILE
    seq = ka_ref.shape[0]
    n_blocks = seq // MOBA_BLOCK
    kbar = [jnp.mean(ka_ref[n * MOBA_BLOCK:(n + 1) * MOBA_BLOCK, :].astype(F32), axis=0, keepdims=True)
            for n in range(n_blocks)]
    kbar = jnp.concatenate(kbar + [jnp.zeros((MAX_BLOCKS - n_blocks, LANES), F32)], axis=0).astype(BF16)

    def tile(qa, qb, j, states, q0, causal):
        rows = slice(j * tq, (j + 1) * tq)
        ka_aug = jnp.concatenate([ka_ref[rows, :], e_ref[rows, :]], axis=1)
        va_ones = _values_and_ones(va_ref[rows, :])
        vb_ones = _values_and_ones(vb_ref[rows, :])
        mask = functools.partial(_causal, q0=q0, k0=j * tq) if causal else (lambda s: s)
        new = []
        for hd in range(2):
            new.append(_softmax_step(mask(_dot_nt(qa[hd], ka_aug)), va_ones[hd], states[hd]))
        for hd in range(2):
            s = _dot_nt(qb[hd], kb_ref[rows, hd * LANES:(hd + 1) * LANES])
            new.append(_softmax_step(mask(s), vb_ones[hd], states[2 + hd]))
        return tuple(new)

    tq2 = 2 * tq
    for t in range(seq // tq2):
        qrows = slice(t * tq2, (t + 1) * tq2)
        qa = _moba_queries(qa_ref[qrows, :], kbar, t, n_blocks)
        qb = [qb_ref[qrows, hd * LANES:(hd + 1) * LANES] for hd in range(2)]
        states = (_softmax_init(tq2),) * 4
        for j in range(2 * t + 1):
            states = tile(qa, qb, j, states, t * tq2, j == 2 * t)
        low = tuple((m[tq:], acc[tq:]) for m, acc in states)
        low = tile([q[tq:] for q in qa], [q[tq:] for q in qb], 2 * t + 1, low, t * tq2 + tq, True)
        states = tuple((jnp.concatenate([m[:tq], ml], axis=0), jnp.concatenate([acc[:tq], accl], axis=0))
                       for (m, acc), (ml, accl) in zip(states, low))
        oa_ref[qrows, :] = _merge_pair(states[:2]).astype(oa_ref.dtype)
        ob_ref[qrows, :] = _merge_pair(states[2:]).astype(ob_ref.dtype)


def _attention(qa, ka, va, e_mat, qb, kb, vb):
    b, s, _ = qa.shape
    seq_spec = lambda w: pl.BlockSpec((None, s, w), lambda i, p: (i, 0, p))
    return pl.pallas_call(
        _attn_body,
        out_shape=[jax.ShapeDtypeStruct((b, s, A_WIDTH), BF16)] * 2,
        grid=(b, N_PAIRS),
        in_specs=[seq_spec(LANES), seq_spec(LANES), seq_spec(LANES), _resident(e_mat.shape),
                  seq_spec(2 * LANES), seq_spec(2 * LANES), seq_spec(LANES)],
        out_specs=[seq_spec(LANES)] * 2,
        compiler_params=_params(2),
        name="attention",
    )(qa, ka, va, e_mat, qb, kb, vb)


def _mixer_out_body(layer_ref, x_ref, oa_ref, ob_ref, kv_ref, gm_ref, wg_ref, wa_ref, wb_ref, wo_ref,
                    gx_ref, wxq_ref, wxo_ref, o_ref):
    x = x_ref[...]
    d = x.shape[-1]
    h = _rms(x, gm_ref[...]).astype(BF16)
    merged = (jax.nn.sigmoid(_dot(h, wg_ref[:, :d])) * _dot(oa_ref[...], wa_ref[...])
              + jax.nn.sigmoid(_dot(h, wg_ref[:, d:])) * _dot(ob_ref[...], wb_ref[...]))
    x1 = x + _dot(merged.astype(BF16), wo_ref[...])

    q = _dot(_rms(x1, gx_ref[...]).astype(BF16), wxq_ref[...]).astype(BF16)
    scale = X_HEAD_DIM ** -0.5
    heads = []
    for hd in range(X_HEADS):
        lanes = slice(hd * X_HEAD_DIM, (hd + 1) * X_HEAD_DIM)
        s = _dot_nt(q[:, lanes], kv_ref[:, lanes])
        m = jnp.max(s, axis=-1, keepdims=True)
        p = jnp.exp((s - m) * scale)
        l = jnp.sum(p, axis=-1, keepdims=True)
        v = kv_ref[:, X_WIDTH + hd * X_HEAD_DIM:X_WIDTH + (hd + 1) * X_HEAD_DIM]
        heads.append((_dot(p.astype(BF16), v) * (1.0 / l)).astype(BF16))
    o_ref[...] = x1 + _dot(jnp.concatenate(heads, axis=1), wxo_ref[...])


def _mixer_out(layer, x, seq, oa, ob, mem_kv, norm_mix, w_gates, w_a, w_b, w_out, norm_xattn, w_xq, w_xo):
    t, d = x.shape
    tm = min(TOK_TILE, seq)
    tiles_per_seq = seq // tm
    m = mem_kv.shape[2]
    tok = lambda w: pl.BlockSpec((tm, w), lambda i, l: (i, 0))
    weights = (norm_mix, w_gates, w_a, w_b, w_out, norm_xattn, w_xq, w_xo)
    return pl.pallas_call(
        _mixer_out_body,
        out_shape=jax.ShapeDtypeStruct((t, d), F32),
        grid_spec=_layer_grid(
            (t // tm,),
            [tok(d), tok(A_WIDTH), tok(A_WIDTH),
             pl.BlockSpec((None, None, m, 2 * X_WIDTH), lambda i, l: (l[0], i // tiles_per_seq, 0, 0))]
            + [_layer_resident(w.shape) for w in weights],
            tok(d)),
        compiler_params=_params(1),
        name="mixer_out",
    )(layer, x, oa, ob, mem_kv, *weights)


FF_CHUNK = 1024


def _ffn_body(layer_ref, x_ref, g_ref, wgu_ref, wd_ref, gf_ref, o_ref, *, final):
    x = x_ref[...]
    h = _rms(x, g_ref[...]).astype(BF16)
    d_ff = wd_ref.shape[0]
    y = x
    for c0 in range(0, d_ff, FF_CHUNK):
        c1 = min(c0 + FF_CHUNK, d_ff)
        gt = _dot(h, wgu_ref[:, c0:c1])
        up = _dot(h, wgu_ref[:, d_ff + c0:d_ff + c1])
        a = (gt * jax.nn.sigmoid(gt) * up).astype(BF16)
        y = y + _dot(a, wd_ref[c0:c1, :])
    o_ref[...] = _rms(y, gf_ref[...]) if final else y


def _ffn(layer, x, norm_ffn, w_gate_up, w_down, norm_final, final):
    t, d = x.shape
    tm = min(TOK_TILE, t)
    tok = pl.BlockSpec((tm, d), lambda i, l: (i, 0))
    weights = (norm_ffn, w_gate_up, w_down)
    return pl.pallas_call(
        functools.partial(_ffn_body, final=final),
        out_shape=jax.ShapeDtypeStruct((t, d), F32),
        grid_spec=_layer_grid((t // tm,), [tok] + [_layer_resident(w.shape) for w in weights]
                              + [_resident(norm_final.shape)], tok),
        compiler_params=_params(1),
        name="ffn_final" if final else "ffn",
    )(layer, x, *weights, norm_final)


PREP_ROWS = 256


def _prep_w_in_body(w_ref, o_ref, og_ref):
    rows = w_ref.shape[0]
    n_gate_groups = (w_ref.shape[1] - _C_LAT - Q_LORA - KV_LORA - MLA_ROPE) // LANES
    k0 = _C_LAT + Q_LORA + KV_LORA
    half = MLA_ROPE // 2
    lane = lax.broadcasted_iota(jnp.int32, (rows, LANES), 1)
    o_ref[:, :k0] = w_ref[:, :k0].astype(BF16)
    g = w_ref[:, k0:k0 + LANES]
    kpe = jnp.where(lane < half, g, jnp.where((lane >= 64) & (lane < 64 + half), pltpu.roll(g, 64 - half, axis=1), 0.0))
    o_ref[:, k0:k0 + LANES] = kpe.astype(BF16)
    for j in range(n_gate_groups):
        a = w_ref[:, k0 + j * LANES:k0 + (j + 1) * LANES]
        if j + 1 < n_gate_groups:
            b = w_ref[:, k0 + (j + 1) * LANES:k0 + (j + 2) * LANES]
        else:
            tail = w_ref[:, k0 + (j + 1) * LANES:]
            b = jnp.concatenate([tail, jnp.zeros((rows, LANES - MLA_ROPE), tail.dtype)], axis=1)
        shifted = pltpu.roll(jnp.where(lane >= MLA_ROPE, a, b), LANES - MLA_ROPE, axis=1)
        og_ref[:, j * LANES:(j + 1) * LANES] = shifted.astype(BF16)


def _prep_w_in(w_in):
    n_layers, d, in_w = w_in.shape
    tr = min(PREP_ROWS, d)
    blk = lambda w: pl.BlockSpec((None, tr, w), lambda l, i: (l, i, 0))
    return pl.pallas_call(
        _prep_w_in_body,
        out_shape=[jax.ShapeDtypeStruct((n_layers, d, _C_G), BF16),
                   jax.ShapeDtypeStruct((n_layers, d, in_w - _C_G + LANES - MLA_ROPE), BF16)],
        grid=(n_layers, d // tr),
        in_specs=[blk(in_w)],
        out_specs=[blk(_C_G), blk(in_w - _C_G + LANES - MLA_ROPE)],
        compiler_params=_params(2),
        name="prep_w_in",
    )(w_in)


def _mla_head_cols(nope, rope):
    half = MLA_ROPE // 2
    z = lambda n: jnp.zeros(nope.shape[:-1] + (n,), nope.dtype)
    t1, t2 = (z(half), z(half)) if rope is None else (rope[..., :half], rope[..., half:])
    return jnp.concatenate([t1, nope[..., :HEAD_DIM - half], t2, nope[..., HEAD_DIM - half:], z(LANES - MLA_QK)], axis=-1)


def _prep_weights(w_in, w_uq, w_ukv):
    n_layers = w_in.shape[0]
    w_in_r, w_gates = _prep_w_in(w_in)
    uq = w_uq.reshape(n_layers, Q_LORA, N_HEADS, MLA_QK)
    w_uq_r = _mla_head_cols(uq[..., :HEAD_DIM], uq[..., HEAD_DIM:]).reshape(n_layers, Q_LORA, _KB_W).astype(BF16)
    ukv = w_ukv.reshape(n_layers, KV_LORA, N_HEADS, 2 * HEAD_DIM)
    w_k = _mla_head_cols(ukv[..., :HEAD_DIM], None).reshape(n_layers, KV_LORA, _KB_W)
    w_v = ukv[..., HEAD_DIM:].reshape(n_layers, KV_LORA, A_WIDTH)
    w_ukv_r = jnp.concatenate([w_k, w_v], axis=-1).astype(BF16)
    return w_in_r, w_gates, w_uq_r, w_ukv_r


def kernel(x, mem, positions, norm_mix, w_in, q_lat_norm, w_uq, kv_lat_norm, w_ukv, w_branch_a, w_branch_b, w_out, norm_xattn, norm_mem, w_xq, w_xkv, w_xo, norm_ffn, w_gate_up, w_down, norm_final):
    b, s, d = x.shape
    n_layers = w_in.shape[0]
    t = b * s
    assert s % (2 * ATT_TILE) == 0 and ATT_TILE % MOBA_BLOCK == 0 and s // MOBA_BLOCK <= MAX_BLOCKS

    w_in_r, w_gates, w_uq_r, w_ukv_r = _prep_weights(w_in, w_uq, w_ukv)
    bf = lambda w: w.astype(BF16)
    row = lambda g: g.reshape(g.shape[0], 1, g.shape[1])
    tabs = _rope_tables(positions)
    mem_kv = _mem_kv(mem, row(norm_mem), bf(w_xkv))
    blk_of_row = np.arange(s) // MOBA_BLOCK
    e_mat = jnp.asarray(np.where(blk_of_row[:, None] == np.arange(LANES)[None, :], NEG_BIG, 0.0), BF16)
    norm_final = norm_final.reshape(1, d)

    norm_mix, q_lat_norm, kv_lat_norm = row(norm_mix), row(q_lat_norm), row(kv_lat_norm)
    norm_xattn, norm_ffn = row(norm_xattn), row(norm_ffn)
    w_a, w_b, w_o, w_q, w_xo_b = bf(w_branch_a), bf(w_branch_b), bf(w_out), bf(w_xq), bf(w_xo)
    w_gu, w_dn = bf(w_gate_up), bf(w_down)
    seq3 = lambda a: a.reshape(b, s, a.shape[-1])

    def layer(li, xt, final):
        li = jnp.full((1,), li, jnp.int32)
        qa, ka, va, qb, kb, vb = _mixer_in(li, xt, norm_mix, w_in_r, q_lat_norm, w_uq_r,
                                           kv_lat_norm, w_ukv_r, tabs)
        oa, ob = _attention(seq3(qa), seq3(ka), seq3(va), e_mat, seq3(qb), seq3(kb), seq3(vb))
        oa, ob = oa.reshape(t, A_WIDTH), ob.reshape(t, A_WIDTH)
        xt = _mixer_out(li, xt, s, oa, ob, mem_kv, norm_mix, w_gates, w_a, w_b, w_o, norm_xattn, w_q, w_xo_b)
        return _ffn(li, xt, norm_ffn, w_gu, w_dn, norm_final, final)

    xt = x.reshape(t, d)
    for li in range(n_layers):
        xt = layer(li, xt, li == n_layers - 1)
    return xt.reshape(b, s, d)
```

```python
import functools
import math

import numpy as np
import jax
import jax.numpy as jnp
from jax import lax
from jax.experimental import pallas as pl
from jax.experimental.pallas import tpu as pltpu

F32 = jnp.float32
BF16 = jnp.bfloat16

EPS = 1e-6
ROPE_THETA = 10000.0
LANES = 128
HALF_LANES = LANES // 2
HEAD_DIM = 64
N_HEADS = 8
N_PAIRS = N_HEADS // 2
A_WIDTH = N_HEADS * HEAD_DIM
MLA_ROPE = 32
MLA_QK = HEAD_DIM + MLA_ROPE
Q_LORA = 384
KV_LORA = 256
MOBA_BLOCK = 256
MOBA_TOPK = 3
MAX_BLOCKS = 16
X_HEADS = 4
X_HEAD_DIM = 128
X_WIDTH = X_HEADS * X_HEAD_DIM
NEG_BIG = -1e30
LOG2E = math.log2(math.e)
VMEM_LIMIT = 56 * 1024 * 1024
TOK_TILE = 1024
ATT_TILE = 512
TAB_W = 5 * LANES


def _resident(shape):
    return pl.BlockSpec(shape, lambda *_: (0,) * len(shape), pipeline_mode=pl.Buffered(1))


def _layer_resident(shape):
    return pl.BlockSpec((None,) + tuple(shape[1:]), lambda *a: (a[-1][0],) + (0,) * (len(shape) - 1),
                        pipeline_mode=pl.Buffered(1))


def _layer_grid(grid, in_specs, out_specs):
    return pltpu.PrefetchScalarGridSpec(num_scalar_prefetch=1, grid=grid, in_specs=in_specs, out_specs=out_specs)


def _params(n_axes):
    return pltpu.CompilerParams(dimension_semantics=("arbitrary",) * n_axes,
                                vmem_limit_bytes=VMEM_LIMIT)


def _rms(x, g):
    return x * lax.rsqrt(jnp.mean(x * x, axis=-1, keepdims=True) + EPS) * g


def _dot(a, b):
    return jnp.dot(a, b, preferred_element_type=F32)


def _dot_nt(a, b):
    return lax.dot_general(a, b, (((1,), (1,)), ((), ())), preferred_element_type=F32)


def _rope_table_body(pos_ref, tab_ref):
    pos = pos_ref[...]
    lane = lax.broadcasted_iota(jnp.int32, pos.shape, 1)
    sub = lane % HALF_LANES
    inv = jnp.where(sub < 32,
                    jnp.exp((-math.log(ROPE_THETA) * (2.0 / HEAD_DIM)) * sub.astype(F32)),
                    jnp.exp((-math.log(ROPE_THETA) * (2.0 / MLA_ROPE)) * (sub - 32).astype(F32)))
    ang = pos * inv
    cos2, sin2 = jnp.cos(ang), jnp.sin(ang)

    def spread(v):
        r32, r96 = pltpu.roll(v, 32, axis=1), pltpu.roll(v, 96, axis=1)
        tiled = jnp.where(lane < 32, v, jnp.where(lane < 64, r32, jnp.where(
            lane < 96, pltpu.roll(v, 64, axis=1), r96)))
        return tiled, jnp.where(lane < 64, r96, r32)

    first_half = sub < 32
    is_rope = sub < MLA_ROPE // 2
    for half in range(2):
        cos, sin = (cos2, sin2) if half == 0 else (pltpu.roll(cos2, HALF_LANES, axis=1),
                                                               pltpu.roll(sin2, HALF_LANES, axis=1))
        cos_a, cos_b = spread(cos)
        sin_a, sin_b = spread(sin)
        tab_ref[half, :, 0:LANES] = cos_a
        tab_ref[half, :, LANES:2 * LANES] = jnp.where(first_half, -sin_a, 0.0)
        tab_ref[half, :, 2 * LANES:3 * LANES] = jnp.where(first_half, 0.0, sin_a)
        tab_ref[half, :, 3 * LANES:4 * LANES] = jnp.where(is_rope, cos_b, 1.0)
        tab_ref[half, :, 4 * LANES:5 * LANES] = jnp.where(is_rope, jnp.where(lane < 64, -sin_b, sin_b), 0.0)


def _rope_tables(positions):
    t = positions.size
    half = t // 2
    pos = positions.reshape(2, half, 1).astype(F32)
    pos2 = jnp.concatenate([jnp.broadcast_to(pos[0], (half, HALF_LANES)),
                            jnp.broadcast_to(pos[1], (half, HALF_LANES))], axis=1)
    tm = min(1024, half)
    tabs = pl.pallas_call(
        _rope_table_body,
        out_shape=jax.ShapeDtypeStruct((2, half, TAB_W), F32),
        grid=(half // tm,),
        in_specs=[pl.BlockSpec((tm, LANES), lambda i: (i, 0))],
        out_specs=pl.BlockSpec((2, tm, TAB_W), lambda i: (0, i, 0)),
        compiler_params=_params(1),
        name="rope_tables",
    )(pos2)
    return tabs.reshape(t, TAB_W)


MEM_BATCH = 4


def _mem_kv_body(mem_ref, g_ref, w_ref, o_ref):
    nb, m, d = mem_ref.shape
    h = _rms(mem_ref[...].reshape(nb * m, d), g_ref[...]).astype(BF16)
    o_ref[...] = _dot(h, w_ref[...]).astype(BF16).reshape(o_ref.shape)


def _mem_kv(mem, norm_mem, w_xkv):
    b, m, d = mem.shape
    n_layers = w_xkv.shape[0]
    nb = math.gcd(b, MEM_BATCH)
    return pl.pallas_call(
        _mem_kv_body,
        out_shape=jax.ShapeDtypeStruct((n_layers, b, m, 2 * X_WIDTH), BF16),
        grid=(n_layers, b // nb),
        in_specs=[pl.BlockSpec((nb, m, d), lambda l, i: (i, 0, 0)),
                  pl.BlockSpec((None, 1, d), lambda l, i: (l, 0, 0)),
                  pl.BlockSpec((None, d, 2 * X_WIDTH), lambda l, i: (l, 0, 0))],
        out_specs=pl.BlockSpec((None, nb, m, 2 * X_WIDTH), lambda l, i: (l, i, 0, 0)),
        compiler_params=_params(2),
        name="mem_kv",
    )(mem, norm_mem, w_xkv)


_C_QK = 0
_C_VA = 2 * A_WIDTH
_C_LAT = 3 * A_WIDTH
_C_G = _C_LAT + Q_LORA + KV_LORA + LANES
_KB_W = N_HEADS * LANES


def _mixer_in_body(layer_ref, x_ref, g_ref, win_ref, qg_ref, wuq_ref, kvg_ref, wukv_ref, tab_ref,
                   qa_ref, ka_ref, va_ref, qb_ref, kb_ref, vb_ref):
    h = _rms(x_ref[...], g_ref[...]).astype(BF16)
    cos_a, sa_lo, sa_hi = tab_ref[:, 0:LANES], tab_ref[:, LANES:2 * LANES], tab_ref[:, 2 * LANES:3 * LANES]
    cos_b, sin_b = tab_ref[:, 3 * LANES:4 * LANES], tab_ref[:, 4 * LANES:5 * LANES]

    def rope_a(y):
        return (y * cos_a + pltpu.roll(y, LANES - 32, axis=1) * sa_lo
                + pltpu.roll(y, 32, axis=1) * sa_hi)

    def rope_b(y):
        return y * cos_b + pltpu.roll(y, 64, axis=1) * sin_b

    scale_a = HEAD_DIM ** -0.5 * LOG2E
    scale_b = MLA_QK ** -0.5 * LOG2E
    y = _dot(h, win_ref[:, _C_QK:_C_QK + 2 * A_WIDTH])
    for p in range(N_PAIRS):
        sl = slice(p * LANES, (p + 1) * LANES)
        qa_ref[:, sl] = (rope_a(y[:, sl]) * scale_a).astype(BF16)
        ka_ref[:, sl] = rope_a(y[:, A_WIDTH + p * LANES:A_WIDTH + (p + 1) * LANES]).astype(BF16)

    va_ref[...] = _dot(h, win_ref[:, _C_VA:_C_VA + A_WIDTH]).astype(BF16)

    lat = _dot(h, win_ref[:, _C_LAT:_C_G])
    q_lat = _rms(lat[:, 0:Q_LORA], qg_ref[...]).astype(BF16)
    kv_lat = _rms(lat[:, Q_LORA:Q_LORA + KV_LORA], kvg_ref[...]).astype(BF16)
    k_pe = rope_b(lat[:, Q_LORA + KV_LORA:])
    yq = _dot(q_lat, wuq_ref[...])
    ykv = _dot(kv_lat, wukv_ref[...])
    for hd in range(N_HEADS):
        sl = slice(hd * LANES, (hd + 1) * LANES)
        qb_ref[:, sl] = (rope_b(yq[:, sl]) * scale_b).astype(BF16)
        kb_ref[:, sl] = (ykv[:, sl] + k_pe).astype(BF16)
    vb_ref[...] = ykv[:, _KB_W:].astype(BF16)


def _mixer_in(layer, x, norm_mix, w_in, q_lat_norm, w_uq, kv_lat_norm, w_ukv, tabs):
    t, d = x.shape
    tm = min(TOK_TILE, t)
    tok = lambda w: pl.BlockSpec((tm, w), lambda i, l: (i, 0))
    widths = (A_WIDTH, A_WIDTH, A_WIDTH, _KB_W, _KB_W, A_WIDTH)
    weights = (norm_mix, w_in, q_lat_norm, w_uq, kv_lat_norm, w_ukv)
    return pl.pallas_call(
        _mixer_in_body,
        out_shape=[jax.ShapeDtypeStruct((t, w), BF16) for w in widths],
        grid_spec=_layer_grid((t // tm,), [tok(d)] + [_layer_resident(w.shape) for w in weights] + [tok(TAB_W)],
                              [tok(w) for w in widths]),
        compiler_params=_params(1),
        name="mixer_in",
    )(layer, x, *weights, tabs)


def _softmax_init(tq):
    return (jnp.full((tq, 1), NEG_BIG, F32), jnp.zeros((tq, LANES), F32))


def _softmax_step(s, v_ones, state):
    m, acc = state
    m_new = jnp.maximum(m, jnp.max(s, axis=-1, keepdims=True))
    p = jnp.exp2(s - m_new)
    acc_new = jnp.exp2(m - m_new) * acc + _dot(p.astype(BF16), v_ones)
    return m_new, acc_new


def _values_and_ones(v):
    lane = lax.broadcasted_iota(jnp.int32, v.shape, 1)
    one = jnp.ones_like(v)
    return jnp.where(lane < HEAD_DIM, v, one), jnp.where(lane < HEAD_DIM, one, v)


def _causal(s, q0, k0):
    row = lax.broadcasted_iota(jnp.int32, s.shape, 0) + q0
    col = lax.broadcasted_iota(jnp.int32, s.shape, 1) + k0
    return jnp.where(col <= row, s, NEG_BIG)


def _merge_pair(states):
    (_, acc0), (_, acc1) = states
    lane = lax.broadcasted_iota(jnp.int32, acc0.shape, 1)
    num = jnp.where(lane < HEAD_DIM, acc0, acc1)
    den = pltpu.roll(jnp.where(lane < HEAD_DIM, acc1, acc0), HEAD_DIM, axis=1)
    return num / den


def _moba_queries(q, kbar, qi, n_blocks):
    tq = q.shape[0]
    lane = lax.broadcasted_iota(jnp.int32, q.shape, 1)
    row = lax.broadcasted_iota(jnp.int32, (MAX_BLOCKS, tq), 0)
    col = lax.broadcasted_iota(jnp.int32, (MAX_BLOCKS, tq), 1)
    cur = qi * (tq // MOBA_BLOCK) + col // MOBA_BLOCK
    q_aug = []
    for hd in range(2):
        qh = jnp.where((lane < HEAD_DIM) == (hd == 0), q, jnp.zeros_like(q))
        gate = jnp.where(row < cur, _dot_nt(kbar, qh), -jnp.inf)
        rank = jnp.zeros(gate.shape, jnp.int32)
        for n in range(n_blocks):
            gn = gate[n:n + 1, :]
            beats = (gn > gate) | ((gn == gate) & (row > n))
            rank = rank + beats.astype(jnp.int32)
        keep = ((row < cur) & (rank < MOBA_TOPK)) | (row == cur)
        dropped = jnp.where(keep, 0.0, 1.0)
        dropped = jnp.concatenate([dropped, jnp.zeros((LANES - MAX_BLOCKS, tq), F32)], axis=0)
        q_aug.append(jnp.concatenate([qh, dropped.T.astype(BF16)], axis=1))
    return q_aug


def _attn_body(qa_ref, ka_ref, va_ref, e_ref, qb_ref, kb_ref, vb_ref, oa_ref, ob_ref):
    tq = ATT_TILE
    seq = ka_ref.shape[0]
    n_blocks = seq // MOBA_BLOCK
    kbar = [jnp.mean(ka_ref[n * MOBA_BLOCK:(n + 1) * MOBA_BLOCK, :].astype(F32), axis=0, keepdims=True)
            for n in range(n_blocks)]
    kbar = jnp.concatenate(kbar + [jnp.zeros((MAX_BLOCKS - n_blocks, LANES), F32)], axis=0).astype(BF16)

    def tile(qa, qb, j, states, q0, causal):
        rows = slice(j * tq, (j + 1) * tq)
        ka_aug = jnp.concatenate([ka_ref[rows, :], e_ref[rows, :]], axis=1)
        va_ones = _values_and_ones(va_ref[rows, :])
        vb_ones = _values_and_ones(vb_ref[rows, :])
        mask = functools.partial(_causal, q0=q0, k0=j * tq) if causal else (lambda s: s)
        new = []
        for hd in range(2):
            new.append(_softmax_step(mask(_dot_nt(qa[hd], ka_aug)), va_ones[hd], states[hd]))
        for hd in range(2):
            s = _dot_nt(qb[hd], kb_ref[rows, hd * LANES:(hd + 1) * LANES])
            new.append(_softmax_step(mask(s), vb_ones[hd], states[2 + hd]))
        return tuple(new)

    tq2 = 2 * tq
    for t in range(seq // tq2):
        qrows = slice(t * tq2, (t + 1) * tq2)
        qa = _moba_queries(qa_ref[qrows, :], kbar, t, n_blocks)
        qb = [qb_ref[qrows, hd * LANES:(hd + 1) * LANES] for hd in range(2)]
        states = (_softmax_init(tq2),) * 4
        for j in range(2 * t + 1):
            states = tile(qa, qb, j, states, t * tq2, j == 2 * t)
        low = tuple((m[tq:], acc[tq:]) for m, acc in states)
        low = tile([q[tq:] for q in qa], [q[tq:] for q in qb], 2 * t + 1, low, t * tq2 + tq, True)
        states = tuple((jnp.concatenate([m[:tq], ml], axis=0), jnp.concatenate([acc[:tq], accl], axis=0))
                       for (m, acc), (ml, accl) in zip(states, low))
        oa_ref[qrows, :] = _merge_pair(states[:2]).astype(oa_ref.dtype)
        ob_ref[qrows, :] = _merge_pair(states[2:]).astype(ob_ref.dtype)


def _attention(qa, ka, va, e_mat, qb, kb, vb):
    b, s, _ = qa.shape
    seq_spec = lambda w: pl.BlockSpec((None, s, w), lambda i, p: (i, 0, p))
    return pl.pallas_call(
        _attn_body,
        out_shape=[jax.ShapeDtypeStruct((b, s, A_WIDTH), BF16)] * 2,
        grid=(b, N_PAIRS),
        in_specs=[seq_spec(LANES), seq_spec(LANES), seq_spec(LANES), _resident(e_mat.shape),
                  seq_spec(2 * LANES), seq_spec(2 * LANES), seq_spec(LANES)],
        out_specs=[seq_spec(LANES)] * 2,
        compiler_params=_params(2),
        name="attention",
    )(qa, ka, va, e_mat, qb, kb, vb)


def _mixer_out_body(layer_ref, x_ref, oa_ref, ob_ref, kv_ref, gm_ref, wg_ref, wa_ref, wb_ref, wo_ref,
                    gx_ref, wxq_ref, wxo_ref, o_ref):
    x = x_ref[...]
    d = x.shape[-1]
    h = _rms(x, gm_ref[...]).astype(BF16)
    merged = (jax.nn.sigmoid(_dot(h, wg_ref[:, :d])) * _dot(oa_ref[...], wa_ref[...])
              + jax.nn.sigmoid(_dot(h, wg_ref[:, d:])) * _dot(ob_ref[...], wb_ref[...]))
    x1 = x + _dot(merged.astype(BF16), wo_ref[...])

    q = _dot(_rms(x1, gx_ref[...]).astype(BF16), wxq_ref[...]).astype(BF16)
    scale = X_HEAD_DIM ** -0.5
    heads = []
    for hd in range(X_HEADS):
        lanes = slice(hd * X_HEAD_DIM, (hd + 1) * X_HEAD_DIM)
        s = _dot_nt(q[:, lanes], kv_ref[:, lanes])
        m = jnp.max(s, axis=-1, keepdims=True)
        p = jnp.exp((s - m) * scale)
        l = jnp.sum(p, axis=-1, keepdims=True)
        v = kv_ref[:, X_WIDTH + hd * X_HEAD_DIM:X_WIDTH + (hd + 1) * X_HEAD_DIM]
        heads.append((_dot(p.astype(BF16), v) * (1.0 / l)).astype(BF16))
    o_ref[...] = x1 + _dot(jnp.concatenate(heads, axis=1), wxo_ref[...])


def _mixer_out(layer, x, seq, oa, ob, mem_kv, norm_mix, w_gates, w_a, w_b, w_out, norm_xattn, w_xq, w_xo):
    t, d = x.shape
    tm = min(TOK_TILE, seq)
    tiles_per_seq = seq // tm
    m = mem_kv.shape[2]
    tok = lambda w: pl.BlockSpec((tm, w), lambda i, l: (i, 0))
    weights = (norm_mix, w_gates, w_a, w_b, w_out, norm_xattn, w_xq, w_xo)
    return pl.pallas_call(
        _mixer_out_body,
        out_shape=jax.ShapeDtypeStruct((t, d), F32),
        grid_spec=_layer_grid(
            (t // tm,),
            [tok(d), tok(A_WIDTH), tok(A_WIDTH),
             pl.BlockSpec((None, None, m, 2 * X_WIDTH), lambda i, l: (l[0], i // tiles_per_seq, 0, 0))]
            + [_layer_resident(w.shape) for w in weights],
            tok(d)),
        compiler_params=_params(1),
        name="mixer_out",
    )(layer, x, oa, ob, mem_kv, *weights)


FF_CHUNK = 1024


def _ffn_body(layer_ref, x_ref, g_ref, wgu_ref, wd_ref, gf_ref, o_ref, *, final):
    x = x_ref[...]
    h = _rms(x, g_ref[...]).astype(BF16)
    d_ff = wd_ref.shape[0]
    y = x
    for c0 in range(0, d_ff, FF_CHUNK):
        c1 = min(c0 + FF_CHUNK, d_ff)
        gt = _dot(h, wgu_ref[:, c0:c1])
        up = _dot(h, wgu_ref[:, d_ff + c0:d_ff + c1])
        a = (gt * jax.nn.sigmoid(gt) * up).astype(BF16)
        y = y + _dot(a, wd_ref[c0:c1, :])
    o_ref[...] = _rms(y, gf_ref[...]) if final else y


def _ffn(layer, x, norm_ffn, w_gate_up, w_down, norm_final, final):
    t, d = x.shape
    tm = min(TOK_TILE, t)
    tok = pl.BlockSpec((tm, d), lambda i, l: (i, 0))
    weights = (norm_ffn, w_gate_up, w_down)
    return pl.pallas_call(
        functools.partial(_ffn_body, final=final),
        out_shape=jax.ShapeDtypeStruct((t, d), F32),
        grid_spec=_layer_grid((t // tm,), [tok] + [_layer_resident(w.shape) for w in weights]
                              + [_resident(norm_final.shape)], tok),
        compiler_params=_params(1),
        name="ffn_final" if final else "ffn",
    )(layer, x, *weights, norm_final)


PREP_ROWS = 256


def _prep_w_in_body(wt_ref, o_ref, og_ref):
    in_w, rows = wt_ref.shape
    k0 = _C_LAT + Q_LORA + KV_LORA
    n_gate_groups = (in_w - k0 - MLA_ROPE) // LANES
    half = MLA_ROPE // 2
    lane = lax.broadcasted_iota(jnp.int32, (rows, LANES), 1)

    def cols(c0):
        n = min(LANES, in_w - c0)
        blk = wt_ref[c0:c0 + n, :]
        if n < LANES:
            blk = jnp.concatenate([blk, jnp.zeros((LANES - n, rows), blk.dtype)], axis=0)
        return blk.T

    for j in range(k0 // LANES):
        o_ref[:, j * LANES:(j + 1) * LANES] = cols(j * LANES).astype(BF16)
    g = cols(k0)
    kpe = jnp.where(lane < half, g, jnp.where((lane >= 64) & (lane < 64 + half), pltpu.roll(g, 64 - half, axis=1), 0.0))
    o_ref[:, k0:k0 + LANES] = kpe.astype(BF16)
    a = g
    for j in range(n_gate_groups):
        b = cols(k0 + (j + 1) * LANES)
        shifted = pltpu.roll(jnp.where(lane >= MLA_ROPE, a, b), LANES - MLA_ROPE, axis=1)
        og_ref[:, j * LANES:(j + 1) * LANES] = shifted.astype(BF16)
        a = b


def _prep_w_in(w_in):
    n_layers, d, in_w = w_in.shape
    tr = min(PREP_ROWS, d)
    blk = lambda w: pl.BlockSpec((None, tr, w), lambda l, i: (l, i, 0))
    return pl.pallas_call(
        _prep_w_in_body,
        out_shape=[jax.ShapeDtypeStruct((n_layers, d, _C_G), BF16),
                   jax.ShapeDtypeStruct((n_layers, d, in_w - _C_G + LANES - MLA_ROPE), BF16)],
        grid=(n_layers, d // tr),
        in_specs=[pl.BlockSpec((None, in_w, tr), lambda l, i: (l, 0, i))],
        out_specs=[blk(_C_G), blk(in_w - _C_G + LANES - MLA_ROPE)],
        compiler_params=_params(2),
        name="prep_w_in",
    )(jnp.swapaxes(w_in, 1, 2))


def _mla_head_cols(nope, rope):
    half = MLA_ROPE // 2
    z = lambda n: jnp.zeros(nope.shape[:-1] + (n,), nope.dtype)
    t1, t2 = (z(half), z(half)) if rope is None else (rope[..., :half], rope[..., half:])
    return jnp.concatenate([t1, nope[..., :HEAD_DIM - half], t2, nope[..., HEAD_DIM - half:], z(LANES - MLA_QK)], axis=-1)


def _prep_weights(w_in, w_uq, w_ukv):
    n_layers = w_in.shape[0]
    w_in_r, w_gates = _prep_w_in(w_in)
    uq = w_uq.reshape(n_layers, Q_LORA, N_HEADS, MLA_QK)
    w_uq_r = _mla_head_cols(uq[..., :HEAD_DIM], uq[..., HEAD_DIM:]).reshape(n_layers, Q_LORA, _KB_W).astype(BF16)
    ukv = w_ukv.reshape(n_layers, KV_LORA, N_HEADS, 2 * HEAD_DIM)
    w_k = _mla_head_cols(ukv[..., :HEAD_DIM], None).reshape(n_layers, KV_LORA, _KB_W)
    w_v = ukv[..., HEAD_DIM:].reshape(n_layers, KV_LORA, A_WIDTH)
    w_ukv_r = jnp.concatenate([w_k, w_v], axis=-1).astype(BF16)
    return w_in_r, w_gates, w_uq_r, w_ukv_r


def kernel(x, mem, positions, norm_mix, w_in, q_lat_norm, w_uq, kv_lat_norm, w_ukv, w_branch_a, w_branch_b, w_out, norm_xattn, norm_mem, w_xq, w_xkv, w_xo, norm_ffn, w_gate_up, w_down, norm_final):
    b, s, d = x.shape
    n_layers = w_in.shape[0]
    t = b * s
    assert s % (2 * ATT_TILE) == 0 and ATT_TILE % MOBA_BLOCK == 0 and s // MOBA_BLOCK <= MAX_BLOCKS

    w_in_r, w_gates, w_uq_r, w_ukv_r = _prep_weights(w_in, w_uq, w_ukv)
    bf = lambda w: w.astype(BF16)
    row = lambda g: g.reshape(g.shape[0], 1, g.shape[1])
    tabs = _rope_tables(positions)
    mem_kv = _mem_kv(mem, row(norm_mem), bf(w_xkv))
    blk_of_row = np.arange(s) // MOBA_BLOCK
    e_mat = jnp.asarray(np.where(blk_of_row[:, None] == np.arange(LANES)[None, :], NEG_BIG, 0.0), BF16)
    norm_final = norm_final.reshape(1, d)

    norm_mix, q_lat_norm, kv_lat_norm = row(norm_mix), row(q_lat_norm), row(kv_lat_norm)
    norm_xattn, norm_ffn = row(norm_xattn), row(norm_ffn)
    w_a, w_b, w_o, w_q, w_xo_b = bf(w_branch_a), bf(w_branch_b), bf(w_out), bf(w_xq), bf(w_xo)
    w_gu, w_dn = bf(w_gate_up), bf(w_down)
    seq3 = lambda a: a.reshape(b, s, a.shape[-1])

    def layer(li, xt, final):
        li = jnp.full((1,), li, jnp.int32)
        qa, ka, va, qb, kb, vb = _mixer_in(li, xt, norm_mix, w_in_r, q_lat_norm, w_uq_r,
                                           kv_lat_norm, w_ukv_r, tabs)
        oa, ob = _attention(seq3(qa), seq3(ka), seq3(va), e_mat, seq3(qb), seq3(kb), seq3(vb))
        oa, ob = oa.reshape(t, A_WIDTH), ob.reshape(t, A_WIDTH)
        xt = _mixer_out(li, xt, s, oa, ob, mem_kv, norm_mix, w_gates, w_a, w_b, w_o, norm_xattn, w_q, w_xo_b)
        return _ffn(li, xt, norm_ffn, w_gu, w_dn, norm_final, final)

    xt = x.reshape(t, d)
    for li in range(n_layers):
        xt = layer(li, xt, li == n_layers - 1)
    return xt.reshape(b, s, d)
```

```python
import functools
import math

import numpy as np
import jax
import jax.numpy as jnp
from jax import lax
from jax.experimental import pallas as pl
from jax.experimental.pallas import tpu as pltpu

F32 = jnp.float32
BF16 = jnp.bfloat16

EPS = 1e-6
ROPE_THETA = 10000.0
LANES = 128
HALF_LANES = LANES // 2
HEAD_DIM = 64
N_HEADS = 8
N_PAIRS = N_HEADS // 2
A_WIDTH = N_HEADS * HEAD_DIM
MLA_ROPE = 32
MLA_QK = HEAD_DIM + MLA_ROPE
Q_LORA = 384
KV_LORA = 256
MOBA_BLOCK = 256
MOBA_TOPK = 3
MAX_BLOCKS = 16
X_HEADS = 4
X_HEAD_DIM = 128
X_WIDTH = X_HEADS * X_HEAD_DIM
NEG_BIG = -1e30
LOG2E = math.log2(math.e)
VMEM_LIMIT = 56 * 1024 * 1024
TOK_TILE = 1024
ATT_TILE = 512
TAB_W = 5 * LANES


def _resident(shape):
    return pl.BlockSpec(shape, lambda *_: (0,) * len(shape), pipeline_mode=pl.Buffered(1))


def _layer_resident(shape):
    return pl.BlockSpec((None,) + tuple(shape[1:]), lambda *a: (a[-1][0],) + (0,) * (len(shape) - 1),
                        pipeline_mode=pl.Buffered(1))


def _layer_grid(grid, in_specs, out_specs):
    return pltpu.PrefetchScalarGridSpec(num_scalar_prefetch=1, grid=grid, in_specs=in_specs, out_specs=out_specs)


def _params(n_axes):
    return pltpu.CompilerParams(dimension_semantics=("arbitrary",) * n_axes,
                                vmem_limit_bytes=VMEM_LIMIT)


def _rms(x, g):
    return x * lax.rsqrt(jnp.mean(x * x, axis=-1, keepdims=True) + EPS) * g


def _dot(a, b):
    return jnp.dot(a, b, preferred_element_type=F32)


def _dot_nt(a, b):
    return lax.dot_general(a, b, (((1,), (1,)), ((), ())), preferred_element_type=F32)


def _rope_table_body(pos_ref, tab_ref):
    pos = pos_ref[...]
    lane = lax.broadcasted_iota(jnp.int32, pos.shape, 1)
    sub = lane % HALF_LANES
    inv = jnp.where(sub < 32,
                    jnp.exp((-math.log(ROPE_THETA) * (2.0 / HEAD_DIM)) * sub.astype(F32)),
                    jnp.exp((-math.log(ROPE_THETA) * (2.0 / MLA_ROPE)) * (sub - 32).astype(F32)))
    ang = pos * inv
    cos2, sin2 = jnp.cos(ang), jnp.sin(ang)

    def spread(v):
        r32, r96 = pltpu.roll(v, 32, axis=1), pltpu.roll(v, 96, axis=1)
        tiled = jnp.where(lane < 32, v, jnp.where(lane < 64, r32, jnp.where(
            lane < 96, pltpu.roll(v, 64, axis=1), r96)))
        return tiled, jnp.where(lane < 64, r96, r32)

    first_half = sub < 32
    is_rope = sub < MLA_ROPE // 2
    for half in range(2):
        cos, sin = (cos2, sin2) if half == 0 else (pltpu.roll(cos2, HALF_LANES, axis=1),
                                                               pltpu.roll(sin2, HALF_LANES, axis=1))
        cos_a, cos_b = spread(cos)
        sin_a, sin_b = spread(sin)
        tab_ref[half, :, 0:LANES] = cos_a
        tab_ref[half, :, LANES:2 * LANES] = jnp.where(first_half, -sin_a, 0.0)
        tab_ref[half, :, 2 * LANES:3 * LANES] = jnp.where(first_half, 0.0, sin_a)
        tab_ref[half, :, 3 * LANES:4 * LANES] = jnp.where(is_rope, cos_b, 1.0)
        tab_ref[half, :, 4 * LANES:5 * LANES] = jnp.where(is_rope, jnp.where(lane < 64, -sin_b, sin_b), 0.0)


def _rope_tables(positions):
    t = positions.size
    half = t // 2
    pos = positions.reshape(2, half, 1).astype(F32)
    pos2 = jnp.concatenate([jnp.broadcast_to(pos[0], (half, HALF_LANES)),
                            jnp.broadcast_to(pos[1], (half, HALF_LANES))], axis=1)
    tm = min(1024, half)
    tabs = pl.pallas_call(
        _rope_table_body,
        out_shape=jax.ShapeDtypeStruct((2, half, TAB_W), F32),
        grid=(half // tm,),
        in_specs=[pl.BlockSpec((tm, LANES), lambda i: (i, 0))],
        out_specs=pl.BlockSpec((2, tm, TAB_W), lambda i: (0, i, 0)),
        compiler_params=_params(1),
        name="rope_tables",
    )(pos2)
    return tabs.reshape(t, TAB_W)


MEM_BATCH = 8


def _mem_kv_body(mem_ref, g_ref, w_ref, o_ref):
    nb, m, d = mem_ref.shape
    h = _rms(mem_ref[...].reshape(nb * m, d), g_ref[...]).astype(BF16)
    o_ref[...] = _dot(h, w_ref[...]).astype(BF16).reshape(o_ref.shape)


def _mem_kv(mem, norm_mem, w_xkv):
    b, m, d = mem.shape
    n_layers = w_xkv.shape[0]
    nb = math.gcd(b, MEM_BATCH)
    return pl.pallas_call(
        _mem_kv_body,
        out_shape=jax.ShapeDtypeStruct((n_layers, b, m, 2 * X_WIDTH), BF16),
        grid=(n_layers, b // nb),
        in_specs=[pl.BlockSpec((nb, m, d), lambda l, i: (i, 0, 0)),
                  pl.BlockSpec((None, 1, d), lambda l, i: (l, 0, 0)),
                  pl.BlockSpec((None, d, 2 * X_WIDTH), lambda l, i: (l, 0, 0))],
        out_specs=pl.BlockSpec((None, nb, m, 2 * X_WIDTH), lambda l, i: (l, i, 0, 0)),
        compiler_params=_params(2),
        name="mem_kv",
    )(mem, norm_mem, w_xkv)


_C_QK = 0
_C_VA = 2 * A_WIDTH
_C_LAT = 3 * A_WIDTH
_C_G = _C_LAT + Q_LORA + KV_LORA + LANES
_KB_W = N_HEADS * LANES


def _mixer_in_body(layer_ref, x_ref, g_ref, win_ref, qg_ref, wuq_ref, kvg_ref, wukv_ref, tab_ref,
                   qa_ref, ka_ref, va_ref, qb_ref, kb_ref, vb_ref):
    h = _rms(x_ref[...], g_ref[...]).astype(BF16)
    cos_a, sa_lo, sa_hi = tab_ref[:, 0:LANES], tab_ref[:, LANES:2 * LANES], tab_ref[:, 2 * LANES:3 * LANES]
    cos_b, sin_b = tab_ref[:, 3 * LANES:4 * LANES], tab_ref[:, 4 * LANES:5 * LANES]

    def rope_a(y):
        return (y * cos_a + pltpu.roll(y, LANES - 32, axis=1) * sa_lo
                + pltpu.roll(y, 32, axis=1) * sa_hi)

    def rope_b(y):
        return y * cos_b + pltpu.roll(y, 64, axis=1) * sin_b

    scale_a = HEAD_DIM ** -0.5 * LOG2E
    scale_b = MLA_QK ** -0.5 * LOG2E
    lat = _dot(h, win_ref[:, _C_LAT:_C_G])
    y = _dot(h, win_ref[:, _C_QK:_C_QK + 2 * A_WIDTH])
    q_lat = _rms(lat[:, 0:Q_LORA], qg_ref[...]).astype(BF16)
    kv_lat = _rms(lat[:, Q_LORA:Q_LORA + KV_LORA], kvg_ref[...]).astype(BF16)
    for p in range(N_PAIRS):
        sl = slice(p * LANES, (p + 1) * LANES)
        qa_ref[:, sl] = (rope_a(y[:, sl]) * scale_a).astype(BF16)
        ka_ref[:, sl] = rope_a(y[:, A_WIDTH + p * LANES:A_WIDTH + (p + 1) * LANES]).astype(BF16)
    yq = _dot(q_lat, wuq_ref[...])
    ykv = _dot(kv_lat, wukv_ref[...])
    va_ref[...] = _dot(h, win_ref[:, _C_VA:_C_VA + A_WIDTH]).astype(BF16)
    k_pe = rope_b(lat[:, Q_LORA + KV_LORA:])
    for hd in range(N_HEADS):
        sl = slice(hd * LANES, (hd + 1) * LANES)
        qb_ref[:, sl] = (rope_b(yq[:, sl]) * scale_b).astype(BF16)
        kb_ref[:, sl] = (ykv[:, sl] + k_pe).astype(BF16)
    vb_ref[...] = ykv[:, _KB_W:].astype(BF16)


def _mixer_in(layer, x, norm_mix, w_in, q_lat_norm, w_uq, kv_lat_norm, w_ukv, tabs):
    t, d = x.shape
    tm = min(TOK_TILE, t)
    tok = lambda w: pl.BlockSpec((tm, w), lambda i, l: (i, 0))
    widths = (A_WIDTH, A_WIDTH, A_WIDTH, _KB_W, _KB_W, A_WIDTH)
    weights = (norm_mix, w_in, q_lat_norm, w_uq, kv_lat_norm, w_ukv)
    return pl.pallas_call(
        _mixer_in_body,
        out_shape=[jax.ShapeDtypeStruct((t, w), BF16) for w in widths],
        grid_spec=_layer_grid((t // tm,), [tok(d)] + [_layer_resident(w.shape) for w in weights] + [tok(TAB_W)],
                              [tok(w) for w in widths]),
        compiler_params=_params(1),
        name="mixer_in",
    )(layer, x, *weights, tabs)


def _softmax_init(tq):
    return (jnp.full((tq, 1), NEG_BIG, F32), jnp.zeros((tq, LANES), F32))


def _softmax_step(s, v_ones, state):
    m, acc = state
    m_new = jnp.maximum(m, jnp.max(s, axis=-1, keepdims=True))
    p = jnp.exp2(s - m_new)
    acc_new = jnp.exp2(m - m_new) * acc + _dot(p.astype(BF16), v_ones)
    return m_new, acc_new


def _values_and_ones(v):
    lane = lax.broadcasted_iota(jnp.int32, v.shape, 1)
    one = jnp.ones_like(v)
    return jnp.where(lane < HEAD_DIM, v, one), jnp.where(lane < HEAD_DIM, one, v)


def _causal(s, q0, k0):
    row = lax.broadcasted_iota(jnp.int32, s.shape, 0) + q0
    col = lax.broadcasted_iota(jnp.int32, s.shape, 1) + k0
    return jnp.where(col <= row, s, NEG_BIG)


def _merge_pair(states):
    (_, acc0), (_, acc1) = states
    lane = lax.broadcasted_iota(jnp.int32, acc0.shape, 1)
    num = jnp.where(lane < HEAD_DIM, acc0, acc1)
    den = pltpu.roll(jnp.where(lane < HEAD_DIM, acc1, acc0), HEAD_DIM, axis=1)
    return num / den


def _moba_queries(q, kbar, qi, n_blocks):
    tq = q.shape[0]
    lane = lax.broadcasted_iota(jnp.int32, q.shape, 1)
    row = lax.broadcasted_iota(jnp.int32, (MAX_BLOCKS, tq), 0)
    col = lax.broadcasted_iota(jnp.int32, (MAX_BLOCKS, tq), 1)
    cur = qi * (tq // MOBA_BLOCK) + col // MOBA_BLOCK
    q_aug = []
    for hd in range(2):
        qh = jnp.where((lane < HEAD_DIM) == (hd == 0), q, jnp.zeros_like(q))
        gate = jnp.where(row < cur, _dot_nt(kbar, qh), -jnp.inf)
        rank = jnp.zeros(gate.shape, jnp.int32)
        for n in range(n_blocks):
            gn = gate[n:n + 1, :]
            beats = (gn > gate) | ((gn == gate) & (row > n))
            rank = rank + beats.astype(jnp.int32)
        keep = ((row < cur) & (rank < MOBA_TOPK)) | (row == cur)
        dropped = jnp.where(keep, 0.0, 1.0)
        dropped = jnp.concatenate([dropped, jnp.zeros((LANES - MAX_BLOCKS, tq), F32)], axis=0)
        q_aug.append(jnp.concatenate([qh, dropped.T.astype(BF16)], axis=1))
    return q_aug


def _attn_body(qa_ref, ka_ref, va_ref, e_ref, qb_ref, kb_ref, vb_ref, oa_ref, ob_ref):
    tq = ATT_TILE
    seq = ka_ref.shape[0]
    n_blocks = seq // MOBA_BLOCK
    kbar = [jnp.mean(ka_ref[n * MOBA_BLOCK:(n + 1) * MOBA_BLOCK, :].astype(F32), axis=0, keepdims=True)
            for n in range(n_blocks)]
    kbar = jnp.concatenate(kbar + [jnp.zeros((MAX_BLOCKS - n_blocks, LANES), F32)], axis=0).astype(BF16)

    def tile(qa, qb, j, states, q0, causal):
        rows = slice(j * tq, (j + 1) * tq)
        ka_aug = jnp.concatenate([ka_ref[rows, :], e_ref[rows, :]], axis=1)
        va_ones = _values_and_ones(va_ref[rows, :])
        vb_ones = _values_and_ones(vb_ref[rows, :])
        mask = functools.partial(_causal, q0=q0, k0=j * tq) if causal else (lambda s: s)
        new = []
        rows_q = qa[0].shape[0]
        s_both = _dot_nt(jnp.concatenate(qa, axis=0), ka_aug)
        for hd in range(2):
            new.append(_softmax_step(mask(s_both[hd * rows_q:(hd + 1) * rows_q]), va_ones[hd], states[hd]))
        for hd in range(2):
            s = _dot_nt(qb[hd], kb_ref[rows, hd * LANES:(hd + 1) * LANES])
            new.append(_softmax_step(mask(s), vb_ones[hd], states[2 + hd]))
        return tuple(new)

    tq2 = 2 * tq
    for t in range(seq // tq2):
        qrows = slice(t * tq2, (t + 1) * tq2)
        qa = _moba_queries(qa_ref[qrows, :], kbar, t, n_blocks)
        qb = [qb_ref[qrows, hd * LANES:(hd + 1) * LANES] for hd in range(2)]
        states = (_softmax_init(tq2),) * 4
        for j in range(2 * t + 1):
            states = tile(qa, qb, j, states, t * tq2, j == 2 * t)
        low = tuple((m[tq:], acc[tq:]) for m, acc in states)
        low = tile([q[tq:] for q in qa], [q[tq:] for q in qb], 2 * t + 1, low, t * tq2 + tq, True)
        states = tuple((jnp.concatenate([m[:tq], ml], axis=0), jnp.concatenate([acc[:tq], accl], axis=0))
                       for (m, acc), (ml, accl) in zip(states, low))
        oa_ref[qrows, :] = _merge_pair(states[:2]).astype(oa_ref.dtype)
        ob_ref[qrows, :] = _merge_pair(states[2:]).astype(ob_ref.dtype)


def _attention(qa, ka, va, e_mat, qb, kb, vb):
    b, s, _ = qa.shape
    seq_spec = lambda w: pl.BlockSpec((None, s, w), lambda i, p: (i, 0, p))
    return pl.pallas_call(
        _attn_body,
        out_shape=[jax.ShapeDtypeStruct((b, s, A_WIDTH), BF16)] * 2,
        grid=(b, N_PAIRS),
        in_specs=[seq_spec(LANES), seq_spec(LANES), seq_spec(LANES), _resident(e_mat.shape),
                  seq_spec(2 * LANES), seq_spec(2 * LANES), seq_spec(LANES)],
        out_specs=[seq_spec(LANES)] * 2,
        compiler_params=_params(2),
        name="attention",
    )(qa, ka, va, e_mat, qb, kb, vb)


def _mixer_out_body(layer_ref, x_ref, oa_ref, ob_ref, kv_ref, gm_ref, wg_ref, wa_ref, wb_ref, wo_ref,
                    gx_ref, wxq_ref, wxo_ref, o_ref):
    x = x_ref[...]
    d = x.shape[-1]
    h = _rms(x, gm_ref[...]).astype(BF16)
    merged = (jax.nn.sigmoid(_dot(h, wg_ref[:, :d])) * _dot(oa_ref[...], wa_ref[...])
              + jax.nn.sigmoid(_dot(h, wg_ref[:, d:])) * _dot(ob_ref[...], wb_ref[...]))
    x1 = x + _dot(merged.astype(BF16), wo_ref[...])

    q = _dot(_rms(x1, gx_ref[...]).astype(BF16), wxq_ref[...]).astype(BF16)
    scale = X_HEAD_DIM ** -0.5
    heads = []
    for hd in range(X_HEADS):
        lanes = slice(hd * X_HEAD_DIM, (hd + 1) * X_HEAD_DIM)
        s = _dot_nt(q[:, lanes], kv_ref[:, lanes])
        m = jnp.max(s, axis=-1, keepdims=True)
        p = jnp.exp((s - m) * scale)
        l = jnp.sum(p, axis=-1, keepdims=True)
        v = kv_ref[:, X_WIDTH + hd * X_HEAD_DIM:X_WIDTH + (hd + 1) * X_HEAD_DIM]
        heads.append((_dot(p.astype(BF16), v) * (1.0 / l)).astype(BF16))
    o_ref[...] = x1 + _dot(jnp.concatenate(heads, axis=1), wxo_ref[...])


def _mixer_out(layer, x, seq, oa, ob, mem_kv, norm_mix, w_gates, w_a, w_b, w_out, norm_xattn, w_xq, w_xo):
    t, d = x.shape
    tm = min(TOK_TILE, seq)
    tiles_per_seq = seq // tm
    m = mem_kv.shape[2]
    tok = lambda w: pl.BlockSpec((tm, w), lambda i, l: (i, 0))
    weights = (norm_mix, w_gates, w_a, w_b, w_out, norm_xattn, w_xq, w_xo)
    return pl.pallas_call(
        _mixer_out_body,
        out_shape=jax.ShapeDtypeStruct((t, d), F32),
        grid_spec=_layer_grid(
            (t // tm,),
            [tok(d), tok(A_WIDTH), tok(A_WIDTH),
             pl.BlockSpec((None, None, m, 2 * X_WIDTH), lambda i, l: (l[0], i // tiles_per_seq, 0, 0))]
            + [_layer_resident(w.shape) for w in weights],
            tok(d)),
        compiler_params=_params(1),
        name="mixer_out",
    )(layer, x, oa, ob, mem_kv, *weights)


FF_CHUNK = 1024


def _ffn_body(layer_ref, x_ref, g_ref, wgu_ref, wd_ref, gf_ref, o_ref, *, final):
    x = x_ref[...]
    h = _rms(x, g_ref[...]).astype(BF16)
    d_ff = wd_ref.shape[0]
    y = x
    for c0 in range(0, d_ff, FF_CHUNK):
        c1 = min(c0 + FF_CHUNK, d_ff)
        gt = _dot(h, wgu_ref[:, c0:c1])
        up = _dot(h, wgu_ref[:, d_ff + c0:d_ff + c1])
        a = (gt * jax.nn.sigmoid(gt) * up).astype(BF16)
        y = y + _dot(a, wd_ref[c0:c1, :])
    o_ref[...] = _rms(y, gf_ref[...]) if final else y


def _ffn(layer, x, norm_ffn, w_gate_up, w_down, norm_final, final):
    t, d = x.shape
    tm = min(TOK_TILE, t)
    tok = pl.BlockSpec((tm, d), lambda i, l: (i, 0))
    weights = (norm_ffn, w_gate_up, w_down)
    return pl.pallas_call(
        functools.partial(_ffn_body, final=final),
        out_shape=jax.ShapeDtypeStruct((t, d), F32),
        grid_spec=_layer_grid((t // tm,), [tok] + [_layer_resident(w.shape) for w in weights]
                              + [_resident(norm_final.shape)], tok),
        compiler_params=_params(1),
        name="ffn_final" if final else "ffn",
    )(layer, x, *weights, norm_final)


PREP_ROWS = 256


def _prep_w_in_body(wt_ref, o_ref, og_ref):
    in_w, rows = wt_ref.shape
    k0 = _C_LAT + Q_LORA + KV_LORA
    n_gate_groups = (in_w - k0 - MLA_ROPE) // LANES
    half = MLA_ROPE // 2
    lane = lax.broadcasted_iota(jnp.int32, (rows, LANES), 1)

    def cols(c0):
        n = min(LANES, in_w - c0)
        blk = wt_ref[c0:c0 + n, :]
        if n < LANES:
            blk = jnp.concatenate([blk, jnp.zeros((LANES - n, rows), blk.dtype)], axis=0)
        return blk.T

    for j in range(k0 // LANES):
        o_ref[:, j * LANES:(j + 1) * LANES] = cols(j * LANES).astype(BF16)
    g = cols(k0)
    kpe = jnp.where(lane < half, g, jnp.where((lane >= 64) & (lane < 64 + half), pltpu.roll(g, 64 - half, axis=1), 0.0))
    o_ref[:, k0:k0 + LANES] = kpe.astype(BF16)
    a = g
    for j in range(n_gate_groups):
        b = cols(k0 + (j + 1) * LANES)
        shifted = pltpu.roll(jnp.where(lane >= MLA_ROPE, a, b), LANES - MLA_ROPE, axis=1)
        og_ref[:, j * LANES:(j + 1) * LANES] = shifted.astype(BF16)
        a = b


def _prep_w_in(w_in):
    n_layers, d, in_w = w_in.shape
    tr = min(PREP_ROWS, d)
    blk = lambda w: pl.BlockSpec((None, tr, w), lambda l, i: (l, i, 0))
    return pl.pallas_call(
        _prep_w_in_body,
        out_shape=[jax.ShapeDtypeStruct((n_layers, d, _C_G), BF16),
                   jax.ShapeDtypeStruct((n_layers, d, in_w - _C_G + LANES - MLA_ROPE), BF16)],
        grid=(n_layers, d // tr),
        in_specs=[pl.BlockSpec((None, in_w, tr), lambda l, i: (l, 0, i))],
        out_specs=[blk(_C_G), blk(in_w - _C_G + LANES - MLA_ROPE)],
        compiler_params=_params(2),
        name="prep_w_in",
    )(jnp.swapaxes(w_in, 1, 2))


def _mla_head_cols(nope, rope):
    half = MLA_ROPE // 2
    z = lambda n: jnp.zeros(nope.shape[:-1] + (n,), nope.dtype)
    t1, t2 = (z(half), z(half)) if rope is None else (rope[..., :half], rope[..., half:])
    return jnp.concatenate([t1, nope[..., :HEAD_DIM - half], t2, nope[..., HEAD_DIM - half:], z(LANES - MLA_QK)], axis=-1)


def _prep_weights(w_in, w_uq, w_ukv):
    n_layers = w_in.shape[0]
    w_in_r, w_gates = _prep_w_in(w_in)
    uq = w_uq.reshape(n_layers, Q_LORA, N_HEADS, MLA_QK)
    w_uq_r = _mla_head_cols(uq[..., :HEAD_DIM], uq[..., HEAD_DIM:]).reshape(n_layers, Q_LORA, _KB_W).astype(BF16)
    ukv = w_ukv.reshape(n_layers, KV_LORA, N_HEADS, 2 * HEAD_DIM)
    w_k = _mla_head_cols(ukv[..., :HEAD_DIM], None).reshape(n_layers, KV_LORA, _KB_W)
    w_v = ukv[..., HEAD_DIM:].reshape(n_layers, KV_LORA, A_WIDTH)
    w_ukv_r = jnp.concatenate([w_k, w_v], axis=-1).astype(BF16)
    return w_in_r, w_gates, w_uq_r, w_ukv_r


def kernel(x, mem, positions, norm_mix, w_in, q_lat_norm, w_uq, kv_lat_norm, w_ukv, w_branch_a, w_branch_b, w_out, norm_xattn, norm_mem, w_xq, w_xkv, w_xo, norm_ffn, w_gate_up, w_down, norm_final):
    b, s, d = x.shape
    n_layers = w_in.shape[0]
    t = b * s
    assert s % (2 * ATT_TILE) == 0 and ATT_TILE % MOBA_BLOCK == 0 and s // MOBA_BLOCK <= MAX_BLOCKS

    w_in_r, w_gates, w_uq_r, w_ukv_r = _prep_weights(w_in, w_uq, w_ukv)
    bf = lambda w: w.astype(BF16)
    row = lambda g: g.reshape(g.shape[0], 1, g.shape[1])
    tabs = _rope_tables(positions)
    mem_kv = _mem_kv(mem, row(norm_mem), bf(w_xkv))
    blk_of_row = np.arange(s) // MOBA_BLOCK
    e_mat = jnp.asarray(np.where(blk_of_row[:, None] == np.arange(LANES)[None, :], NEG_BIG, 0.0), BF16)
    norm_final = norm_final.reshape(1, d)

    norm_mix, q_lat_norm, kv_lat_norm = row(norm_mix), row(q_lat_norm), row(kv_lat_norm)
    norm_xattn, norm_ffn = row(norm_xattn), row(norm_ffn)
    w_a, w_b, w_o, w_q, w_xo_b = bf(w_branch_a), bf(w_branch_b), bf(w_out), bf(w_xq), bf(w_xo)
    w_gu, w_dn = bf(w_gate_up), bf(w_down)
    seq3 = lambda a: a.reshape(b, s, a.shape[-1])

    def layer(li, xt, final):
        li = jnp.full((1,), li, jnp.int32)
        qa, ka, va, qb, kb, vb = _mixer_in(li, xt, norm_mix, w_in_r, q_lat_norm, w_uq_r,
                                           kv_lat_norm, w_ukv_r, tabs)
        oa, ob = _attention(seq3(qa), seq3(ka), seq3(va), e_mat, seq3(qb), seq3(kb), seq3(vb))
        oa, ob = oa.reshape(t, A_WIDTH), ob.reshape(t, A_WIDTH)
        xt = _mixer_out(li, xt, s, oa, ob, mem_kv, norm_mix, w_gates, w_a, w_b, w_o, norm_xattn, w_q, w_xo_b)
        return _ffn(li, xt, norm_ffn, w_gu, w_dn, norm_final, final)

    xt = x.reshape(t, d)
    for li in range(n_layers):
        xt = layer(li, xt, li == n_layers - 1)
    return xt.reshape(b, s, d)
```

```python
import functools
import math

import numpy as np
import jax
import jax.numpy as jnp
from jax import lax
from jax.experimental import pallas as pl
from jax.experimental.pallas import tpu as pltpu

F32 = jnp.float32
BF16 = jnp.bfloat16

EPS = 1e-6
ROPE_THETA = 10000.0
LANES = 128
HALF_LANES = LANES // 2
HEAD_DIM = 64
N_HEADS = 8
N_PAIRS = N_HEADS // 2
A_WIDTH = N_HEADS * HEAD_DIM
MLA_ROPE = 32
MLA_QK = HEAD_DIM + MLA_ROPE
Q_LORA = 384
KV_LORA = 256
MOBA_BLOCK = 256
MOBA_TOPK = 3
MAX_BLOCKS = 16
X_HEADS = 4
X_HEAD_DIM = 128
X_WIDTH = X_HEADS * X_HEAD_DIM
NEG_BIG = -1e30
LOG2E = math.log2(math.e)
VMEM_LIMIT = 56 * 1024 * 1024
TOK_TILE = 1024
ATT_TILE = 512
TAB_W = 5 * LANES


def _resident(shape):
    return pl.BlockSpec(shape, lambda *_: (0,) * len(shape), pipeline_mode=pl.Buffered(1))


def _layer_resident(shape):
    return pl.BlockSpec((None,) + tuple(shape[1:]), lambda *a: (a[-1][0],) + (0,) * (len(shape) - 1),
                        pipeline_mode=pl.Buffered(1))


def _layer_grid(grid, in_specs, out_specs):
    return pltpu.PrefetchScalarGridSpec(num_scalar_prefetch=1, grid=grid, in_specs=in_specs, out_specs=out_specs)


def _params(n_axes):
    return pltpu.CompilerParams(dimension_semantics=("arbitrary",) * n_axes,
                                vmem_limit_bytes=VMEM_LIMIT)


def _rms(x, g):
    return x * lax.rsqrt(jnp.mean(x * x, axis=-1, keepdims=True) + EPS) * g


def _dot(a, b):
    return jnp.dot(a, b, preferred_element_type=F32)


def _dot_nt(a, b):
    return lax.dot_general(a, b, (((1,), (1,)), ((), ())), preferred_element_type=F32)


def _rope_table_body(pos_ref, tab_ref):
    pos = pos_ref[...]
    lane = lax.broadcasted_iota(jnp.int32, pos.shape, 1)
    sub = lane % HALF_LANES
    qtr = HEAD_DIM // 2
    inv = jnp.where(sub < qtr,
                    jnp.exp((-math.log(ROPE_THETA) * (2.0 / HEAD_DIM)) * sub.astype(F32)),
                    jnp.exp((-math.log(ROPE_THETA) * (2.0 / MLA_ROPE)) * (sub - qtr).astype(F32)))
    ang = pos * inv
    cos2, sin2 = jnp.cos(ang), jnp.sin(ang)

    def spread(v):
        r1, r3 = pltpu.roll(v, qtr, axis=1), pltpu.roll(v, 3 * qtr, axis=1)
        tiled = jnp.where(lane < qtr, v, jnp.where(lane < 2 * qtr, r1, jnp.where(
            lane < 3 * qtr, pltpu.roll(v, 2 * qtr, axis=1), r3)))
        return tiled, jnp.where(lane < HALF_LANES, r3, r1)

    first_half = sub < qtr
    is_rope = sub < MLA_ROPE // 2
    for half in range(2):
        cos, sin = (cos2, sin2) if half == 0 else (pltpu.roll(cos2, HALF_LANES, axis=1),
                                                               pltpu.roll(sin2, HALF_LANES, axis=1))
        cos_a, cos_b = spread(cos)
        sin_a, sin_b = spread(sin)
        tab_ref[half, :, 0:LANES] = cos_a
        tab_ref[half, :, LANES:2 * LANES] = jnp.where(first_half, -sin_a, 0.0)
        tab_ref[half, :, 2 * LANES:3 * LANES] = jnp.where(first_half, 0.0, sin_a)
        tab_ref[half, :, 3 * LANES:4 * LANES] = jnp.where(is_rope, cos_b, 1.0)
        tab_ref[half, :, 4 * LANES:5 * LANES] = jnp.where(is_rope, jnp.where(lane < HALF_LANES, -sin_b, sin_b), 0.0)


def _rope_tables(positions):
    t = positions.size
    half = t // 2
    pos = positions.reshape(2, half, 1).astype(F32)
    pos2 = jnp.concatenate([jnp.broadcast_to(pos[0], (half, HALF_LANES)),
                            jnp.broadcast_to(pos[1], (half, HALF_LANES))], axis=1)
    tm = min(1024, half)
    tabs = pl.pallas_call(
        _rope_table_body,
        out_shape=jax.ShapeDtypeStruct((2, half, TAB_W), F32),
        grid=(half // tm,),
        in_specs=[pl.BlockSpec((tm, LANES), lambda i: (i, 0))],
        out_specs=pl.BlockSpec((2, tm, TAB_W), lambda i: (0, i, 0)),
        compiler_params=_params(1),
        name="rope_tables",
    )(pos2)
    return tabs.reshape(t, TAB_W)


MEM_BATCH = 8


def _mem_kv_body(mem_ref, g_ref, w_ref, o_ref):
    nb, m, d = mem_ref.shape
    h = _rms(mem_ref[...].reshape(nb * m, d), g_ref[...]).astype(BF16)
    o_ref[...] = _dot(h, w_ref[...]).astype(BF16).reshape(o_ref.shape)


def _mem_kv(mem, norm_mem, w_xkv):
    b, m, d = mem.shape
    n_layers = w_xkv.shape[0]
    nb = math.gcd(b, MEM_BATCH)
    return pl.pallas_call(
        _mem_kv_body,
        out_shape=jax.ShapeDtypeStruct((n_layers, b, m, 2 * X_WIDTH), BF16),
        grid=(n_layers, b // nb),
        in_specs=[pl.BlockSpec((nb, m, d), lambda l, i: (i, 0, 0)),
                  pl.BlockSpec((None, 1, d), lambda l, i: (l, 0, 0)),
                  pl.BlockSpec((None, d, 2 * X_WIDTH), lambda l, i: (l, 0, 0))],
        out_specs=pl.BlockSpec((None, nb, m, 2 * X_WIDTH), lambda l, i: (l, i, 0, 0)),
        compiler_params=_params(2),
        name="mem_kv",
    )(mem, norm_mem, w_xkv)


_C_QK = 0
_C_VA = 2 * A_WIDTH
_C_LAT = 3 * A_WIDTH
_C_G = _C_LAT + Q_LORA + KV_LORA + LANES
_KB_W = N_HEADS * LANES


def _mixer_in_body(layer_ref, x_ref, g_ref, win_ref, qg_ref, wuq_ref, kvg_ref, wukv_ref, tab_ref,
                   qa_ref, ka_ref, va_ref, qb_ref, kb_ref, vb_ref):
    h = _rms(x_ref[...], g_ref[...]).astype(BF16)
    cos_a, sa_lo, sa_hi = tab_ref[:, 0:LANES], tab_ref[:, LANES:2 * LANES], tab_ref[:, 2 * LANES:3 * LANES]
    cos_b, sin_b = tab_ref[:, 3 * LANES:4 * LANES], tab_ref[:, 4 * LANES:5 * LANES]

    def rope_a(y):
        return (y * cos_a + pltpu.roll(y, LANES - HEAD_DIM // 2, axis=1) * sa_lo
                + pltpu.roll(y, HEAD_DIM // 2, axis=1) * sa_hi)

    def rope_b(y):
        return y * cos_b + pltpu.roll(y, HALF_LANES, axis=1) * sin_b

    scale_a = HEAD_DIM ** -0.5 * LOG2E
    scale_b = MLA_QK ** -0.5 * LOG2E
    lat = _dot(h, win_ref[:, _C_LAT:_C_G])
    y = _dot(h, win_ref[:, _C_QK:_C_QK + 2 * A_WIDTH])
    q_lat = _rms(lat[:, 0:Q_LORA], qg_ref[...]).astype(BF16)
    kv_lat = _rms(lat[:, Q_LORA:Q_LORA + KV_LORA], kvg_ref[...]).astype(BF16)
    for p in range(N_PAIRS):
        sl = slice(p * LANES, (p + 1) * LANES)
        qa_ref[:, sl] = (rope_a(y[:, sl]) * scale_a).astype(BF16)
        ka_ref[:, sl] = rope_a(y[:, A_WIDTH + p * LANES:A_WIDTH + (p + 1) * LANES]).astype(BF16)
    yq = _dot(q_lat, wuq_ref[...])
    ykv = _dot(kv_lat, wukv_ref[...])
    va_ref[...] = _dot(h, win_ref[:, _C_VA:_C_VA + A_WIDTH]).astype(BF16)
    k_pe = rope_b(lat[:, Q_LORA + KV_LORA:])
    for hd in range(N_HEADS):
        sl = slice(hd * LANES, (hd + 1) * LANES)
        qb_ref[:, sl] = (rope_b(yq[:, sl]) * scale_b).astype(BF16)
        kb_ref[:, sl] = (ykv[:, sl] + k_pe).astype(BF16)
    vb_ref[...] = ykv[:, _KB_W:].astype(BF16)


def _mixer_in(layer, x, norm_mix, w_in, q_lat_norm, w_uq, kv_lat_norm, w_ukv, tabs):
    t, d = x.shape
    tm = min(TOK_TILE, t)
    tok = lambda w: pl.BlockSpec((tm, w), lambda i, l: (i, 0))
    widths = (A_WIDTH, A_WIDTH, A_WIDTH, _KB_W, _KB_W, A_WIDTH)
    weights = (norm_mix, w_in, q_lat_norm, w_uq, kv_lat_norm, w_ukv)
    return pl.pallas_call(
        _mixer_in_body,
        out_shape=[jax.ShapeDtypeStruct((t, w), BF16) for w in widths],
        grid_spec=_layer_grid((t // tm,), [tok(d)] + [_layer_resident(w.shape) for w in weights] + [tok(TAB_W)],
                              [tok(w) for w in widths]),
        compiler_params=_params(1),
        name="mixer_in",
    )(layer, x, *weights, tabs)


def _softmax_init(tq):
    return (jnp.full((tq, 1), NEG_BIG, F32), jnp.zeros((tq, LANES), F32))


def _softmax_step(s, v_ones, state):
    m, acc = state
    m_new = jnp.maximum(m, jnp.max(s, axis=-1, keepdims=True))
    p = jnp.exp2(s - m_new)
    acc_new = jnp.exp2(m - m_new) * acc + _dot(p.astype(BF16), v_ones)
    return m_new, acc_new


def _values_and_ones(v):
    lane = lax.broadcasted_iota(jnp.int32, v.shape, 1)
    one = jnp.ones_like(v)
    return jnp.where(lane < HEAD_DIM, v, one), jnp.where(lane < HEAD_DIM, one, v)


def _causal(s, q0, k0):
    row = lax.broadcasted_iota(jnp.int32, s.shape, 0) + q0
    col = lax.broadcasted_iota(jnp.int32, s.shape, 1) + k0
    return jnp.where(col <= row, s, NEG_BIG)


def _merge_pair(states):
    (_, acc0), (_, acc1) = states
    lane = lax.broadcasted_iota(jnp.int32, acc0.shape, 1)
    num = jnp.where(lane < HEAD_DIM, acc0, acc1)
    den = pltpu.roll(jnp.where(lane < HEAD_DIM, acc1, acc0), HEAD_DIM, axis=1)
    return num / den


def _moba_queries(q, kbar, qi, n_blocks):
    tq = q.shape[0]
    lane = lax.broadcasted_iota(jnp.int32, q.shape, 1)
    row = lax.broadcasted_iota(jnp.int32, (MAX_BLOCKS, tq), 0)
    col = lax.broadcasted_iota(jnp.int32, (MAX_BLOCKS, tq), 1)
    cur = qi * (tq // MOBA_BLOCK) + col // MOBA_BLOCK
    q_aug = []
    for hd in range(2):
        qh = jnp.where((lane < HEAD_DIM) == (hd == 0), q, jnp.zeros_like(q))
        gate = jnp.where(row < cur, _dot_nt(kbar, qh), -jnp.inf)
        rank = jnp.zeros(gate.shape, jnp.int32)
        for n in range(n_blocks):
            gn = gate[n:n + 1, :]
            beats = (gn > gate) | ((gn == gate) & (row > n))
            rank = rank + beats.astype(jnp.int32)
        keep = ((row < cur) & (rank < MOBA_TOPK)) | (row == cur)
        dropped = jnp.where(keep, 0.0, 1.0)
        dropped = jnp.concatenate([dropped, jnp.zeros((LANES - MAX_BLOCKS, tq), F32)], axis=0)
        q_aug.append(jnp.concatenate([qh, dropped.T.astype(BF16)], axis=1))
    return q_aug


def _attn_body(qa_ref, ka_ref, va_ref, e_ref, qb_ref, kb_ref, vb_ref, oa_ref, ob_ref):
    tq = ATT_TILE
    seq = ka_ref.shape[0]
    n_blocks = seq // MOBA_BLOCK
    kbar = [jnp.mean(ka_ref[n * MOBA_BLOCK:(n + 1) * MOBA_BLOCK, :].astype(F32), axis=0, keepdims=True)
            for n in range(n_blocks)]
    kbar = jnp.concatenate(kbar + [jnp.zeros((MAX_BLOCKS - n_blocks, LANES), F32)], axis=0).astype(BF16)

    def tile(qa, qb, j, states, q0, causal):
        rows = slice(j * tq, (j + 1) * tq)
        ka_aug = jnp.concatenate([ka_ref[rows, :], e_ref[rows, :]], axis=1)
        va_ones = _values_and_ones(va_ref[rows, :])
        vb_ones = _values_and_ones(vb_ref[rows, :])
        mask = functools.partial(_causal, q0=q0, k0=j * tq) if causal else (lambda s: s)
        new = []
        rows_q = qa[0].shape[0]
        s_both = _dot_nt(jnp.concatenate(qa, axis=0), ka_aug)
        for hd in range(2):
            new.append(_softmax_step(mask(s_both[hd * rows_q:(hd + 1) * rows_q]), va_ones[hd], states[hd]))
        for hd in range(2):
            s = _dot_nt(qb[hd], kb_ref[rows, hd * LANES:(hd + 1) * LANES])
            new.append(_softmax_step(mask(s), vb_ones[hd], states[2 + hd]))
        return tuple(new)

    tq2 = 2 * tq
    qa_all = [_moba_queries(qa_ref[t * tq2:(t + 1) * tq2, :], kbar, t, n_blocks) for t in range(seq // tq2)]
    for t in range(seq // tq2):
        qrows = slice(t * tq2, (t + 1) * tq2)
        qa = qa_all[t]
        qb = [qb_ref[qrows, hd * LANES:(hd + 1) * LANES] for hd in range(2)]
        states = (_softmax_init(tq2),) * 4
        for j in range(2 * t + 1):
            states = tile(qa, qb, j, states, t * tq2, j == 2 * t)
        low = tuple((m[tq:], acc[tq:]) for m, acc in states)
        low = tile([q[tq:] for q in qa], [q[tq:] for q in qb], 2 * t + 1, low, t * tq2 + tq, True)
        states = tuple((jnp.concatenate([m[:tq], ml], axis=0), jnp.concatenate([acc[:tq], accl], axis=0))
                       for (m, acc), (ml, accl) in zip(states, low))
        oa_ref[qrows, :] = _merge_pair(states[:2]).astype(oa_ref.dtype)
        ob_ref[qrows, :] = _merge_pair(states[2:]).astype(ob_ref.dtype)


def _attention(qa, ka, va, e_mat, qb, kb, vb):
    b, s, _ = qa.shape
    seq_spec = lambda w: pl.BlockSpec((None, s, w), lambda i, p: (i, 0, p))
    return pl.pallas_call(
        _attn_body,
        out_shape=[jax.ShapeDtypeStruct((b, s, A_WIDTH), BF16)] * 2,
        grid=(b, N_PAIRS),
        in_specs=[seq_spec(LANES), seq_spec(LANES), seq_spec(LANES), _resident(e_mat.shape),
                  seq_spec(2 * LANES), seq_spec(2 * LANES), seq_spec(LANES)],
        out_specs=[seq_spec(LANES)] * 2,
        compiler_params=_params(2),
        name="attention",
    )(qa, ka, va, e_mat, qb, kb, vb)


def _mixer_out_body(layer_ref, x_ref, oa_ref, ob_ref, kv_ref, gm_ref, wg_ref, wa_ref, wb_ref, wo_ref,
                    gx_ref, wxq_ref, wxo_ref, o_ref):
    x = x_ref[...]
    d = x.shape[-1]
    h = _rms(x, gm_ref[...]).astype(BF16)
    merged = (jax.nn.sigmoid(_dot(h, wg_ref[:, :d])) * _dot(oa_ref[...], wa_ref[...])
              + jax.nn.sigmoid(_dot(h, wg_ref[:, d:])) * _dot(ob_ref[...], wb_ref[...]))
    x1 = x + _dot(merged.astype(BF16), wo_ref[...])

    q = _dot(_rms(x1, gx_ref[...]).astype(BF16), wxq_ref[...]).astype(BF16)
    scale = X_HEAD_DIM ** -0.5
    heads = []
    for hd in range(X_HEADS):
        lanes = slice(hd * X_HEAD_DIM, (hd + 1) * X_HEAD_DIM)
        s = _dot_nt(q[:, lanes], kv_ref[:, lanes])
        m = jnp.max(s, axis=-1, keepdims=True)
        p = jnp.exp((s - m) * scale)
        l = jnp.sum(p, axis=-1, keepdims=True)
        v = kv_ref[:, X_WIDTH + hd * X_HEAD_DIM:X_WIDTH + (hd + 1) * X_HEAD_DIM]
        heads.append((_dot(p.astype(BF16), v) * (1.0 / l)).astype(BF16))
    o_ref[...] = x1 + _dot(jnp.concatenate(heads, axis=1), wxo_ref[...])


def _mixer_out(layer, x, seq, oa, ob, mem_kv, norm_mix, w_gates, w_a, w_b, w_out, norm_xattn, w_xq, w_xo):
    t, d = x.shape
    tm = min(TOK_TILE, seq)
    tiles_per_seq = seq // tm
    m = mem_kv.shape[2]
    tok = lambda w: pl.BlockSpec((tm, w), lambda i, l: (i, 0))
    weights = (norm_mix, w_gates, w_a, w_b, w_out, norm_xattn, w_xq, w_xo)
    return pl.pallas_call(
        _mixer_out_body,
        out_shape=jax.ShapeDtypeStruct((t, d), F32),
        grid_spec=_layer_grid(
            (t // tm,),
            [tok(d), tok(A_WIDTH), tok(A_WIDTH),
             pl.BlockSpec((None, None, m, 2 * X_WIDTH), lambda i, l: (l[0], i // tiles_per_seq, 0, 0))]
            + [_layer_resident(w.shape) for w in weights],
            tok(d)),
        compiler_params=_params(1),
        name="mixer_out",
    )(layer, x, oa, ob, mem_kv, *weights)


FF_CHUNK = 1024


def _ffn_body(layer_ref, x_ref, g_ref, wgu_ref, wd_ref, gf_ref, o_ref, *, final):
    x = x_ref[...]
    h = _rms(x, g_ref[...]).astype(BF16)
    d_ff = wd_ref.shape[0]
    y = x
    for c0 in range(0, d_ff, FF_CHUNK):
        c1 = min(c0 + FF_CHUNK, d_ff)
        gt = _dot(h, wgu_ref[:, c0:c1])
        up = _dot(h, wgu_ref[:, d_ff + c0:d_ff + c1])
        a = (gt * jax.nn.sigmoid(gt) * up).astype(BF16)
        y = y + _dot(a, wd_ref[c0:c1, :])
    o_ref[...] = _rms(y, gf_ref[...]) if final else y


def _ffn(layer, x, norm_ffn, w_gate_up, w_down, norm_final, final):
    t, d = x.shape
    tm = min(TOK_TILE, t)
    tok = pl.BlockSpec((tm, d), lambda i, l: (i, 0))
    weights = (norm_ffn, w_gate_up, w_down)
    return pl.pallas_call(
        functools.partial(_ffn_body, final=final),
        out_shape=jax.ShapeDtypeStruct((t, d), F32),
        grid_spec=_layer_grid((t // tm,), [tok] + [_layer_resident(w.shape) for w in weights]
                              + [_resident(norm_final.shape)], tok),
        compiler_params=_params(1),
        name="ffn_final" if final else "ffn",
    )(layer, x, *weights, norm_final)


PREP_ROWS = 256


def _prep_w_in_body(wt_ref, o_ref, og_ref):
    in_w, rows = wt_ref.shape
    k0 = _C_LAT + Q_LORA + KV_LORA
    n_gate_groups = (in_w - k0 - MLA_ROPE) // LANES
    half = MLA_ROPE // 2
    lane = lax.broadcasted_iota(jnp.int32, (rows, LANES), 1)

    def cols(c0):
        n = min(LANES, in_w - c0)
        blk = wt_ref[c0:c0 + n, :]
        if n < LANES:
            blk = jnp.concatenate([blk, jnp.zeros((LANES - n, rows), blk.dtype)], axis=0)
        return blk.T

    for j in range(k0 // LANES):
        o_ref[:, j * LANES:(j + 1) * LANES] = cols(j * LANES).astype(BF16)
    g = cols(k0)
    kpe = jnp.where(lane < half, g, jnp.where((lane >= HALF_LANES) & (lane < HALF_LANES + half),
                                              pltpu.roll(g, HALF_LANES - half, axis=1), 0.0))
    o_ref[:, k0:k0 + LANES] = kpe.astype(BF16)
    a = g
    for j in range(n_gate_groups):
        b = cols(k0 + (j + 1) * LANES)
        shifted = pltpu.roll(jnp.where(lane >= MLA_ROPE, a, b), LANES - MLA_ROPE, axis=1)
        og_ref[:, j * LANES:(j + 1) * LANES] = shifted.astype(BF16)
        a = b


def _prep_w_in(w_in):
    n_layers, d, in_w = w_in.shape
    tr = min(PREP_ROWS, d)
    blk = lambda w: pl.BlockSpec((None, tr, w), lambda l, i: (l, i, 0))
    return pl.pallas_call(
        _prep_w_in_body,
        out_shape=[jax.ShapeDtypeStruct((n_layers, d, _C_G), BF16),
                   jax.ShapeDtypeStruct((n_layers, d, in_w - _C_G + LANES - MLA_ROPE), BF16)],
        grid=(n_layers, d // tr),
        in_specs=[pl.BlockSpec((None, in_w, tr), lambda l, i: (l, 0, i))],
        out_specs=[blk(_C_G), blk(in_w - _C_G + LANES - MLA_ROPE)],
        compiler_params=_params(2),
        name="prep_w_in",
    )(jnp.swapaxes(w_in, 1, 2))


def _mla_head_cols(nope, rope):
    half = MLA_ROPE // 2
    z = lambda n: jnp.zeros(nope.shape[:-1] + (n,), nope.dtype)
    t1, t2 = (z(half), z(half)) if rope is None else (rope[..., :half], rope[..., half:])
    return jnp.concatenate([t1, nope[..., :HEAD_DIM - half], t2, nope[..., HEAD_DIM - half:], z(LANES - MLA_QK)], axis=-1)


def _prep_weights(w_in, w_uq, w_ukv):
    n_layers = w_in.shape[0]
    w_in_r, w_gates = _prep_w_in(w_in)
    uq = w_uq.reshape(n_layers, Q_LORA, N_HEADS, MLA_QK)
    w_uq_r = _mla_head_cols(uq[..., :HEAD_DIM], uq[..., HEAD_DIM:]).reshape(n_layers, Q_LORA, _KB_W).astype(BF16)
    ukv = w_ukv.reshape(n_layers, KV_LORA, N_HEADS, 2 * HEAD_DIM)
    w_k = _mla_head_cols(ukv[..., :HEAD_DIM], None).reshape(n_layers, KV_LORA, _KB_W)
    w_v = ukv[..., HEAD_DIM:].reshape(n_layers, KV_LORA, A_WIDTH)
    w_ukv_r = jnp.concatenate([w_k, w_v], axis=-1).astype(BF16)
    return w_in_r, w_gates, w_uq_r, w_ukv_r


def kernel(x, mem, positions, norm_mix, w_in, q_lat_norm, w_uq, kv_lat_norm, w_ukv, w_branch_a, w_branch_b, w_out, norm_xattn, norm_mem, w_xq, w_xkv, w_xo, norm_ffn, w_gate_up, w_down, norm_final):
    b, s, d = x.shape
    n_layers = w_in.shape[0]
    t = b * s
    assert s % (2 * ATT_TILE) == 0 and ATT_TILE % MOBA_BLOCK == 0 and s // MOBA_BLOCK <= MAX_BLOCKS

    w_in_r, w_gates, w_uq_r, w_ukv_r = _prep_weights(w_in, w_uq, w_ukv)
    bf = lambda w: w.astype(BF16)
    row = lambda g: g.reshape(g.shape[0], 1, g.shape[1])
    tabs = _rope_tables(positions)
    mem_kv = _mem_kv(mem, row(norm_mem), bf(w_xkv))
    blk_of_row = np.arange(s) // MOBA_BLOCK
    e_mat = jnp.asarray(np.where(blk_of_row[:, None] == np.arange(LANES)[None, :], NEG_BIG, 0.0), BF16)
    norm_final = norm_final.reshape(1, d)

    norm_mix, q_lat_norm, kv_lat_norm = row(norm_mix), row(q_lat_norm), row(kv_lat_norm)
    norm_xattn, norm_ffn = row(norm_xattn), row(norm_ffn)
    w_a, w_b, w_o, w_q, w_xo_b = bf(w_branch_a), bf(w_branch_b), bf(w_out), bf(w_xq), bf(w_xo)
    w_gu, w_dn = bf(w_gate_up), bf(w_down)
    seq3 = lambda a: a.reshape(b, s, a.shape[-1])

    def layer(li, xt, final):
        li = jnp.full((1,), li, jnp.int32)
        qa, ka, va, qb, kb, vb = _mixer_in(li, xt, norm_mix, w_in_r, q_lat_norm, w_uq_r,
                                           kv_lat_norm, w_ukv_r, tabs)
        oa, ob = _attention(seq3(qa), seq3(ka), seq3(va), e_mat, seq3(qb), seq3(kb), seq3(vb))
        oa, ob = oa.reshape(t, A_WIDTH), ob.reshape(t, A_WIDTH)
        xt = _mixer_out(li, xt, s, oa, ob, mem_kv, norm_mix, w_gates, w_a, w_b, w_o, norm_xattn, w_q, w_xo_b)
        return _ffn(li, xt, norm_ffn, w_gu, w_dn, norm_final, final)

    xt = x.reshape(t, d)
    for li in range(n_layers):
        xt = layer(li, xt, li == n_layers - 1)
    return xt.reshape(b, s, d)
```

```python
import functools
import math

import numpy as np
import jax
import jax.numpy as jnp
from jax import lax
from jax.experimental import pallas as pl
from jax.experimental.pallas import tpu as pltpu

F32 = jnp.float32
BF16 = jnp.bfloat16

EPS = 1e-6
ROPE_THETA = 10000.0
LANES = 128
HALF_LANES = LANES // 2
HEAD_DIM = 64
N_HEADS = 8
N_PAIRS = N_HEADS // 2
A_WIDTH = N_HEADS * HEAD_DIM
MLA_ROPE = 32
MLA_QK = HEAD_DIM + MLA_ROPE
Q_LORA = 384
KV_LORA = 256
MOBA_BLOCK = 256
MOBA_TOPK = 3
MAX_BLOCKS = 16
X_HEADS = 4
X_HEAD_DIM = 128
X_WIDTH = X_HEADS * X_HEAD_DIM
NEG_BIG = -1e30
LOG2E = math.log2(math.e)
VMEM_LIMIT = 56 * 1024 * 1024
TOK_TILE = 1024
ATT_TILE = 512
TAB_W = 5 * LANES


def _resident(shape):
    return pl.BlockSpec(shape, lambda *_: (0,) * len(shape), pipeline_mode=pl.Buffered(1))


def _layer_resident(shape):
    return pl.BlockSpec((None,) + tuple(shape[1:]), lambda *a: (a[-1][0],) + (0,) * (len(shape) - 1),
                        pipeline_mode=pl.Buffered(1))


def _layer_grid(grid, in_specs, out_specs):
    return pltpu.PrefetchScalarGridSpec(num_scalar_prefetch=1, grid=grid, in_specs=in_specs, out_specs=out_specs)


def _params(n_axes):
    return pltpu.CompilerParams(dimension_semantics=("arbitrary",) * n_axes,
                                vmem_limit_bytes=VMEM_LIMIT)


def _rms(x, g):
    return x * lax.rsqrt(jnp.mean(x * x, axis=-1, keepdims=True) + EPS) * g


def _dot(a, b):
    return jnp.dot(a, b, preferred_element_type=F32)


def _dot_nt(a, b):
    return lax.dot_general(a, b, (((1,), (1,)), ((), ())), preferred_element_type=F32)


def _rope_table_body(pos_ref, tab_ref):
    pos = pos_ref[...]
    lane = lax.broadcasted_iota(jnp.int32, pos.shape, 1)
    sub = lane % HALF_LANES
    qtr = HEAD_DIM // 2
    inv = jnp.where(sub < qtr,
                    jnp.exp((-math.log(ROPE_THETA) * (2.0 / HEAD_DIM)) * sub.astype(F32)),
                    jnp.exp((-math.log(ROPE_THETA) * (2.0 / MLA_ROPE)) * (sub - qtr).astype(F32)))
    ang = pos * inv
    cos2, sin2 = jnp.cos(ang), jnp.sin(ang)

    def spread(v):
        r1, r3 = pltpu.roll(v, qtr, axis=1), pltpu.roll(v, 3 * qtr, axis=1)
        tiled = jnp.where(lane < qtr, v, jnp.where(lane < 2 * qtr, r1, jnp.where(
            lane < 3 * qtr, pltpu.roll(v, 2 * qtr, axis=1), r3)))
        return tiled, jnp.where(lane < HALF_LANES, r3, r1)

    first_half = sub < qtr
    is_rope = sub < MLA_ROPE // 2
    for half in range(2):
        cos, sin = (cos2, sin2) if half == 0 else (pltpu.roll(cos2, HALF_LANES, axis=1),
                                                               pltpu.roll(sin2, HALF_LANES, axis=1))
        cos_a, cos_b = spread(cos)
        sin_a, sin_b = spread(sin)
        tab_ref[half, :, 0:LANES] = cos_a
        tab_ref[half, :, LANES:2 * LANES] = jnp.where(first_half, -sin_a, 0.0)
        tab_ref[half, :, 2 * LANES:3 * LANES] = jnp.where(first_half, 0.0, sin_a)
        tab_ref[half, :, 3 * LANES:4 * LANES] = jnp.where(is_rope, cos_b, 1.0)
        tab_ref[half, :, 4 * LANES:5 * LANES] = jnp.where(is_rope, jnp.where(lane < HALF_LANES, -sin_b, sin_b), 0.0)


def _rope_tables(positions):
    t = positions.size
    half = t // 2
    pos = positions.reshape(2, half, 1).astype(F32)
    pos2 = jnp.concatenate([jnp.broadcast_to(pos[0], (half, HALF_LANES)),
                            jnp.broadcast_to(pos[1], (half, HALF_LANES))], axis=1)
    tm = min(1024, half)
    tabs = pl.pallas_call(
        _rope_table_body,
        out_shape=jax.ShapeDtypeStruct((2, half, TAB_W), F32),
        grid=(half // tm,),
        in_specs=[pl.BlockSpec((tm, LANES), lambda i: (i, 0))],
        out_specs=pl.BlockSpec((2, tm, TAB_W), lambda i: (0, i, 0)),
        compiler_params=_params(1),
        name="rope_tables",
    )(pos2)
    return tabs.reshape(t, TAB_W)


MEM_BATCH = 8


def _mem_kv_body(mem_ref, g_ref, w_ref, o_ref):
    nb, m, d = mem_ref.shape
    h = _rms(mem_ref[...].reshape(nb * m, d), g_ref[...]).astype(BF16)
    o_ref[...] = _dot(h, w_ref[...]).astype(BF16).reshape(o_ref.shape)


def _mem_kv(mem, norm_mem, w_xkv):
    b, m, d = mem.shape
    n_layers = w_xkv.shape[0]
    nb = math.gcd(b, MEM_BATCH)
    return pl.pallas_call(
        _mem_kv_body,
        out_shape=jax.ShapeDtypeStruct((n_layers, b, m, 2 * X_WIDTH), BF16),
        grid=(n_layers, b // nb),
        in_specs=[pl.BlockSpec((nb, m, d), lambda l, i: (i, 0, 0)),
                  pl.BlockSpec((None, 1, d), lambda l, i: (l, 0, 0)),
                  pl.BlockSpec((None, d, 2 * X_WIDTH), lambda l, i: (l, 0, 0))],
        out_specs=pl.BlockSpec((None, nb, m, 2 * X_WIDTH), lambda l, i: (l, i, 0, 0)),
        compiler_params=_params(2),
        name="mem_kv",
    )(mem, norm_mem, w_xkv)


_C_QK = 0
_C_VA = 2 * A_WIDTH
_C_LAT = 3 * A_WIDTH
_C_G = _C_LAT + Q_LORA + KV_LORA + LANES
_KB_W = N_HEADS * LANES


def _mixer_in_body(layer_ref, x_ref, g_ref, win_ref, qg_ref, wuq_ref, kvg_ref, wukv_ref, tab_ref,
                   qa_ref, ka_ref, va_ref, qb_ref, kb_ref, vb_ref):
    h = _rms(x_ref[...], g_ref[...]).astype(BF16)
    cos_a, sa_lo, sa_hi = tab_ref[:, 0:LANES], tab_ref[:, LANES:2 * LANES], tab_ref[:, 2 * LANES:3 * LANES]
    cos_b, sin_b = tab_ref[:, 3 * LANES:4 * LANES], tab_ref[:, 4 * LANES:5 * LANES]

    def rope_a(y):
        return (y * cos_a + pltpu.roll(y, LANES - HEAD_DIM // 2, axis=1) * sa_lo
                + pltpu.roll(y, HEAD_DIM // 2, axis=1) * sa_hi)

    def rope_b(y):
        return y * cos_b + pltpu.roll(y, HALF_LANES, axis=1) * sin_b

    scale_a = HEAD_DIM ** -0.5 * LOG2E
    scale_b = MLA_QK ** -0.5 * LOG2E
    lat = _dot(h, win_ref[:, _C_LAT:_C_G])
    y = _dot(h, win_ref[:, _C_QK:_C_QK + 2 * A_WIDTH])
    q_lat = _rms(lat[:, 0:Q_LORA], qg_ref[...]).astype(BF16)
    kv_lat = _rms(lat[:, Q_LORA:Q_LORA + KV_LORA], kvg_ref[...]).astype(BF16)
    for p in range(N_PAIRS):
        sl = slice(p * LANES, (p + 1) * LANES)
        qa_ref[:, sl] = (rope_a(y[:, sl]) * scale_a).astype(BF16)
        ka_ref[:, sl] = rope_a(y[:, A_WIDTH + p * LANES:A_WIDTH + (p + 1) * LANES]).astype(BF16)
    yq = _dot(q_lat, wuq_ref[...])
    ykv = _dot(kv_lat, wukv_ref[...])
    va_ref[...] = _dot(h, win_ref[:, _C_VA:_C_VA + A_WIDTH]).astype(BF16)
    k_pe = rope_b(lat[:, Q_LORA + KV_LORA:])
    for hd in range(N_HEADS):
        sl = slice(hd * LANES, (hd + 1) * LANES)
        qb_ref[:, sl] = (rope_b(yq[:, sl]) * scale_b).astype(BF16)
        kb_ref[:, sl] = (ykv[:, sl] + k_pe).astype(BF16)
    vb_ref[...] = ykv[:, _KB_W:].astype(BF16)


def _mixer_in(layer, x, norm_mix, w_in, q_lat_norm, w_uq, kv_lat_norm, w_ukv, tabs):
    t, d = x.shape
    tm = min(TOK_TILE, t)
    tok = lambda w: pl.BlockSpec((tm, w), lambda i, l: (i, 0))
    widths = (A_WIDTH, A_WIDTH, A_WIDTH, _KB_W, _KB_W, A_WIDTH)
    weights = (norm_mix, w_in, q_lat_norm, w_uq, kv_lat_norm, w_ukv)
    return pl.pallas_call(
        _mixer_in_body,
        out_shape=[jax.ShapeDtypeStruct((t, w), BF16) for w in widths],
        grid_spec=_layer_grid((t // tm,), [tok(d)] + [_layer_resident(w.shape) for w in weights] + [tok(TAB_W)],
                              [tok(w) for w in widths]),
        compiler_params=_params(1),
        name="mixer_in",
    )(layer, x, *weights, tabs)


def _softmax_init(tq):
    return (jnp.full((tq, 1), NEG_BIG, F32), jnp.zeros((tq, LANES), F32))


def _softmax_step(s, v_ones, state):
    m, acc = state
    m_new = jnp.maximum(m, jnp.max(s, axis=-1, keepdims=True))
    p = jnp.exp2(s - m_new)
    acc_new = jnp.exp2(m - m_new) * acc + _dot(p.astype(BF16), v_ones)
    return m_new, acc_new


def _values_and_ones(v):
    lane = lax.broadcasted_iota(jnp.int32, v.shape, 1)
    one = jnp.ones_like(v)
    return jnp.where(lane < HEAD_DIM, v, one), jnp.where(lane < HEAD_DIM, one, v)


def _causal(s, q0, k0):
    row = lax.broadcasted_iota(jnp.int32, s.shape, 0) + q0
    col = lax.broadcasted_iota(jnp.int32, s.shape, 1) + k0
    return jnp.where(col <= row, s, NEG_BIG)


def _merge_pair(states):
    (_, acc0), (_, acc1) = states
    lane = lax.broadcasted_iota(jnp.int32, acc0.shape, 1)
    num = jnp.where(lane < HEAD_DIM, acc0, acc1)
    den = pltpu.roll(jnp.where(lane < HEAD_DIM, acc1, acc0), HEAD_DIM, axis=1)
    return num / den


def _moba_queries(q, kbar, qi, n_blocks):
    tq = q.shape[0]
    lane = lax.broadcasted_iota(jnp.int32, q.shape, 1)
    row = lax.broadcasted_iota(jnp.int32, (MAX_BLOCKS, tq), 0)
    col = lax.broadcasted_iota(jnp.int32, (MAX_BLOCKS, tq), 1)
    cur = qi * (tq // MOBA_BLOCK) + col // MOBA_BLOCK
    q_aug = []
    for hd in range(2):
        qh = jnp.where((lane < HEAD_DIM) == (hd == 0), q, jnp.zeros_like(q))
        gate = jnp.where(row < cur, _dot_nt(kbar, qh), -jnp.inf)
        rank = jnp.zeros(gate.shape, jnp.int32)
        for n in range(n_blocks):
            gn = gate[n:n + 1, :]
            beats = (gn > gate) | ((gn == gate) & (row > n))
            rank = rank + beats.astype(jnp.int32)
        keep = ((row < cur) & (rank < MOBA_TOPK)) | (row == cur)
        dropped = jnp.where(keep, 0.0, 1.0)
        dropped = jnp.concatenate([dropped, jnp.zeros((LANES - MAX_BLOCKS, tq), F32)], axis=0)
        q_aug.append(jnp.concatenate([qh, dropped.T.astype(BF16)], axis=1))
    return q_aug


def _attn_body(qa_ref, ka_ref, va_ref, e_ref, qb_ref, kb_ref, vb_ref, oa_ref, ob_ref):
    tq = ATT_TILE
    seq = ka_ref.shape[0]
    n_blocks = seq // MOBA_BLOCK
    kbar = [jnp.mean(ka_ref[n * MOBA_BLOCK:(n + 1) * MOBA_BLOCK, :].astype(F32), axis=0, keepdims=True)
            for n in range(n_blocks)]
    kbar = jnp.concatenate(kbar + [jnp.zeros((MAX_BLOCKS - n_blocks, LANES), F32)], axis=0).astype(BF16)

    def tile(qa, qb, j, states, q0, causal):
        rows = slice(j * tq, (j + 1) * tq)
        ka_aug = jnp.concatenate([ka_ref[rows, :], e_ref[rows, :]], axis=1)
        va_ones = _values_and_ones(va_ref[rows, :])
        vb_ones = _values_and_ones(vb_ref[rows, :])
        mask = functools.partial(_causal, q0=q0, k0=j * tq) if causal else (lambda s: s)
        rows_q = qa[0].shape[0]
        mla = []
        for hd in range(2):
            s = _dot_nt(qb[hd], kb_ref[rows, hd * LANES:(hd + 1) * LANES])
            mla.append(_softmax_step(mask(s), vb_ones[hd], states[2 + hd]))
        s_both = _dot_nt(jnp.concatenate(qa, axis=0), ka_aug)
        moba = [_softmax_step(mask(s_both[hd * rows_q:(hd + 1) * rows_q]), va_ones[hd], states[hd])
                for hd in range(2)]
        return tuple(moba + mla)

    tq2 = 2 * tq
    qa_all = [_moba_queries(qa_ref[t * tq2:(t + 1) * tq2, :], kbar, t, n_blocks) for t in range(seq // tq2)]
    for t in range(seq // tq2):
        qrows = slice(t * tq2, (t + 1) * tq2)
        qa = qa_all[t]
        qb = [qb_ref[qrows, hd * LANES:(hd + 1) * LANES] for hd in range(2)]
        states = (_softmax_init(tq2),) * 4
        for j in range(2 * t + 1):
            states = tile(qa, qb, j, states, t * tq2, j == 2 * t)
        low = tuple((m[tq:], acc[tq:]) for m, acc in states)
        low = tile([q[tq:] for q in qa], [q[tq:] for q in qb], 2 * t + 1, low, t * tq2 + tq, True)
        states = tuple((jnp.concatenate([m[:tq], ml], axis=0), jnp.concatenate([acc[:tq], accl], axis=0))
                       for (m, acc), (ml, accl) in zip(states, low))
        oa_ref[qrows, :] = _merge_pair(states[:2]).astype(oa_ref.dtype)
        ob_ref[qrows, :] = _merge_pair(states[2:]).astype(ob_ref.dtype)


def _attention(qa, ka, va, e_mat, qb, kb, vb):
    b, s, _ = qa.shape
    seq_spec = lambda w: pl.BlockSpec((None, s, w), lambda i, p: (i, 0, p))
    return pl.pallas_call(
        _attn_body,
        out_shape=[jax.ShapeDtypeStruct((b, s, A_WIDTH), BF16)] * 2,
        grid=(b, N_PAIRS),
        in_specs=[seq_spec(LANES), seq_spec(LANES), seq_spec(LANES), _resident(e_mat.shape),
                  seq_spec(2 * LANES), seq_spec(2 * LANES), seq_spec(LANES)],
        out_specs=[seq_spec(LANES)] * 2,
        compiler_params=_params(2),
        name="attention",
    )(qa, ka, va, e_mat, qb, kb, vb)


def _mixer_out_body(layer_ref, x_ref, oa_ref, ob_ref, kv_ref, gm_ref, wg_ref, wa_ref, wb_ref, wo_ref,
                    gx_ref, wxq_ref, wxo_ref, o_ref):
    x = x_ref[...]
    d = x.shape[-1]
    h = _rms(x, gm_ref[...]).astype(BF16)
    merged = (jax.nn.sigmoid(_dot(h, wg_ref[:, :d])) * _dot(oa_ref[...], wa_ref[...])
              + jax.nn.sigmoid(_dot(h, wg_ref[:, d:])) * _dot(ob_ref[...], wb_ref[...]))
    x1 = x + _dot(merged.astype(BF16), wo_ref[...])

    q = _dot(_rms(x1, gx_ref[...]).astype(BF16), wxq_ref[...]).astype(BF16)
    scale = X_HEAD_DIM ** -0.5
    heads = []
    for hd in range(X_HEADS):
        lanes = slice(hd * X_HEAD_DIM, (hd + 1) * X_HEAD_DIM)
        s = _dot_nt(q[:, lanes], kv_ref[:, lanes])
        m = jnp.max(s, axis=-1, keepdims=True)
        p = jnp.exp((s - m) * scale)
        l = jnp.sum(p, axis=-1, keepdims=True)
        v = kv_ref[:, X_WIDTH + hd * X_HEAD_DIM:X_WIDTH + (hd + 1) * X_HEAD_DIM]
        heads.append((_dot(p.astype(BF16), v) * (1.0 / l)).astype(BF16))
    o_ref[...] = x1 + _dot(jnp.concatenate(heads, axis=1), wxo_ref[...])


def _mixer_out(layer, x, seq, oa, ob, mem_kv, norm_mix, w_gates, w_a, w_b, w_out, norm_xattn, w_xq, w_xo):
    t, d = x.shape
    tm = min(TOK_TILE, seq)
    tiles_per_seq = seq // tm
    m = mem_kv.shape[2]
    tok = lambda w: pl.BlockSpec((tm, w), lambda i, l: (i, 0))
    weights = (norm_mix, w_gates, w_a, w_b, w_out, norm_xattn, w_xq, w_xo)
    return pl.pallas_call(
        _mixer_out_body,
        out_shape=jax.ShapeDtypeStruct((t, d), F32),
        grid_spec=_layer_grid(
            (t // tm,),
            [tok(d), tok(A_WIDTH), tok(A_WIDTH),
             pl.BlockSpec((None, None, m, 2 * X_WIDTH), lambda i, l: (l[0], i // tiles_per_seq, 0, 0))]
            + [_layer_resident(w.shape) for w in weights],
            tok(d)),
        compiler_params=_params(1),
        name="mixer_out",
    )(layer, x, oa, ob, mem_kv, *weights)


FF_CHUNK = 1024


def _ffn_body(layer_ref, x_ref, g_ref, wgu_ref, wd_ref, gf_ref, o_ref, *, final):
    x = x_ref[...]
    h = _rms(x, g_ref[...]).astype(BF16)
    d_ff = wd_ref.shape[0]
    y = x
    for c0 in range(0, d_ff, FF_CHUNK):
        c1 = min(c0 + FF_CHUNK, d_ff)
        gt = _dot(h, wgu_ref[:, c0:c1])
        up = _dot(h, wgu_ref[:, d_ff + c0:d_ff + c1])
        a = (gt * jax.nn.sigmoid(gt) * up).astype(BF16)
        y = y + _dot(a, wd_ref[c0:c1, :])
    o_ref[...] = _rms(y, gf_ref[...]) if final else y


def _ffn(layer, x, norm_ffn, w_gate_up, w_down, norm_final, final):
    t, d = x.shape
    tm = min(TOK_TILE, t)
    tok = pl.BlockSpec((tm, d), lambda i, l: (i, 0))
    weights = (norm_ffn, w_gate_up, w_down)
    return pl.pallas_call(
        functools.partial(_ffn_body, final=final),
        out_shape=jax.ShapeDtypeStruct((t, d), F32),
        grid_spec=_layer_grid((t // tm,), [tok] + [_layer_resident(w.shape) for w in weights]
                              + [_resident(norm_final.shape)], tok),
        compiler_params=_params(1),
        name="ffn_final" if final else "ffn",
    )(layer, x, *weights, norm_final)


PREP_ROWS = 256


def _prep_w_in_body(wt_ref, o_ref, og_ref):
    in_w, rows = wt_ref.shape
    k0 = _C_LAT + Q_LORA + KV_LORA
    n_gate_groups = (in_w - k0 - MLA_ROPE) // LANES
    half = MLA_ROPE // 2
    lane = lax.broadcasted_iota(jnp.int32, (rows, LANES), 1)

    def cols(c0):
        n = min(LANES, in_w - c0)
        blk = wt_ref[c0:c0 + n, :]
        if n < LANES:
            blk = jnp.concatenate([blk, jnp.zeros((LANES - n, rows), blk.dtype)], axis=0)
        return blk.T

    for j in range(k0 // LANES):
        o_ref[:, j * LANES:(j + 1) * LANES] = cols(j * LANES).astype(BF16)
    g = cols(k0)
    kpe = jnp.where(lane < half, g, jnp.where((lane >= HALF_LANES) & (lane < HALF_LANES + half),
                                              pltpu.roll(g, HALF_LANES - half, axis=1), 0.0))
    o_ref[:, k0:k0 + LANES] = kpe.astype(BF16)
    a = g
    for j in range(n_gate_groups):
        b = cols(k0 + (j + 1) * LANES)
        shifted = pltpu.roll(jnp.where(lane >= MLA_ROPE, a, b), LANES - MLA_ROPE, axis=1)
        og_ref[:, j * LANES:(j + 1) * LANES] = shifted.astype(BF16)
        a = b


def _prep_w_in(w_in):
    n_layers, d, in_w = w_in.shape
    tr = min(PREP_ROWS, d)
    blk = lambda w: pl.BlockSpec((None, tr, w), lambda l, i: (l, i, 0))
    return pl.pallas_call(
        _prep_w_in_body,
        out_shape=[jax.ShapeDtypeStruct((n_layers, d, _C_G), BF16),
                   jax.ShapeDtypeStruct((n_layers, d, in_w - _C_G + LANES - MLA_ROPE), BF16)],
        grid=(n_layers, d // tr),
        in_specs=[pl.BlockSpec((None, in_w, tr), lambda l, i: (l, 0, i))],
        out_specs=[blk(_C_G), blk(in_w - _C_G + LANES - MLA_ROPE)],
        compiler_params=_params(2),
        name="prep_w_in",
    )(jnp.swapaxes(w_in, 1, 2))


def _mla_head_cols(nope, rope):
    half = MLA_ROPE // 2
    z = lambda n: jnp.zeros(nope.shape[:-1] + (n,), nope.dtype)
    t1, t2 = (z(half), z(half)) if rope is None else (rope[..., :half], rope[..., half:])
    return jnp.concatenate([t1, nope[..., :HEAD_DIM - half], t2, nope[..., HEAD_DIM - half:], z(LANES - MLA_QK)], axis=-1)


def _prep_weights(w_in, w_uq, w_ukv):
    n_layers = w_in.shape[0]
    w_in_r, w_gates = _prep_w_in(w_in)
    uq = w_uq.reshape(n_layers, Q_LORA, N_HEADS, MLA_QK)
    w_uq_r = _mla_head_cols(uq[..., :HEAD_DIM], uq[..., HEAD_DIM:]).reshape(n_layers, Q_LORA, _KB_W).astype(BF16)
    ukv = w_ukv.reshape(n_layers, KV_LORA, N_HEADS, 2 * HEAD_DIM)
    w_k = _mla_head_cols(ukv[..., :HEAD_DIM], None).reshape(n_layers, KV_LORA, _KB_W)
    w_v = ukv[..., HEAD_DIM:].reshape(n_layers, KV_LORA, A_WIDTH)
    w_ukv_r = jnp.concatenate([w_k, w_v], axis=-1).astype(BF16)
    return w_in_r, w_gates, w_uq_r, w_ukv_r


def kernel(x, mem, positions, norm_mix, w_in, q_lat_norm, w_uq, kv_lat_norm, w_ukv, w_branch_a, w_branch_b, w_out, norm_xattn, norm_mem, w_xq, w_xkv, w_xo, norm_ffn, w_gate_up, w_down, norm_final):
    b, s, d = x.shape
    n_layers = w_in.shape[0]
    t = b * s
    assert s % (2 * ATT_TILE) == 0 and ATT_TILE % MOBA_BLOCK == 0 and s // MOBA_BLOCK <= MAX_BLOCKS

    w_in_r, w_gates, w_uq_r, w_ukv_r = _prep_weights(w_in, w_uq, w_ukv)
    bf = lambda w: w.astype(BF16)
    row = lambda g: g.reshape(g.shape[0], 1, g.shape[1])
    tabs = _rope_tables(positions)
    mem_kv = _mem_kv(mem, row(norm_mem), bf(w_xkv))
    blk_of_row = np.arange(s) // MOBA_BLOCK
    e_mat = jnp.asarray(np.where(blk_of_row[:, None] == np.arange(LANES)[None, :], NEG_BIG, 0.0), BF16)
    norm_final = norm_final.reshape(1, d)

    norm_mix, q_lat_norm, kv_lat_norm = row(norm_mix), row(q_lat_norm), row(kv_lat_norm)
    norm_xattn, norm_ffn = row(norm_xattn), row(norm_ffn)
    w_a, w_b, w_o, w_q, w_xo_b = bf(w_branch_a), bf(w_branch_b), bf(w_out), bf(w_xq), bf(w_xo)
    w_gu, w_dn = bf(w_gate_up), bf(w_down)
    seq3 = lambda a: a.reshape(b, s, a.shape[-1])

    def layer(li, xt, final):
        li = jnp.full((1,), li, jnp.int32)
        qa, ka, va, qb, kb, vb = _mixer_in(li, xt, norm_mix, w_in_r, q_lat_norm, w_uq_r,
                                           kv_lat_norm, w_ukv_r, tabs)
        oa, ob = _attention(seq3(qa), seq3(ka), seq3(va), e_mat, seq3(qb), seq3(kb), seq3(vb))
        oa, ob = oa.reshape(t, A_WIDTH), ob.reshape(t, A_WIDTH)
        xt = _mixer_out(li, xt, s, oa, ob, mem_kv, norm_mix, w_gates, w_a, w_b, w_o, norm_xattn, w_q, w_xo_b)
        return _ffn(li, xt, norm_ffn, w_gu, w_dn, norm_final, final)

    xt = x.reshape(t, d)
    for li in range(n_layers):
        xt = layer(li, xt, li == n_layers - 1)
    return xt.reshape(b, s, d)
```

```python
import functools
import math

import numpy as np
import jax
import jax.numpy as jnp
from jax import lax
from jax.experimental import pallas as pl
from jax.experimental.pallas import tpu as pltpu

F32 = jnp.float32
BF16 = jnp.bfloat16

EPS = 1e-6
ROPE_THETA = 10000.0
LANES = 128
HALF_LANES = LANES // 2
HEAD_DIM = 64
N_HEADS = 8
N_PAIRS = N_HEADS // 2
A_WIDTH = N_HEADS * HEAD_DIM
MLA_ROPE = 32
MLA_QK = HEAD_DIM + MLA_ROPE
Q_LORA = 384
KV_LORA = 256
MOBA_BLOCK = 256
MOBA_TOPK = 3
MAX_BLOCKS = 16
X_HEADS = 4
X_HEAD_DIM = 128
X_WIDTH = X_HEADS * X_HEAD_DIM
NEG_BIG = -1e30
LOG2E = math.log2(math.e)
VMEM_LIMIT = 56 * 1024 * 1024
TOK_TILE = 1024
PAIRS_PER_STEP = 2
ATT_TILE = 512
TAB_W = 5 * LANES


def _resident(shape):
    return pl.BlockSpec(shape, lambda *_: (0,) * len(shape), pipeline_mode=pl.Buffered(1))


def _layer_resident(shape):
    return pl.BlockSpec((None,) + tuple(shape[1:]), lambda *a: (a[-1][0],) + (0,) * (len(shape) - 1),
                        pipeline_mode=pl.Buffered(1))


def _layer_grid(grid, in_specs, out_specs):
    return pltpu.PrefetchScalarGridSpec(num_scalar_prefetch=1, grid=grid, in_specs=in_specs, out_specs=out_specs)


def _params(n_axes):
    return pltpu.CompilerParams(dimension_semantics=("arbitrary",) * n_axes,
                                vmem_limit_bytes=VMEM_LIMIT)


def _rms(x, g):
    return x * lax.rsqrt(jnp.mean(x * x, axis=-1, keepdims=True) + EPS) * g


def _dot(a, b):
    return jnp.dot(a, b, preferred_element_type=F32)


def _dot_nt(a, b):
    return lax.dot_general(a, b, (((1,), (1,)), ((), ())), preferred_element_type=F32)


def _rope_table_body(pos_ref, tab_ref):
    pos = pos_ref[...]
    lane = lax.broadcasted_iota(jnp.int32, pos.shape, 1)
    sub = lane % HALF_LANES
    qtr = HEAD_DIM // 2
    inv = jnp.where(sub < qtr,
                    jnp.exp((-math.log(ROPE_THETA) * (2.0 / HEAD_DIM)) * sub.astype(F32)),
                    jnp.exp((-math.log(ROPE_THETA) * (2.0 / MLA_ROPE)) * (sub - qtr).astype(F32)))
    ang = pos * inv
    cos2, sin2 = jnp.cos(ang), jnp.sin(ang)

    def spread(v):
        r1, r3 = pltpu.roll(v, qtr, axis=1), pltpu.roll(v, 3 * qtr, axis=1)
        tiled = jnp.where(lane < qtr, v, jnp.where(lane < 2 * qtr, r1, jnp.where(
            lane < 3 * qtr, pltpu.roll(v, 2 * qtr, axis=1), r3)))
        return tiled, jnp.where(lane < HALF_LANES, r3, r1)

    first_half = sub < qtr
    is_rope = sub < MLA_ROPE // 2
    for half in range(2):
        cos, sin = (cos2, sin2) if half == 0 else (pltpu.roll(cos2, HALF_LANES, axis=1),
                                                               pltpu.roll(sin2, HALF_LANES, axis=1))
        cos_a, cos_b = spread(cos)
        sin_a, sin_b = spread(sin)
        tab_ref[half, :, 0:LANES] = cos_a
        tab_ref[half, :, LANES:2 * LANES] = jnp.where(first_half, -sin_a, 0.0)
        tab_ref[half, :, 2 * LANES:3 * LANES] = jnp.where(first_half, 0.0, sin_a)
        tab_ref[half, :, 3 * LANES:4 * LANES] = jnp.where(is_rope, cos_b, 1.0)
        tab_ref[half, :, 4 * LANES:5 * LANES] = jnp.where(is_rope, jnp.where(lane < HALF_LANES, -sin_b, sin_b), 0.0)


def _rope_tables(positions):
    t = positions.size
    half = t // 2
    pos = positions.reshape(2, half, 1).astype(F32)
    pos2 = jnp.concatenate([jnp.broadcast_to(pos[0], (half, HALF_LANES)),
                            jnp.broadcast_to(pos[1], (half, HALF_LANES))], axis=1)
    tm = min(1024, half)
    tabs = pl.pallas_call(
        _rope_table_body,
        out_shape=jax.ShapeDtypeStruct((2, half, TAB_W), F32),
        grid=(half // tm,),
        in_specs=[pl.BlockSpec((tm, LANES), lambda i: (i, 0))],
        out_specs=pl.BlockSpec((2, tm, TAB_W), lambda i: (0, i, 0)),
        compiler_params=_params(1),
        name="rope_tables",
    )(pos2)
    return tabs.reshape(t, TAB_W)


MEM_BATCH = 8


def _mem_kv_body(mem_ref, g_ref, w_ref, o_ref):
    nb, m, d = mem_ref.shape
    h = _rms(mem_ref[...].reshape(nb * m, d), g_ref[...]).astype(BF16)
    o_ref[...] = _dot(h, w_ref[...]).astype(BF16).reshape(o_ref.shape)


def _mem_kv(mem, norm_mem, w_xkv):
    b, m, d = mem.shape
    n_layers = w_xkv.shape[0]
    nb = math.gcd(b, MEM_BATCH)
    return pl.pallas_call(
        _mem_kv_body,
        out_shape=jax.ShapeDtypeStruct((n_layers, b, m, 2 * X_WIDTH), BF16),
        grid=(n_layers, b // nb),
        in_specs=[pl.BlockSpec((nb, m, d), lambda l, i: (i, 0, 0)),
                  pl.BlockSpec((None, 1, d), lambda l, i: (l, 0, 0)),
                  pl.BlockSpec((None, d, 2 * X_WIDTH), lambda l, i: (l, 0, 0))],
        out_specs=pl.BlockSpec((None, nb, m, 2 * X_WIDTH), lambda l, i: (l, i, 0, 0)),
        compiler_params=_params(2),
        name="mem_kv",
    )(mem, norm_mem, w_xkv)


_C_QK = 0
_C_VA = 2 * A_WIDTH
_C_LAT = 3 * A_WIDTH
_C_G = _C_LAT + Q_LORA + KV_LORA + LANES
_KB_W = N_HEADS * LANES


def _mixer_in_body(layer_ref, x_ref, g_ref, win_ref, qg_ref, wuq_ref, kvg_ref, wukv_ref, tab_ref,
                   qa_ref, ka_ref, va_ref, qb_ref, kb_ref, vb_ref):
    h = _rms(x_ref[...], g_ref[...]).astype(BF16)
    cos_a, sa_lo, sa_hi = tab_ref[:, 0:LANES], tab_ref[:, LANES:2 * LANES], tab_ref[:, 2 * LANES:3 * LANES]
    cos_b, sin_b = tab_ref[:, 3 * LANES:4 * LANES], tab_ref[:, 4 * LANES:5 * LANES]

    def rope_a(y):
        return (y * cos_a + pltpu.roll(y, LANES - HEAD_DIM // 2, axis=1) * sa_lo
                + pltpu.roll(y, HEAD_DIM // 2, axis=1) * sa_hi)

    def rope_b(y):
        return y * cos_b + pltpu.roll(y, HALF_LANES, axis=1) * sin_b

    scale_a = HEAD_DIM ** -0.5 * LOG2E
    scale_b = MLA_QK ** -0.5 * LOG2E
    lat = _dot(h, win_ref[:, _C_LAT:_C_G])
    y = _dot(h, win_ref[:, _C_QK:_C_QK + 2 * A_WIDTH])
    q_lat = _rms(lat[:, 0:Q_LORA], qg_ref[...]).astype(BF16)
    kv_lat = _rms(lat[:, Q_LORA:Q_LORA + KV_LORA], kvg_ref[...]).astype(BF16)
    for p in range(N_PAIRS):
        sl = slice(p * LANES, (p + 1) * LANES)
        qa_ref[:, sl] = (rope_a(y[:, sl]) * scale_a).astype(BF16)
        ka_ref[:, sl] = rope_a(y[:, A_WIDTH + p * LANES:A_WIDTH + (p + 1) * LANES]).astype(BF16)
    yq = _dot(q_lat, wuq_ref[...])
    ykv = _dot(kv_lat, wukv_ref[...])
    va_ref[...] = _dot(h, win_ref[:, _C_VA:_C_VA + A_WIDTH]).astype(BF16)
    k_pe = rope_b(lat[:, Q_LORA + KV_LORA:])
    for hd in range(N_HEADS):
        sl = slice(hd * LANES, (hd + 1) * LANES)
        qb_ref[:, sl] = (rope_b(yq[:, sl]) * scale_b).astype(BF16)
        kb_ref[:, sl] = (ykv[:, sl] + k_pe).astype(BF16)
    vb_ref[...] = ykv[:, _KB_W:].astype(BF16)


def _mixer_in(layer, x, norm_mix, w_in, q_lat_norm, w_uq, kv_lat_norm, w_ukv, tabs):
    t, d = x.shape
    tm = min(TOK_TILE, t)
    tok = lambda w: pl.BlockSpec((tm, w), lambda i, l: (i, 0))
    widths = (A_WIDTH, A_WIDTH, A_WIDTH, _KB_W, _KB_W, A_WIDTH)
    weights = (norm_mix, w_in, q_lat_norm, w_uq, kv_lat_norm, w_ukv)
    return pl.pallas_call(
        _mixer_in_body,
        out_shape=[jax.ShapeDtypeStruct((t, w), BF16) for w in widths],
        grid_spec=_layer_grid((t // tm,), [tok(d)] + [_layer_resident(w.shape) for w in weights] + [tok(TAB_W)],
                              [tok(w) for w in widths]),
        compiler_params=_params(1),
        name="mixer_in",
    )(layer, x, *weights, tabs)


def _softmax_init(tq):
    return (jnp.full((tq, 1), NEG_BIG, F32), jnp.zeros((tq, LANES), F32))


def _softmax_step(s, v_ones, state):
    m, acc = state
    m_new = jnp.maximum(m, jnp.max(s, axis=-1, keepdims=True))
    p = jnp.exp2(s - m_new)
    acc_new = jnp.exp2(m - m_new) * acc + _dot(p.astype(BF16), v_ones)
    return m_new, acc_new


def _values_and_ones(v):
    lane = lax.broadcasted_iota(jnp.int32, v.shape, 1)
    one = jnp.ones_like(v)
    return jnp.where(lane < HEAD_DIM, v, one), jnp.where(lane < HEAD_DIM, one, v)


def _causal(s, q0, k0):
    row = lax.broadcasted_iota(jnp.int32, s.shape, 0) + q0
    col = lax.broadcasted_iota(jnp.int32, s.shape, 1) + k0
    return jnp.where(col <= row, s, NEG_BIG)


def _merge_pair(states):
    (_, acc0), (_, acc1) = states
    lane = lax.broadcasted_iota(jnp.int32, acc0.shape, 1)
    num = jnp.where(lane < HEAD_DIM, acc0, acc1)
    den = pltpu.roll(jnp.where(lane < HEAD_DIM, acc1, acc0), HEAD_DIM, axis=1)
    return num / den


def _moba_queries(q, kbar, qi, n_blocks):
    tq = q.shape[0]
    lane = lax.broadcasted_iota(jnp.int32, q.shape, 1)
    row = lax.broadcasted_iota(jnp.int32, (MAX_BLOCKS, tq), 0)
    col = lax.broadcasted_iota(jnp.int32, (MAX_BLOCKS, tq), 1)
    cur = qi * (tq // MOBA_BLOCK) + col // MOBA_BLOCK
    q_aug = []
    for hd in range(2):
        qh = jnp.where((lane < HEAD_DIM) == (hd == 0), q, jnp.zeros_like(q))
        gate = jnp.where(row < cur, _dot_nt(kbar, qh), -jnp.inf)
        rank = jnp.zeros(gate.shape, jnp.int32)
        for n in range(n_blocks):
            gn = gate[n:n + 1, :]
            beats = (gn > gate) | ((gn == gate) & (row > n))
            rank = rank + beats.astype(jnp.int32)
        keep = ((row < cur) & (rank < MOBA_TOPK)) | (row == cur)
        dropped = jnp.where(keep, 0.0, 1.0)
        dropped = jnp.concatenate([dropped, jnp.zeros((LANES - MAX_BLOCKS, tq), F32)], axis=0)
        q_aug.append(jnp.concatenate([qh, dropped.T.astype(BF16)], axis=1))
    return q_aug


def _attn_body(qa_ref, ka_ref, va_ref, e_ref, qb_ref, kb_ref, vb_ref, oa_ref, ob_ref):
    for pp in range(PAIRS_PER_STEP):
        pair = lambda ref, w=LANES: ref.at[:, pp * w:(pp + 1) * w]
        _attn_pair(pair(qa_ref), pair(ka_ref), pair(va_ref), e_ref, pair(qb_ref, 2 * LANES), pair(kb_ref, 2 * LANES),
                   pair(vb_ref), pair(oa_ref), pair(ob_ref))


def _attn_pair(qa_ref, ka_ref, va_ref, e_ref, qb_ref, kb_ref, vb_ref, oa_ref, ob_ref):
    tq = ATT_TILE
    seq = ka_ref.shape[0]
    n_blocks = seq // MOBA_BLOCK
    kbar = [jnp.mean(ka_ref[n * MOBA_BLOCK:(n + 1) * MOBA_BLOCK, :].astype(F32), axis=0, keepdims=True)
            for n in range(n_blocks)]
    kbar = jnp.concatenate(kbar + [jnp.zeros((MAX_BLOCKS - n_blocks, LANES), F32)], axis=0).astype(BF16)

    def tile(qa, qb, j, states, q0, causal):
        rows = slice(j * tq, (j + 1) * tq)
        ka_aug = jnp.concatenate([ka_ref[rows, :], e_ref[rows, :]], axis=1)
        va_ones = _values_and_ones(va_ref[rows, :])
        vb_ones = _values_and_ones(vb_ref[rows, :])
        mask = functools.partial(_causal, q0=q0, k0=j * tq) if causal else (lambda s: s)
        rows_q = qa[0].shape[0]
        mla = []
        for hd in range(2):
            s = _dot_nt(qb[hd], kb_ref[rows, hd * LANES:(hd + 1) * LANES])
            mla.append(_softmax_step(mask(s), vb_ones[hd], states[2 + hd]))
        s_both = _dot_nt(jnp.concatenate(qa, axis=0), ka_aug)
        moba = [_softmax_step(mask(s_both[hd * rows_q:(hd + 1) * rows_q]), va_ones[hd], states[hd])
                for hd in range(2)]
        return tuple(moba + mla)

    tq2 = 2 * tq
    qa_all = [_moba_queries(qa_ref[t * tq2:(t + 1) * tq2, :], kbar, t, n_blocks) for t in range(seq // tq2)]
    for t in range(seq // tq2):
        qrows = slice(t * tq2, (t + 1) * tq2)
        qa = qa_all[t]
        qb = [qb_ref[qrows, hd * LANES:(hd + 1) * LANES] for hd in range(2)]
        states = (_softmax_init(tq2),) * 4
        for j in range(2 * t + 1):
            states = tile(qa, qb, j, states, t * tq2, j == 2 * t)
        low = tuple((m[tq:], acc[tq:]) for m, acc in states)
        low = tile([q[tq:] for q in qa], [q[tq:] for q in qb], 2 * t + 1, low, t * tq2 + tq, True)
        states = tuple((jnp.concatenate([m[:tq], ml], axis=0), jnp.concatenate([acc[:tq], accl], axis=0))
                       for (m, acc), (ml, accl) in zip(states, low))
        oa_ref[qrows, :] = _merge_pair(states[:2]).astype(oa_ref.dtype)
        ob_ref[qrows, :] = _merge_pair(states[2:]).astype(ob_ref.dtype)


def _attention(qa, ka, va, e_mat, qb, kb, vb):
    b, s, _ = qa.shape
    n = PAIRS_PER_STEP
    seq_spec = lambda w: pl.BlockSpec((None, s, n * w), lambda i, p: (i, 0, p))
    return pl.pallas_call(
        _attn_body,
        out_shape=[jax.ShapeDtypeStruct((b, s, A_WIDTH), BF16)] * 2,
        grid=(b, N_PAIRS // n),
        in_specs=[seq_spec(LANES), seq_spec(LANES), seq_spec(LANES), _resident(e_mat.shape),
                  seq_spec(2 * LANES), seq_spec(2 * LANES), seq_spec(LANES)],
        out_specs=[seq_spec(LANES)] * 2,
        compiler_params=_params(2),
        name="attention",
    )(qa, ka, va, e_mat, qb, kb, vb)


def _mixer_out_body(layer_ref, x_ref, oa_ref, ob_ref, kv_ref, gm_ref, wg_ref, wa_ref, wb_ref, wo_ref,
                    gx_ref, wxq_ref, wxo_ref, o_ref):
    x = x_ref[...]
    d = x.shape[-1]
    h = _rms(x, gm_ref[...]).astype(BF16)
    merged = (jax.nn.sigmoid(_dot(h, wg_ref[:, :d])) * _dot(oa_ref[...], wa_ref[...])
              + jax.nn.sigmoid(_dot(h, wg_ref[:, d:])) * _dot(ob_ref[...], wb_ref[...]))
    x1 = x + _dot(merged.astype(BF16), wo_ref[...])

    q = _dot(_rms(x1, gx_ref[...]).astype(BF16), wxq_ref[...]).astype(BF16)
    scale = X_HEAD_DIM ** -0.5
    heads = []
    for hd in range(X_HEADS):
        lanes = slice(hd * X_HEAD_DIM, (hd + 1) * X_HEAD_DIM)
        s = _dot_nt(q[:, lanes], kv_ref[:, lanes])
        m = jnp.max(s, axis=-1, keepdims=True)
        p = jnp.exp((s - m) * scale)
        l = jnp.sum(p, axis=-1, keepdims=True)
        v = kv_ref[:, X_WIDTH + hd * X_HEAD_DIM:X_WIDTH + (hd + 1) * X_HEAD_DIM]
        heads.append((_dot(p.astype(BF16), v) * (1.0 / l)).astype(BF16))
    o_ref[...] = x1 + _dot(jnp.concatenate(heads, axis=1), wxo_ref[...])


def _mixer_out(layer, x, seq, oa, ob, mem_kv, norm_mix, w_gates, w_a, w_b, w_out, norm_xattn, w_xq, w_xo):
    t, d = x.shape
    tm = min(TOK_TILE, seq)
    tiles_per_seq = seq // tm
    m = mem_kv.shape[2]
    tok = lambda w: pl.BlockSpec((tm, w), lambda i, l: (i, 0))
    weights = (norm_mix, w_gates, w_a, w_b, w_out, norm_xattn, w_xq, w_xo)
    return pl.pallas_call(
        _mixer_out_body,
        out_shape=jax.ShapeDtypeStruct((t, d), F32),
        grid_spec=_layer_grid(
            (t // tm,),
            [tok(d), tok(A_WIDTH), tok(A_WIDTH),
             pl.BlockSpec((None, None, m, 2 * X_WIDTH), lambda i, l: (l[0], i // tiles_per_seq, 0, 0))]
            + [_layer_resident(w.shape) for w in weights],
            tok(d)),
        compiler_params=_params(1),
        name="mixer_out",
    )(layer, x, oa, ob, mem_kv, *weights)


FF_CHUNK = 1024


def _ffn_body(layer_ref, x_ref, g_ref, wgu_ref, wd_ref, gf_ref, o_ref, *, final):
    x = x_ref[...]
    h = _rms(x, g_ref[...]).astype(BF16)
    d_ff = wd_ref.shape[0]
    y = x
    for c0 in range(0, d_ff, FF_CHUNK):
        c1 = min(c0 + FF_CHUNK, d_ff)
        gt = _dot(h, wgu_ref[:, c0:c1])
        up = _dot(h, wgu_ref[:, d_ff + c0:d_ff + c1])
        a = (gt * jax.nn.sigmoid(gt) * up).astype(BF16)
        y = y + _dot(a, wd_ref[c0:c1, :])
    o_ref[...] = _rms(y, gf_ref[...]) if final else y


def _ffn(layer, x, norm_ffn, w_gate_up, w_down, norm_final, final):
    t, d = x.shape
    tm = min(TOK_TILE, t)
    tok = pl.BlockSpec((tm, d), lambda i, l: (i, 0))
    weights = (norm_ffn, w_gate_up, w_down)
    return pl.pallas_call(
        functools.partial(_ffn_body, final=final),
        out_shape=jax.ShapeDtypeStruct((t, d), F32),
        grid_spec=_layer_grid((t // tm,), [tok] + [_layer_resident(w.shape) for w in weights]
                              + [_resident(norm_final.shape)], tok),
        compiler_params=_params(1),
        name="ffn_final" if final else "ffn",
    )(layer, x, *weights, norm_final)


PREP_ROWS = 256


def _prep_w_in_body(wt_ref, o_ref, og_ref):
    in_w, rows = wt_ref.shape
    k0 = _C_LAT + Q_LORA + KV_LORA
    n_gate_groups = (in_w - k0 - MLA_ROPE) // LANES
    half = MLA_ROPE // 2
    lane = lax.broadcasted_iota(jnp.int32, (rows, LANES), 1)

    def cols(c0):
        n = min(LANES, in_w - c0)
        blk = wt_ref[c0:c0 + n, :]
        if n < LANES:
            blk = jnp.concatenate([blk, jnp.zeros((LANES - n, rows), blk.dtype)], axis=0)
        return blk.T

    for j in range(k0 // LANES):
        o_ref[:, j * LANES:(j + 1) * LANES] = cols(j * LANES).astype(BF16)
    g = cols(k0)
    kpe = jnp.where(lane < half, g, jnp.where((lane >= HALF_LANES) & (lane < HALF_LANES + half),
                                              pltpu.roll(g, HALF_LANES - half, axis=1), 0.0))
    o_ref[:, k0:k0 + LANES] = kpe.astype(BF16)
    a = g
    for j in range(n_gate_groups):
        b = cols(k0 + (j + 1) * LANES)
        shifted = pltpu.roll(jnp.where(lane >= MLA_ROPE, a, b), LANES - MLA_ROPE, axis=1)
        og_ref[:, j * LANES:(j + 1) * LANES] = shifted.astype(BF16)
        a = b


def _prep_w_in(w_in):
    n_layers, d, in_w = w_in.shape
    tr = min(PREP_ROWS, d)
    blk = lambda w: pl.BlockSpec((None, tr, w), lambda l, i: (l, i, 0))
    return pl.pallas_call(
        _prep_w_in_body,
        out_shape=[jax.ShapeDtypeStruct((n_layers, d, _C_G), BF16),
                   jax.ShapeDtypeStruct((n_layers, d, in_w - _C_G + LANES - MLA_ROPE), BF16)],
        grid=(n_layers, d // tr),
        in_specs=[pl.BlockSpec((None, in_w, tr), lambda l, i: (l, 0, i))],
        out_specs=[blk(_C_G), blk(in_w - _C_G + LANES - MLA_ROPE)],
        compiler_params=_params(2),
        name="prep_w_in",
    )(jnp.swapaxes(w_in, 1, 2))


def _mla_head_cols(nope, rope):
    half = MLA_ROPE // 2
    z = lambda n: jnp.zeros(nope.shape[:-1] + (n,), nope.dtype)
    t1, t2 = (z(half), z(half)) if rope is None else (rope[..., :half], rope[..., half:])
    return jnp.concatenate([t1, nope[..., :HEAD_DIM - half], t2, nope[..., HEAD_DIM - half:], z(LANES - MLA_QK)], axis=-1)


def _prep_weights(w_in, w_uq, w_ukv):
    n_layers = w_in.shape[0]
    w_in_r, w_gates = _prep_w_in(w_in)
    uq = w_uq.reshape(n_layers, Q_LORA, N_HEADS, MLA_QK)
    w_uq_r = _mla_head_cols(uq[..., :HEAD_DIM], uq[..., HEAD_DIM:]).reshape(n_layers, Q_LORA, _KB_W).astype(BF16)
    ukv = w_ukv.reshape(n_layers, KV_LORA, N_HEADS, 2 * HEAD_DIM)
    w_k = _mla_head_cols(ukv[..., :HEAD_DIM], None).reshape(n_layers, KV_LORA, _KB_W)
    w_v = ukv[..., HEAD_DIM:].reshape(n_layers, KV_LORA, A_WIDTH)
    w_ukv_r = jnp.concatenate([w_k, w_v], axis=-1).astype(BF16)
    return w_in_r, w_gates, w_uq_r, w_ukv_r


def kernel(x, mem, positions, norm_mix, w_in, q_lat_norm, w_uq, kv_lat_norm, w_ukv, w_branch_a, w_branch_b, w_out, norm_xattn, norm_mem, w_xq, w_xkv, w_xo, norm_ffn, w_gate_up, w_down, norm_final):
    b, s, d = x.shape
    n_layers = w_in.shape[0]
    t = b * s
    assert s % (2 * ATT_TILE) == 0 and ATT_TILE % MOBA_BLOCK == 0 and s // MOBA_BLOCK <= MAX_BLOCKS

    w_in_r, w_gates, w_uq_r, w_ukv_r = _prep_weights(w_in, w_uq, w_ukv)
    bf = lambda w: w.astype(BF16)
    row = lambda g: g.reshape(g.shape[0], 1, g.shape[1])
    tabs = _rope_tables(positions)
    mem_kv = _mem_kv(mem, row(norm_mem), bf(w_xkv))
    blk_of_row = np.arange(s) // MOBA_BLOCK
    e_mat = jnp.asarray(np.where(blk_of_row[:, None] == np.arange(LANES)[None, :], NEG_BIG, 0.0), BF16)
    norm_final = norm_final.reshape(1, d)

    norm_mix, q_lat_norm, kv_lat_norm = row(norm_mix), row(q_lat_norm), row(kv_lat_norm)
    norm_xattn, norm_ffn = row(norm_xattn), row(norm_ffn)
    w_a, w_b, w_o, w_q, w_xo_b = bf(w_branch_a), bf(w_branch_b), bf(w_out), bf(w_xq), bf(w_xo)
    w_gu, w_dn = bf(w_gate_up), bf(w_down)
    seq3 = lambda a: a.reshape(b, s, a.shape[-1])

    def layer(li, xt, final):
        li = jnp.full((1,), li, jnp.int32)
        qa, ka, va, qb, kb, vb = _mixer_in(li, xt, norm_mix, w_in_r, q_lat_norm, w_uq_r,
                                           kv_lat_norm, w_ukv_r, tabs)
        oa, ob = _attention(seq3(qa), seq3(ka), seq3(va), e_mat, seq3(qb), seq3(kb), seq3(vb))
        oa, ob = oa.reshape(t, A_WIDTH), ob.reshape(t, A_WIDTH)
        xt = _mixer_out(li, xt, s, oa, ob, mem_kv, norm_mix, w_gates, w_a, w_b, w_o, norm_xattn, w_q, w_xo_b)
        return _ffn(li, xt, norm_ffn, w_gu, w_dn, norm_final, final)

    xt = x.reshape(t, d)
    for li in range(n_layers):
        xt = layer(li, xt, li == n_layers - 1)
    return xt.reshape(b, s, d)
```

```python
import functools
import math

import numpy as np
import jax
import jax.numpy as jnp
from jax import lax
from jax.experimental import pallas as pl
from jax.experimental.pallas import tpu as pltpu

F32 = jnp.float32
BF16 = jnp.bfloat16

EPS = 1e-6
ROPE_THETA = 10000.0
LANES = 128
HALF_LANES = LANES // 2
HEAD_DIM = 64
N_HEADS = 8
N_PAIRS = N_HEADS // 2
A_WIDTH = N_HEADS * HEAD_DIM
MLA_ROPE = 32
MLA_QK = HEAD_DIM + MLA_ROPE
Q_LORA = 384
KV_LORA = 256
MOBA_BLOCK = 256
MOBA_TOPK = 3
MAX_BLOCKS = 16
X_HEADS = 4
X_HEAD_DIM = 128
X_WIDTH = X_HEADS * X_HEAD_DIM
NEG_BIG = -1e30
LOG2E = math.log2(math.e)
VMEM_LIMIT = 56 * 1024 * 1024
TOK_TILE = 1024
ATT_TILE = 512
TAB_W = 5 * LANES


def _resident(shape):
    return pl.BlockSpec(shape, lambda *_: (0,) * len(shape), pipeline_mode=pl.Buffered(1))


def _layer_resident(shape):
    return pl.BlockSpec((None,) + tuple(shape[1:]), lambda *a: (a[-1][0],) + (0,) * (len(shape) - 1),
                        pipeline_mode=pl.Buffered(1))


def _layer_grid(grid, in_specs, out_specs):
    return pltpu.PrefetchScalarGridSpec(num_scalar_prefetch=1, grid=grid, in_specs=in_specs, out_specs=out_specs)


def _params(n_axes):
    return pltpu.CompilerParams(dimension_semantics=("arbitrary",) * n_axes,
                                vmem_limit_bytes=VMEM_LIMIT)


def _rms(x, g):
    return x * lax.rsqrt(jnp.mean(x * x, axis=-1, keepdims=True) + EPS) * g


def _dot(a, b):
    return jnp.dot(a, b, preferred_element_type=F32)


def _dot_nt(a, b):
    return lax.dot_general(a, b, (((1,), (1,)), ((), ())), preferred_element_type=F32)


def _rope_table_body(pos_ref, tab_ref):
    pos = pos_ref[...]
    lane = lax.broadcasted_iota(jnp.int32, pos.shape, 1)
    sub = lane % HALF_LANES
    qtr = HEAD_DIM // 2
    inv = jnp.where(sub < qtr,
                    jnp.exp((-math.log(ROPE_THETA) * (2.0 / HEAD_DIM)) * sub.astype(F32)),
                    jnp.exp((-math.log(ROPE_THETA) * (2.0 / MLA_ROPE)) * (sub - qtr).astype(F32)))
    ang = pos * inv
    cos2, sin2 = jnp.cos(ang), jnp.sin(ang)

    def spread(v):
        r1, r3 = pltpu.roll(v, qtr, axis=1), pltpu.roll(v, 3 * qtr, axis=1)
        tiled = jnp.where(lane < qtr, v, jnp.where(lane < 2 * qtr, r1, jnp.where(
            lane < 3 * qtr, pltpu.roll(v, 2 * qtr, axis=1), r3)))
        return tiled, jnp.where(lane < HALF_LANES, r3, r1)

    first_half = sub < qtr
    is_rope = sub < MLA_ROPE // 2
    for half in range(2):
        cos, sin = (cos2, sin2) if half == 0 else (pltpu.roll(cos2, HALF_LANES, axis=1),
                                                               pltpu.roll(sin2, HALF_LANES, axis=1))
        cos_a, cos_b = spread(cos)
        sin_a, sin_b = spread(sin)
        tab_ref[half, :, 0:LANES] = cos_a
        tab_ref[half, :, LANES:2 * LANES] = jnp.where(first_half, -sin_a, 0.0)
        tab_ref[half, :, 2 * LANES:3 * LANES] = jnp.where(first_half, 0.0, sin_a)
        tab_ref[half, :, 3 * LANES:4 * LANES] = jnp.where(is_rope, cos_b, 1.0)
        tab_ref[half, :, 4 * LANES:5 * LANES] = jnp.where(is_rope, jnp.where(lane < HALF_LANES, -sin_b, sin_b), 0.0)


def _rope_tables(positions):
    t = positions.size
    half = t // 2
    pos = positions.reshape(2, half, 1).astype(F32)
    pos2 = jnp.concatenate([jnp.broadcast_to(pos[0], (half, HALF_LANES)),
                            jnp.broadcast_to(pos[1], (half, HALF_LANES))], axis=1)
    tm = min(1024, half)
    tabs = pl.pallas_call(
        _rope_table_body,
        out_shape=jax.ShapeDtypeStruct((2, half, TAB_W), F32),
        grid=(half // tm,),
        in_specs=[pl.BlockSpec((tm, LANES), lambda i: (i, 0))],
        out_specs=pl.BlockSpec((2, tm, TAB_W), lambda i: (0, i, 0)),
        compiler_params=_params(1),
        name="rope_tables",
    )(pos2)
    return tabs.reshape(t, TAB_W)


MEM_BATCH = 8


def _mem_kv_body(mem_ref, g_ref, w_ref, o_ref):
    nb, m, d = mem_ref.shape
    h = _rms(mem_ref[...].reshape(nb * m, d), g_ref[...]).astype(BF16)
    o_ref[...] = _dot(h, w_ref[...]).astype(BF16).reshape(o_ref.shape)


def _mem_kv(mem, norm_mem, w_xkv):
    b, m, d = mem.shape
    n_layers = w_xkv.shape[0]
    nb = math.gcd(b, MEM_BATCH)
    return pl.pallas_call(
        _mem_kv_body,
        out_shape=jax.ShapeDtypeStruct((n_layers, b, m, 2 * X_WIDTH), BF16),
        grid=(n_layers, b // nb),
        in_specs=[pl.BlockSpec((nb, m, d), lambda l, i: (i, 0, 0)),
                  pl.BlockSpec((None, 1, d), lambda l, i: (l, 0, 0)),
                  pl.BlockSpec((None, d, 2 * X_WIDTH), lambda l, i: (l, 0, 0))],
        out_specs=pl.BlockSpec((None, nb, m, 2 * X_WIDTH), lambda l, i: (l, i, 0, 0)),
        compiler_params=_params(2),
        name="mem_kv",
    )(mem, norm_mem, w_xkv)


_C_QK = 0
_C_VA = 2 * A_WIDTH
_C_LAT = 3 * A_WIDTH
_C_G = _C_LAT + Q_LORA + KV_LORA + LANES
_KB_W = N_HEADS * LANES


def _mixer_in_body(layer_ref, x_ref, g_ref, win_ref, qg_ref, wuq_ref, kvg_ref, wukv_ref, tab_ref,
                   qa_ref, ka_ref, va_ref, qb_ref, kb_ref, vb_ref):
    h = _rms(x_ref[...], g_ref[...]).astype(BF16)
    cos_a, sa_lo, sa_hi = tab_ref[:, 0:LANES], tab_ref[:, LANES:2 * LANES], tab_ref[:, 2 * LANES:3 * LANES]
    cos_b, sin_b = tab_ref[:, 3 * LANES:4 * LANES], tab_ref[:, 4 * LANES:5 * LANES]

    def rope_a(y):
        return (y * cos_a + pltpu.roll(y, LANES - HEAD_DIM // 2, axis=1) * sa_lo
                + pltpu.roll(y, HEAD_DIM // 2, axis=1) * sa_hi)

    def rope_b(y):
        return y * cos_b + pltpu.roll(y, HALF_LANES, axis=1) * sin_b

    scale_a = HEAD_DIM ** -0.5 * LOG2E
    scale_b = MLA_QK ** -0.5 * LOG2E
    lat = _dot(h, win_ref[:, _C_LAT:_C_G])
    y = _dot(h, win_ref[:, _C_QK:_C_QK + 2 * A_WIDTH])
    q_lat = _rms(lat[:, 0:Q_LORA], qg_ref[...]).astype(BF16)
    kv_lat = _rms(lat[:, Q_LORA:Q_LORA + KV_LORA], kvg_ref[...]).astype(BF16)
    for p in range(N_PAIRS):
        sl = slice(p * LANES, (p + 1) * LANES)
        qa_ref[:, sl] = (rope_a(y[:, sl]) * scale_a).astype(BF16)
        ka_ref[:, sl] = rope_a(y[:, A_WIDTH + p * LANES:A_WIDTH + (p + 1) * LANES]).astype(BF16)
    yq = _dot(q_lat, wuq_ref[...])
    ykv = _dot(kv_lat, wukv_ref[...])
    va_ref[...] = _dot(h, win_ref[:, _C_VA:_C_VA + A_WIDTH]).astype(BF16)
    k_pe = rope_b(lat[:, Q_LORA + KV_LORA:])
    for hd in range(N_HEADS):
        sl = slice(hd * LANES, (hd + 1) * LANES)
        qb_ref[:, sl] = (rope_b(yq[:, sl]) * scale_b).astype(BF16)
        kb_ref[:, sl] = (ykv[:, sl] + k_pe).astype(BF16)
    vb_ref[...] = ykv[:, _KB_W:].astype(BF16)


def _mixer_in(layer, x, norm_mix, w_in, q_lat_norm, w_uq, kv_lat_norm, w_ukv, tabs):
    t, d = x.shape
    tm = min(TOK_TILE, t)
    tok = lambda w: pl.BlockSpec((tm, w), lambda i, l: (i, 0))
    widths = (A_WIDTH, A_WIDTH, A_WIDTH, _KB_W, _KB_W, A_WIDTH)
    weights = (norm_mix, w_in, q_lat_norm, w_uq, kv_lat_norm, w_ukv)
    return pl.pallas_call(
        _mixer_in_body,
        out_shape=[jax.ShapeDtypeStruct((t, w), BF16) for w in widths],
        grid_spec=_layer_grid((t // tm,), [tok(d)] + [_layer_resident(w.shape) for w in weights] + [tok(TAB_W)],
                              [tok(w) for w in widths]),
        compiler_params=_params(1),
        name="mixer_in",
    )(layer, x, *weights, tabs)


def _softmax_init(tq):
    return (jnp.full((tq, 1), NEG_BIG, F32), jnp.zeros((tq, LANES), F32))


def _softmax_step(s, v_ones, state):
    m, acc = state
    m_new = jnp.maximum(m, jnp.max(s, axis=-1, keepdims=True))
    p = jnp.exp2((s - m_new).astype(BF16))
    acc_new = jnp.exp2(m - m_new) * acc + _dot(p, v_ones)
    return m_new, acc_new


def _values_and_ones(v):
    lane = lax.broadcasted_iota(jnp.int32, v.shape, 1)
    one = jnp.ones_like(v)
    return jnp.where(lane < HEAD_DIM, v, one), jnp.where(lane < HEAD_DIM, one, v)


def _causal(s, q0, k0):
    row = lax.broadcasted_iota(jnp.int32, s.shape, 0) + q0
    col = lax.broadcasted_iota(jnp.int32, s.shape, 1) + k0
    return jnp.where(col <= row, s, NEG_BIG)


def _merge_pair(states):
    (_, acc0), (_, acc1) = states
    lane = lax.broadcasted_iota(jnp.int32, acc0.shape, 1)
    num = jnp.where(lane < HEAD_DIM, acc0, acc1)
    den = pltpu.roll(jnp.where(lane < HEAD_DIM, acc1, acc0), HEAD_DIM, axis=1)
    return num / den


def _moba_queries(q, kbar, qi, n_blocks):
    tq = q.shape[0]
    lane = lax.broadcasted_iota(jnp.int32, q.shape, 1)
    row = lax.broadcasted_iota(jnp.int32, (MAX_BLOCKS, tq), 0)
    col = lax.broadcasted_iota(jnp.int32, (MAX_BLOCKS, tq), 1)
    cur = qi * (tq // MOBA_BLOCK) + col // MOBA_BLOCK
    q_aug = []
    for hd in range(2):
        qh = jnp.where((lane < HEAD_DIM) == (hd == 0), q, jnp.zeros_like(q))
        gate = jnp.where(row < cur, _dot_nt(kbar, qh), -jnp.inf)
        rank = jnp.zeros(gate.shape, jnp.int32)
        for n in range(n_blocks):
            gn = gate[n:n + 1, :]
            beats = (gn > gate) | ((gn == gate) & (row > n))
            rank = rank + beats.astype(jnp.int32)
        keep = ((row < cur) & (rank < MOBA_TOPK)) | (row == cur)
        dropped = jnp.where(keep, 0.0, 1.0)
        dropped = jnp.concatenate([dropped, jnp.zeros((LANES - MAX_BLOCKS, tq), F32)], axis=0)
        q_aug.append(jnp.concatenate([qh, dropped.T.astype(BF16)], axis=1))
    return q_aug


def _attn_body(qa_ref, ka_ref, va_ref, e_ref, qb_ref, kb_ref, vb_ref, oa_ref, ob_ref):
    tq = ATT_TILE
    seq = ka_ref.shape[0]
    n_blocks = seq // MOBA_BLOCK
    kbar = [jnp.mean(ka_ref[n * MOBA_BLOCK:(n + 1) * MOBA_BLOCK, :].astype(F32), axis=0, keepdims=True)
            for n in range(n_blocks)]
    kbar = jnp.concatenate(kbar + [jnp.zeros((MAX_BLOCKS - n_blocks, LANES), F32)], axis=0).astype(BF16)

    def tile(qa, qb, j, states, q0, causal):
        rows = slice(j * tq, (j + 1) * tq)
        ka_aug = jnp.concatenate([ka_ref[rows, :], e_ref[rows, :]], axis=1)
        va_ones = _values_and_ones(va_ref[rows, :])
        vb_ones = _values_and_ones(vb_ref[rows, :])
        mask = functools.partial(_causal, q0=q0, k0=j * tq) if causal else (lambda s: s)
        rows_q = qa[0].shape[0]
        mla = []
        for hd in range(2):
            s = _dot_nt(qb[hd], kb_ref[rows, hd * LANES:(hd + 1) * LANES])
            mla.append(_softmax_step(mask(s), vb_ones[hd], states[2 + hd]))
        s_both = _dot_nt(jnp.concatenate(qa, axis=0), ka_aug)
        moba = [_softmax_step(mask(s_both[hd * rows_q:(hd + 1) * rows_q]), va_ones[hd], states[hd])
                for hd in range(2)]
        return tuple(moba + mla)

    tq2 = 2 * tq
    qa_all = [_moba_queries(qa_ref[t * tq2:(t + 1) * tq2, :], kbar, t, n_blocks) for t in range(seq // tq2)]
    for t in range(seq // tq2):
        qrows = slice(t * tq2, (t + 1) * tq2)
        qa = qa_all[t]
        qb = [qb_ref[qrows, hd * LANES:(hd + 1) * LANES] for hd in range(2)]
        states = (_softmax_init(tq2),) * 4
        for j in range(2 * t + 1):
            states = tile(qa, qb, j, states, t * tq2, j == 2 * t)
        low = tuple((m[tq:], acc[tq:]) for m, acc in states)
        low = tile([q[tq:] for q in qa], [q[tq:] for q in qb], 2 * t + 1, low, t * tq2 + tq, True)
        states = tuple((jnp.concatenate([m[:tq], ml], axis=0), jnp.concatenate([acc[:tq], accl], axis=0))
                       for (m, acc), (ml, accl) in zip(states, low))
        oa_ref[qrows, :] = _merge_pair(states[:2]).astype(oa_ref.dtype)
        ob_ref[qrows, :] = _merge_pair(states[2:]).astype(ob_ref.dtype)


def _attention(qa, ka, va, e_mat, qb, kb, vb):
    b, s, _ = qa.shape
    seq_spec = lambda w: pl.BlockSpec((None, s, w), lambda i, p: (i, 0, p))
    return pl.pallas_call(
        _attn_body,
        out_shape=[jax.ShapeDtypeStruct((b, s, A_WIDTH), BF16)] * 2,
        grid=(b, N_PAIRS),
        in_specs=[seq_spec(LANES), seq_spec(LANES), seq_spec(LANES), _resident(e_mat.shape),
                  seq_spec(2 * LANES), seq_spec(2 * LANES), seq_spec(LANES)],
        out_specs=[seq_spec(LANES)] * 2,
        compiler_params=_params(2),
        name="attention",
    )(qa, ka, va, e_mat, qb, kb, vb)


def _mixer_out_body(layer_ref, x_ref, oa_ref, ob_ref, kv_ref, gm_ref, wg_ref, wa_ref, wb_ref, wo_ref,
                    gx_ref, wxq_ref, wxo_ref, o_ref):
    x = x_ref[...]
    d = x.shape[-1]
    h = _rms(x, gm_ref[...]).astype(BF16)
    merged = (jax.nn.sigmoid(_dot(h, wg_ref[:, :d])) * _dot(oa_ref[...], wa_ref[...])
              + jax.nn.sigmoid(_dot(h, wg_ref[:, d:])) * _dot(ob_ref[...], wb_ref[...]))
    x1 = x + _dot(merged.astype(BF16), wo_ref[...])

    q = _dot(_rms(x1, gx_ref[...]).astype(BF16), wxq_ref[...]).astype(BF16)
    scale = X_HEAD_DIM ** -0.5
    heads = []
    for hd in range(X_HEADS):
        lanes = slice(hd * X_HEAD_DIM, (hd + 1) * X_HEAD_DIM)
        s = _dot_nt(q[:, lanes], kv_ref[:, lanes])
        m = jnp.max(s, axis=-1, keepdims=True)
        p = jnp.exp((s - m) * scale)
        l = jnp.sum(p, axis=-1, keepdims=True)
        v = kv_ref[:, X_WIDTH + hd * X_HEAD_DIM:X_WIDTH + (hd + 1) * X_HEAD_DIM]
        heads.append((_dot(p.astype(BF16), v) * (1.0 / l)).astype(BF16))
    o_ref[...] = x1 + _dot(jnp.concatenate(heads, axis=1), wxo_ref[...])


def _mixer_out(layer, x, seq, oa, ob, mem_kv, norm_mix, w_gates, w_a, w_b, w_out, norm_xattn, w_xq, w_xo):
    t, d = x.shape
    tm = min(TOK_TILE, seq)
    tiles_per_seq = seq // tm
    m = mem_kv.shape[2]
    tok = lambda w: pl.BlockSpec((tm, w), lambda i, l: (i, 0))
    weights = (norm_mix, w_gates, w_a, w_b, w_out, norm_xattn, w_xq, w_xo)
    return pl.pallas_call(
        _mixer_out_body,
        out_shape=jax.ShapeDtypeStruct((t, d), F32),
        grid_spec=_layer_grid(
            (t // tm,),
            [tok(d), tok(A_WIDTH), tok(A_WIDTH),
             pl.BlockSpec((None, None, m, 2 * X_WIDTH), lambda i, l: (l[0], i // tiles_per_seq, 0, 0))]
            + [_layer_resident(w.shape) for w in weights],
            tok(d)),
        compiler_params=_params(1),
        name="mixer_out",
    )(layer, x, oa, ob, mem_kv, *weights)


FF_CHUNK = 1024


def _ffn_body(layer_ref, x_ref, g_ref, wgu_ref, wd_ref, gf_ref, o_ref, *, final):
    x = x_ref[...]
    h = _rms(x, g_ref[...]).astype(BF16)
    d_ff = wd_ref.shape[0]
    y = x
    for c0 in range(0, d_ff, FF_CHUNK):
        c1 = min(c0 + FF_CHUNK, d_ff)
        gt = _dot(h, wgu_ref[:, c0:c1])
        up = _dot(h, wgu_ref[:, d_ff + c0:d_ff + c1])
        a = (gt * jax.nn.sigmoid(gt) * up).astype(BF16)
        y = y + _dot(a, wd_ref[c0:c1, :])
    o_ref[...] = _rms(y, gf_ref[...]) if final else y


def _ffn(layer, x, norm_ffn, w_gate_up, w_down, norm_final, final):
    t, d = x.shape
    tm = min(TOK_TILE, t)
    tok = pl.BlockSpec((tm, d), lambda i, l: (i, 0))
    weights = (norm_ffn, w_gate_up, w_down)
    return pl.pallas_call(
        functools.partial(_ffn_body, final=final),
        out_shape=jax.ShapeDtypeStruct((t, d), F32),
        grid_spec=_layer_grid((t // tm,), [tok] + [_layer_resident(w.shape) for w in weights]
                              + [_resident(norm_final.shape)], tok),
        compiler_params=_params(1),
        name="ffn_final" if final else "ffn",
    )(layer, x, *weights, norm_final)


PREP_ROWS = 256


def _prep_w_in_body(wt_ref, o_ref, og_ref):
    in_w, rows = wt_ref.shape
    k0 = _C_LAT + Q_LORA + KV_LORA
    n_gate_groups = (in_w - k0 - MLA_ROPE) // LANES
    half = MLA_ROPE // 2
    lane = lax.broadcasted_iota(jnp.int32, (rows, LANES), 1)

    def cols(c0):
        n = min(LANES, in_w - c0)
        blk = wt_ref[c0:c0 + n, :]
        if n < LANES:
            blk = jnp.concatenate([blk, jnp.zeros((LANES - n, rows), blk.dtype)], axis=0)
        return blk.T

    for j in range(k0 // LANES):
        o_ref[:, j * LANES:(j + 1) * LANES] = cols(j * LANES).astype(BF16)
    g = cols(k0)
    kpe = jnp.where(lane < half, g, jnp.where((lane >= HALF_LANES) & (lane < HALF_LANES + half),
                                              pltpu.roll(g, HALF_LANES - half, axis=1), 0.0))
    o_ref[:, k0:k0 + LANES] = kpe.astype(BF16)
    a = g
    for j in range(n_gate_groups):
        b = cols(k0 + (j + 1) * LANES)
        shifted = pltpu.roll(jnp.where(lane >= MLA_ROPE, a, b), LANES - MLA_ROPE, axis=1)
        og_ref[:, j * LANES:(j + 1) * LANES] = shifted.astype(BF16)
        a = b


def _prep_w_in(w_in):
    n_layers, d, in_w = w_in.shape
    tr = min(PREP_ROWS, d)
    blk = lambda w: pl.BlockSpec((None, tr, w), lambda l, i: (l, i, 0))
    return pl.pallas_call(
        _prep_w_in_body,
        out_shape=[jax.ShapeDtypeStruct((n_layers, d, _C_G), BF16),
                   jax.ShapeDtypeStruct((n_layers, d, in_w - _C_G + LANES - MLA_ROPE), BF16)],
        grid=(n_layers, d // tr),
        in_specs=[pl.BlockSpec((None, in_w, tr), lambda l, i: (l, 0, i))],
        out_specs=[blk(_C_G), blk(in_w - _C_G + LANES - MLA_ROPE)],
        compiler_params=_params(2),
        name="prep_w_in",
    )(jnp.swapaxes(w_in, 1, 2))


def _mla_head_cols(nope, rope):
    half = MLA_ROPE // 2
    z = lambda n: jnp.zeros(nope.shape[:-1] + (n,), nope.dtype)
    t1, t2 = (z(half), z(half)) if rope is None else (rope[..., :half], rope[..., half:])
    return jnp.concatenate([t1, nope[..., :HEAD_DIM - half], t2, nope[..., HEAD_DIM - half:], z(LANES - MLA_QK)], axis=-1)


def _prep_weights(w_in, w_uq, w_ukv):
    n_layers = w_in.shape[0]
    w_in_r, w_gates = _prep_w_in(w_in)
    uq = w_uq.reshape(n_layers, Q_LORA, N_HEADS, MLA_QK)
    w_uq_r = _mla_head_cols(uq[..., :HEAD_DIM], uq[..., HEAD_DIM:]).reshape(n_layers, Q_LORA, _KB_W).astype(BF16)
    ukv = w_ukv.reshape(n_layers, KV_LORA, N_HEADS, 2 * HEAD_DIM)
    w_k = _mla_head_cols(ukv[..., :HEAD_DIM], None).reshape(n_layers, KV_LORA, _KB_W)
    w_v = ukv[..., HEAD_DIM:].reshape(n_layers, KV_LORA, A_WIDTH)
    w_ukv_r = jnp.concatenate([w_k, w_v], axis=-1).astype(BF16)
    return w_in_r, w_gates, w_uq_r, w_ukv_r


def kernel(x, mem, positions, norm_mix, w_in, q_lat_norm, w_uq, kv_lat_norm, w_ukv, w_branch_a, w_branch_b, w_out, norm_xattn, norm_mem, w_xq, w_xkv, w_xo, norm_ffn, w_gate_up, w_down, norm_final):
    b, s, d = x.shape
    n_layers = w_in.shape[0]
    t = b * s
    assert s % (2 * ATT_TILE) == 0 and ATT_TILE % MOBA_BLOCK == 0 and s // MOBA_BLOCK <= MAX_BLOCKS

    w_in_r, w_gates, w_uq_r, w_ukv_r = _prep_weights(w_in, w_uq, w_ukv)
    bf = lambda w: w.astype(BF16)
    row = lambda g: g.reshape(g.shape[0], 1, g.shape[1])
    tabs = _rope_tables(positions)
    mem_kv = _mem_kv(mem, row(norm_mem), bf(w_xkv))
    blk_of_row = np.arange(s) // MOBA_BLOCK
    e_mat = jnp.asarray(np.where(blk_of_row[:, None] == np.arange(LANES)[None, :], NEG_BIG, 0.0), BF16)
    norm_final = norm_final.reshape(1, d)

    norm_mix, q_lat_norm, kv_lat_norm = row(norm_mix), row(q_lat_norm), row(kv_lat_norm)
    norm_xattn, norm_ffn = row(norm_xattn), row(norm_ffn)
    w_a, w_b, w_o, w_q, w_xo_b = bf(w_branch_a), bf(w_branch_b), bf(w_out), bf(w_xq), bf(w_xo)
    w_gu, w_dn = bf(w_gate_up), bf(w_down)
    seq3 = lambda a: a.reshape(b, s, a.shape[-1])

    def layer(li, xt, final):
        li = jnp.full((1,), li, jnp.int32)
        qa, ka, va, qb, kb, vb = _mixer_in(li, xt, norm_mix, w_in_r, q_lat_norm, w_uq_r,
                                           kv_lat_norm, w_ukv_r, tabs)
        oa, ob = _attention(seq3(qa), seq3(ka), seq3(va), e_mat, seq3(qb), seq3(kb), seq3(vb))
        oa, ob = oa.reshape(t, A_WIDTH), ob.reshape(t, A_WIDTH)
        xt = _mixer_out(li, xt, s, oa, ob, mem_kv, norm_mix, w_gates, w_a, w_b, w_o, norm_xattn, w_q, w_xo_b)
        return _ffn(li, xt, norm_ffn, w_gu, w_dn, norm_final, final)

    xt = x.reshape(t, d)
    for li in range(n_layers):
        xt = layer(li, xt, li == n_layers - 1)
    return xt.reshape(b, s, d)
```

```python
import functools
import math

import numpy as np
import jax
import jax.numpy as jnp
from jax import lax
from jax.experimental import pallas as pl
from jax.experimental.pallas import tpu as pltpu

F32 = jnp.float32
BF16 = jnp.bfloat16

EPS = 1e-6
ROPE_THETA = 10000.0
LANES = 128
HALF_LANES = LANES // 2
HEAD_DIM = 64
N_HEADS = 8
N_PAIRS = N_HEADS // 2
A_WIDTH = N_HEADS * HEAD_DIM
MLA_ROPE = 32
MLA_QK = HEAD_DIM + MLA_ROPE
Q_LORA = 384
KV_LORA = 256
MOBA_BLOCK = 256
MOBA_TOPK = 3
MAX_BLOCKS = 16
X_HEADS = 4
X_HEAD_DIM = 128
X_WIDTH = X_HEADS * X_HEAD_DIM
NEG_BIG = -1e30
LOG2E = math.log2(math.e)
VMEM_LIMIT = 56 * 1024 * 1024
TOK_TILE = 1024
ATT_TILE = 512
TAB_W = 5 * LANES


def _resident(shape):
    return pl.BlockSpec(shape, lambda *_: (0,) * len(shape), pipeline_mode=pl.Buffered(1))


def _layer_resident(shape):
    return pl.BlockSpec((None,) + tuple(shape[1:]), lambda *a: (a[-1][0],) + (0,) * (len(shape) - 1),
                        pipeline_mode=pl.Buffered(1))


def _layer_grid(grid, in_specs, out_specs):
    return pltpu.PrefetchScalarGridSpec(num_scalar_prefetch=1, grid=grid, in_specs=in_specs, out_specs=out_specs)


def _params(n_axes):
    return pltpu.CompilerParams(dimension_semantics=("arbitrary",) * n_axes,
                                vmem_limit_bytes=VMEM_LIMIT)


def _rms(x, g):
    return x * lax.rsqrt(jnp.mean(x * x, axis=-1, keepdims=True) + EPS) * g


def _dot(a, b):
    return jnp.dot(a, b, preferred_element_type=F32)


def _dot_nt(a, b):
    return lax.dot_general(a, b, (((1,), (1,)), ((), ())), preferred_element_type=F32)


def _rope_table_body(pos_ref, tab_ref):
    pos = pos_ref[...]
    lane = lax.broadcasted_iota(jnp.int32, pos.shape, 1)
    sub = lane % HALF_LANES
    qtr = HEAD_DIM // 2
    inv = jnp.where(sub < qtr,
                    jnp.exp((-math.log(ROPE_THETA) * (2.0 / HEAD_DIM)) * sub.astype(F32)),
                    jnp.exp((-math.log(ROPE_THETA) * (2.0 / MLA_ROPE)) * (sub - qtr).astype(F32)))
    ang = pos * inv
    cos2, sin2 = jnp.cos(ang), jnp.sin(ang)

    def spread(v):
        r1, r3 = pltpu.roll(v, qtr, axis=1), pltpu.roll(v, 3 * qtr, axis=1)
        tiled = jnp.where(lane < qtr, v, jnp.where(lane < 2 * qtr, r1, jnp.where(
            lane < 3 * qtr, pltpu.roll(v, 2 * qtr, axis=1), r3)))
        return tiled, jnp.where(lane < HALF_LANES, r3, r1)

    first_half = sub < qtr
    is_rope = sub < MLA_ROPE // 2
    for half in range(2):
        cos, sin = (cos2, sin2) if half == 0 else (pltpu.roll(cos2, HALF_LANES, axis=1),
                                                               pltpu.roll(sin2, HALF_LANES, axis=1))
        cos_a, cos_b = spread(cos)
        sin_a, sin_b = spread(sin)
        tab_ref[half, :, 0:LANES] = cos_a
        tab_ref[half, :, LANES:2 * LANES] = jnp.where(first_half, -sin_a, 0.0)
        tab_ref[half, :, 2 * LANES:3 * LANES] = jnp.where(first_half, 0.0, sin_a)
        tab_ref[half, :, 3 * LANES:4 * LANES] = jnp.where(is_rope, cos_b, 1.0)
        tab_ref[half, :, 4 * LANES:5 * LANES] = jnp.where(is_rope, jnp.where(lane < HALF_LANES, -sin_b, sin_b), 0.0)


def _rope_tables(positions):
    t = positions.size
    half = t // 2
    pos = positions.reshape(2, half, 1).astype(F32)
    pos2 = jnp.concatenate([jnp.broadcast_to(pos[0], (half, HALF_LANES)),
                            jnp.broadcast_to(pos[1], (half, HALF_LANES))], axis=1)
    tm = min(1024, half)
    tabs = pl.pallas_call(
        _rope_table_body,
        out_shape=jax.ShapeDtypeStruct((2, half, TAB_W), F32),
        grid=(half // tm,),
        in_specs=[pl.BlockSpec((tm, LANES), lambda i: (i, 0))],
        out_specs=pl.BlockSpec((2, tm, TAB_W), lambda i: (0, i, 0)),
        compiler_params=_params(1),
        name="rope_tables",
    )(pos2)
    return tabs.reshape(t, TAB_W)


MEM_BATCH = 8


def _mem_kv_body(mem_ref, g_ref, w_ref, o_ref):
    nb, m, d = mem_ref.shape
    h = _rms(mem_ref[...].reshape(nb * m, d), g_ref[...]).astype(BF16)
    o_ref[...] = _dot(h, w_ref[...]).astype(BF16).reshape(o_ref.shape)


def _mem_kv(mem, norm_mem, w_xkv):
    b, m, d = mem.shape
    n_layers = w_xkv.shape[0]
    nb = math.gcd(b, MEM_BATCH)
    return pl.pallas_call(
        _mem_kv_body,
        out_shape=jax.ShapeDtypeStruct((n_layers, b, m, 2 * X_WIDTH), BF16),
        grid=(n_layers, b // nb),
        in_specs=[pl.BlockSpec((nb, m, d), lambda l, i: (i, 0, 0)),
                  pl.BlockSpec((None, 1, d), lambda l, i: (l, 0, 0)),
                  pl.BlockSpec((None, d, 2 * X_WIDTH), lambda l, i: (l, 0, 0))],
        out_specs=pl.BlockSpec((None, nb, m, 2 * X_WIDTH), lambda l, i: (l, i, 0, 0)),
        compiler_params=_params(2),
        name="mem_kv",
    )(mem, norm_mem, w_xkv)


_C_QK = 0
_C_VA = 2 * A_WIDTH
_C_LAT = 3 * A_WIDTH
_C_G = _C_LAT + Q_LORA + KV_LORA + LANES
_KB_W = N_HEADS * LANES


def _mixer_in_body(layer_ref, x_ref, g_ref, win_ref, qg_ref, wuq_ref, kvg_ref, wukv_ref, tab_ref,
                   qa_ref, kat_ref, va_ref, qb_ref, kbt_ref, vb_ref, kbar_ref):
    h = _rms(x_ref[...], g_ref[...]).astype(BF16)
    cos_a, sa_lo, sa_hi = tab_ref[:, 0:LANES], tab_ref[:, LANES:2 * LANES], tab_ref[:, 2 * LANES:3 * LANES]
    cos_b, sin_b = tab_ref[:, 3 * LANES:4 * LANES], tab_ref[:, 4 * LANES:5 * LANES]

    def rope_a(y):
        return (y * cos_a + pltpu.roll(y, LANES - HEAD_DIM // 2, axis=1) * sa_lo
                + pltpu.roll(y, HEAD_DIM // 2, axis=1) * sa_hi)

    def rope_b(y):
        return y * cos_b + pltpu.roll(y, HALF_LANES, axis=1) * sin_b

    scale_a = HEAD_DIM ** -0.5 * LOG2E
    scale_b = MLA_QK ** -0.5 * LOG2E
    lat = _dot(h, win_ref[:, _C_LAT:_C_G])
    y = _dot(h, win_ref[:, _C_QK:_C_QK + 2 * A_WIDTH])
    q_lat = _rms(lat[:, 0:Q_LORA], qg_ref[...]).astype(BF16)
    kv_lat = _rms(lat[:, Q_LORA:Q_LORA + KV_LORA], kvg_ref[...]).astype(BF16)
    n_blk = h.shape[0] // MOBA_BLOCK
    kbar_ref[...] = jnp.zeros_like(kbar_ref)
    for p in range(N_PAIRS):
        sl = slice(p * LANES, (p + 1) * LANES)
        qa_ref[:, sl] = (rope_a(y[:, sl]) * scale_a).astype(BF16)
        k = rope_a(y[:, A_WIDTH + p * LANES:A_WIDTH + (p + 1) * LANES])
        kat_ref[sl, :] = k.T.astype(BF16)
        for blk in range(n_blk):
            kbar_ref[blk:blk + 1, sl] = jnp.mean(k[blk * MOBA_BLOCK:(blk + 1) * MOBA_BLOCK], axis=0, keepdims=True)
    yq = _dot(q_lat, wuq_ref[...])
    ykv = _dot(kv_lat, wukv_ref[...])
    va_ref[...] = _dot(h, win_ref[:, _C_VA:_C_VA + A_WIDTH]).astype(BF16)
    k_pe = rope_b(lat[:, Q_LORA + KV_LORA:])
    for hd in range(N_HEADS):
        sl = slice(hd * LANES, (hd + 1) * LANES)
        qb_ref[:, sl] = (rope_b(yq[:, sl]) * scale_b).astype(BF16)
        kbt_ref[sl, :] = (ykv[:, sl] + k_pe).T.astype(BF16)
    vb_ref[...] = ykv[:, _KB_W:].astype(BF16)


KBAR_ROWS = 8


def _mixer_in(layer, x, seq, norm_mix, w_in, q_lat_norm, w_uq, kv_lat_norm, w_ukv, tabs):
    t, d = x.shape
    tm = min(TOK_TILE, seq)
    tps = seq // tm
    b = t // seq
    assert tm // MOBA_BLOCK <= KBAR_ROWS
    tok = lambda w: pl.BlockSpec((tm, w), lambda i, l: (i, 0))
    keyt = lambda w: pl.BlockSpec((None, w, tm), lambda i, l: (i // tps, 0, i % tps))
    weights = (norm_mix, w_in, q_lat_norm, w_uq, kv_lat_norm, w_ukv)
    tok_out = lambda w: jax.ShapeDtypeStruct((t, w), BF16)
    key_out = lambda w: jax.ShapeDtypeStruct((b, w, seq), BF16)
    return pl.pallas_call(
        _mixer_in_body,
        out_shape=[tok_out(A_WIDTH), key_out(A_WIDTH), tok_out(A_WIDTH), tok_out(_KB_W), key_out(_KB_W), tok_out(A_WIDTH),
                   jax.ShapeDtypeStruct((b, tps, KBAR_ROWS, A_WIDTH), F32)],
        grid_spec=_layer_grid((t // tm,), [tok(d)] + [_layer_resident(w.shape) for w in weights] + [tok(TAB_W)],
                              [tok(A_WIDTH), keyt(A_WIDTH), tok(A_WIDTH), tok(_KB_W), keyt(_KB_W), tok(A_WIDTH),
                               pl.BlockSpec((None, None, KBAR_ROWS, A_WIDTH), lambda i, l: (i // tps, i % tps, 0, 0))]),
        compiler_params=_params(1),
        name="mixer_in",
    )(layer, x, *weights, tabs)


def _softmax_init(tq):
    return (jnp.full((tq, 1), NEG_BIG, F32), jnp.zeros((tq, LANES), F32))


def _softmax_step(s, v_ones, state):
    m, acc = state
    m_new = jnp.maximum(m, jnp.max(s, axis=-1, keepdims=True))
    p = jnp.exp2(s - m_new)
    acc_new = jnp.exp2(m - m_new) * acc + _dot(p.astype(BF16), v_ones)
    return m_new, acc_new


def _values_and_ones(v):
    lane = lax.broadcasted_iota(jnp.int32, v.shape, 1)
    one = jnp.ones_like(v)
    return jnp.where(lane < HEAD_DIM, v, one), jnp.where(lane < HEAD_DIM, one, v)


def _causal(s, q0, k0):
    row = lax.broadcasted_iota(jnp.int32, s.shape, 0) + q0
    col = lax.broadcasted_iota(jnp.int32, s.shape, 1) + k0
    return jnp.where(col <= row, s, NEG_BIG)


def _merge_pair(states):
    (_, acc0), (_, acc1) = states
    lane = lax.broadcasted_iota(jnp.int32, acc0.shape, 1)
    num = jnp.where(lane < HEAD_DIM, acc0, acc1)
    den = pltpu.roll(jnp.where(lane < HEAD_DIM, acc1, acc0), HEAD_DIM, axis=1)
    return num / den


def _moba_queries(q, kbar, qi, n_blocks):
    tq = q.shape[0]
    lane = lax.broadcasted_iota(jnp.int32, q.shape, 1)
    row = lax.broadcasted_iota(jnp.int32, (MAX_BLOCKS, tq), 0)
    col = lax.broadcasted_iota(jnp.int32, (MAX_BLOCKS, tq), 1)
    cur = qi * (tq // MOBA_BLOCK) + col // MOBA_BLOCK
    q_aug = []
    for hd in range(2):
        qh = jnp.where((lane < HEAD_DIM) == (hd == 0), q, jnp.zeros_like(q))
        gate = jnp.where(row < cur, _dot_nt(kbar, qh), -jnp.inf)
        rank = jnp.zeros(gate.shape, jnp.int32)
        for n in range(n_blocks):
            gn = gate[n:n + 1, :]
            beats = (gn > gate) | ((gn == gate) & (row > n))
            rank = rank + beats.astype(jnp.int32)
        keep = ((row < cur) & (rank < MOBA_TOPK)) | (row == cur)
        dropped = jnp.where(keep, 0.0, 1.0)
        dropped = jnp.concatenate([dropped, jnp.zeros((LANES - MAX_BLOCKS, tq), F32)], axis=0)
        q_aug.append(jnp.concatenate([qh, dropped.T.astype(BF16)], axis=1))
    return q_aug


def _attn_body(qa_ref, ka_ref, va_ref, e_ref, qb_ref, kb_ref, vb_ref, kbar_ref, oa_ref, ob_ref):
    tq = ATT_TILE
    seq = ka_ref.shape[1]
    n_blocks = seq // MOBA_BLOCK
    kbar = kbar_ref[...].astype(BF16)

    def tile(qa, qb, j, states, q0, causal):
        rows = slice(j * tq, (j + 1) * tq)
        ka_aug = jnp.concatenate([ka_ref[:, rows], e_ref[:, rows]], axis=0)
        va_ones = _values_and_ones(va_ref[rows, :])
        vb_ones = _values_and_ones(vb_ref[rows, :])
        mask = functools.partial(_causal, q0=q0, k0=j * tq) if causal else (lambda s: s)
        rows_q = qa[0].shape[0]
        mla = []
        for hd in range(2):
            s = _dot(qb[hd], kb_ref[hd * LANES:(hd + 1) * LANES, rows])
            mla.append(_softmax_step(mask(s), vb_ones[hd], states[2 + hd]))
        s_both = _dot(jnp.concatenate(qa, axis=0), ka_aug)
        moba = [_softmax_step(mask(s_both[hd * rows_q:(hd + 1) * rows_q]), va_ones[hd], states[hd])
                for hd in range(2)]
        return tuple(moba + mla)

    tq2 = 2 * tq
    qa_all = [_moba_queries(qa_ref[t * tq2:(t + 1) * tq2, :], kbar, t, n_blocks) for t in range(seq // tq2)]
    for t in range(seq // tq2):
        qrows = slice(t * tq2, (t + 1) * tq2)
        qa = qa_all[t]
        qb = [qb_ref[qrows, hd * LANES:(hd + 1) * LANES] for hd in range(2)]
        states = (_softmax_init(tq2),) * 4
        for j in range(2 * t + 1):
            states = tile(qa, qb, j, states, t * tq2, j == 2 * t)
        low = tuple((m[tq:], acc[tq:]) for m, acc in states)
        low = tile([q[tq:] for q in qa], [q[tq:] for q in qb], 2 * t + 1, low, t * tq2 + tq, True)
        states = tuple((jnp.concatenate([m[:tq], ml], axis=0), jnp.concatenate([acc[:tq], accl], axis=0))
                       for (m, acc), (ml, accl) in zip(states, low))
        oa_ref[qrows, :] = _merge_pair(states[:2]).astype(oa_ref.dtype)
        ob_ref[qrows, :] = _merge_pair(states[2:]).astype(ob_ref.dtype)


def _attention(qa, ka, va, e_mat, qb, kb, vb, kbar):
    b, s, _ = qa.shape
    seq_spec = lambda w: pl.BlockSpec((None, s, w), lambda i, p: (i, 0, p))
    key_spec = lambda w: pl.BlockSpec((None, w, s), lambda i, p: (i, p, 0))
    return pl.pallas_call(
        _attn_body,
        out_shape=[jax.ShapeDtypeStruct((b, s, A_WIDTH), BF16)] * 2,
        grid=(b, N_PAIRS),
        in_specs=[seq_spec(LANES), key_spec(LANES), seq_spec(LANES), _resident(e_mat.shape),
                  seq_spec(2 * LANES), key_spec(2 * LANES), seq_spec(LANES),
                  pl.BlockSpec((None, None, MAX_BLOCKS, LANES), lambda i, p: (i, p, 0, 0))],
        out_specs=[seq_spec(LANES)] * 2,
        compiler_params=_params(2),
        name="attention",
    )(qa, ka, va, e_mat, qb, kb, vb, kbar)


def _mixer_out_body(layer_ref, x_ref, oa_ref, ob_ref, kv_ref, gm_ref, wg_ref, wa_ref, wb_ref, wo_ref,
                    gx_ref, wxq_ref, wxo_ref, o_ref):
    x = x_ref[...]
    d = x.shape[-1]
    h = _rms(x, gm_ref[...]).astype(BF16)
    merged = (jax.nn.sigmoid(_dot(h, wg_ref[:, :d])) * _dot(oa_ref[...], wa_ref[...])
              + jax.nn.sigmoid(_dot(h, wg_ref[:, d:])) * _dot(ob_ref[...], wb_ref[...]))
    x1 = x + _dot(merged.astype(BF16), wo_ref[...])

    q = _dot(_rms(x1, gx_ref[...]).astype(BF16), wxq_ref[...]).astype(BF16)
    scale = X_HEAD_DIM ** -0.5
    heads = []
    for hd in range(X_HEADS):
        lanes = slice(hd * X_HEAD_DIM, (hd + 1) * X_HEAD_DIM)
        s = _dot_nt(q[:, lanes], kv_ref[:, lanes])
        m = jnp.max(s, axis=-1, keepdims=True)
        p = jnp.exp((s - m) * scale)
        l = jnp.sum(p, axis=-1, keepdims=True)
        v = kv_ref[:, X_WIDTH + hd * X_HEAD_DIM:X_WIDTH + (hd + 1) * X_HEAD_DIM]
        heads.append((_dot(p.astype(BF16), v) * (1.0 / l)).astype(BF16))
    o_ref[...] = x1 + _dot(jnp.concatenate(heads, axis=1), wxo_ref[...])


def _mixer_out(layer, x, seq, oa, ob, mem_kv, norm_mix, w_gates, w_a, w_b, w_out, norm_xattn, w_xq, w_xo):
    t, d = x.shape
    tm = min(TOK_TILE, seq)
    tiles_per_seq = seq // tm
    m = mem_kv.shape[2]
    tok = lambda w: pl.BlockSpec((tm, w), lambda i, l: (i, 0))
    weights = (norm_mix, w_gates, w_a, w_b, w_out, norm_xattn, w_xq, w_xo)
    return pl.pallas_call(
        _mixer_out_body,
        out_shape=jax.ShapeDtypeStruct((t, d), F32),
        grid_spec=_layer_grid(
            (t // tm,),
            [tok(d), tok(A_WIDTH), tok(A_WIDTH),
             pl.BlockSpec((None, None, m, 2 * X_WIDTH), lambda i, l: (l[0], i // tiles_per_seq, 0, 0))]
            + [_layer_resident(w.shape) for w in weights],
            tok(d)),
        compiler_params=_params(1),
        name="mixer_out",
    )(layer, x, oa, ob, mem_kv, *weights)


FF_CHUNK = 1024


def _ffn_body(layer_ref, x_ref, g_ref, wgu_ref, wd_ref, gf_ref, o_ref, *, final):
    x = x_ref[...]
    h = _rms(x, g_ref[...]).astype(BF16)
    d_ff = wd_ref.shape[0]
    y = x
    for c0 in range(0, d_ff, FF_CHUNK):
        c1 = min(c0 + FF_CHUNK, d_ff)
        gt = _dot(h, wgu_ref[:, c0:c1])
        up = _dot(h, wgu_ref[:, d_ff + c0:d_ff + c1])
        a = (gt * jax.nn.sigmoid(gt) * up).astype(BF16)
        y = y + _dot(a, wd_ref[c0:c1, :])
    o_ref[...] = _rms(y, gf_ref[...]) if final else y


def _ffn(layer, x, norm_ffn, w_gate_up, w_down, norm_final, final):
    t, d = x.shape
    tm = min(TOK_TILE, t)
    tok = pl.BlockSpec((tm, d), lambda i, l: (i, 0))
    weights = (norm_ffn, w_gate_up, w_down)
    return pl.pallas_call(
        functools.partial(_ffn_body, final=final),
        out_shape=jax.ShapeDtypeStruct((t, d), F32),
        grid_spec=_layer_grid((t // tm,), [tok] + [_layer_resident(w.shape) for w in weights]
                              + [_resident(norm_final.shape)], tok),
        compiler_params=_params(1),
        name="ffn_final" if final else "ffn",
    )(layer, x, *weights, norm_final)


PREP_ROWS = 256


def _prep_w_in_body(wt_ref, o_ref, og_ref):
    in_w, rows = wt_ref.shape
    k0 = _C_LAT + Q_LORA + KV_LORA
    n_gate_groups = (in_w - k0 - MLA_ROPE) // LANES
    half = MLA_ROPE // 2
    lane = lax.broadcasted_iota(jnp.int32, (rows, LANES), 1)

    def cols(c0):
        n = min(LANES, in_w - c0)
        blk = wt_ref[c0:c0 + n, :]
        if n < LANES:
            blk = jnp.concatenate([blk, jnp.zeros((LANES - n, rows), blk.dtype)], axis=0)
        return blk.T

    for j in range(k0 // LANES):
        o_ref[:, j * LANES:(j + 1) * LANES] = cols(j * LANES).astype(BF16)
    g = cols(k0)
    kpe = jnp.where(lane < half, g, jnp.where((lane >= HALF_LANES) & (lane < HALF_LANES + half),
                                              pltpu.roll(g, HALF_LANES - half, axis=1), 0.0))
    o_ref[:, k0:k0 + LANES] = kpe.astype(BF16)
    a = g
    for j in range(n_gate_groups):
        b = cols(k0 + (j + 1) * LANES)
        shifted = pltpu.roll(jnp.where(lane >= MLA_ROPE, a, b), LANES - MLA_ROPE, axis=1)
        og_ref[:, j * LANES:(j + 1) * LANES] = shifted.astype(BF16)
        a = b


def _prep_w_in(w_in):
    n_layers, d, in_w = w_in.shape
    tr = min(PREP_ROWS, d)
    blk = lambda w: pl.BlockSpec((None, tr, w), lambda l, i: (l, i, 0))
    return pl.pallas_call(
        _prep_w_in_body,
        out_shape=[jax.ShapeDtypeStruct((n_layers, d, _C_G), BF16),
                   jax.ShapeDtypeStruct((n_layers, d, in_w - _C_G + LANES - MLA_ROPE), BF16)],
        grid=(n_layers, d // tr),
        in_specs=[pl.BlockSpec((None, in_w, tr), lambda l, i: (l, 0, i))],
        out_specs=[blk(_C_G), blk(in_w - _C_G + LANES - MLA_ROPE)],
        compiler_params=_params(2),
        name="prep_w_in",
    )(jnp.swapaxes(w_in, 1, 2))


def _mla_head_cols(nope, rope):
    half = MLA_ROPE // 2
    z = lambda n: jnp.zeros(nope.shape[:-1] + (n,), nope.dtype)
    t1, t2 = (z(half), z(half)) if rope is None else (rope[..., :half], rope[..., half:])
    return jnp.concatenate([t1, nope[..., :HEAD_DIM - half], t2, nope[..., HEAD_DIM - half:], z(LANES - MLA_QK)], axis=-1)


def _prep_weights(w_in, w_uq, w_ukv):
    n_layers = w_in.shape[0]
    w_in_r, w_gates = _prep_w_in(w_in)
    uq = w_uq.reshape(n_layers, Q_LORA, N_HEADS, MLA_QK)
    w_uq_r = _mla_head_cols(uq[..., :HEAD_DIM], uq[..., HEAD_DIM:]).reshape(n_layers, Q_LORA, _KB_W).astype(BF16)
    ukv = w_ukv.reshape(n_layers, KV_LORA, N_HEADS, 2 * HEAD_DIM)
    w_k = _mla_head_cols(ukv[..., :HEAD_DIM], None).reshape(n_layers, KV_LORA, _KB_W)
    w_v = ukv[..., HEAD_DIM:].reshape(n_layers, KV_LORA, A_WIDTH)
    w_ukv_r = jnp.concatenate([w_k, w_v], axis=-1).astype(BF16)
    return w_in_r, w_gates, w_uq_r, w_ukv_r


def kernel(x, mem, positions, norm_mix, w_in, q_lat_norm, w_uq, kv_lat_norm, w_ukv, w_branch_a, w_branch_b, w_out, norm_xattn, norm_mem, w_xq, w_xkv, w_xo, norm_ffn, w_gate_up, w_down, norm_final):
    b, s, d = x.shape
    n_layers = w_in.shape[0]
    t = b * s
    assert s % (2 * ATT_TILE) == 0 and ATT_TILE % MOBA_BLOCK == 0 and s // MOBA_BLOCK <= MAX_BLOCKS

    w_in_r, w_gates, w_uq_r, w_ukv_r = _prep_weights(w_in, w_uq, w_ukv)
    bf = lambda w: w.astype(BF16)
    row = lambda g: g.reshape(g.shape[0], 1, g.shape[1])
    tabs = _rope_tables(positions)
    mem_kv = _mem_kv(mem, row(norm_mem), bf(w_xkv))
    blk_of_row = np.arange(s) // MOBA_BLOCK
    e_mat = jnp.asarray(np.where(np.arange(LANES)[:, None] == blk_of_row[None, :], NEG_BIG, 0.0), BF16)
    norm_final = norm_final.reshape(1, d)

    norm_mix, q_lat_norm, kv_lat_norm = row(norm_mix), row(q_lat_norm), row(kv_lat_norm)
    norm_xattn, norm_ffn = row(norm_xattn), row(norm_ffn)
    w_a, w_b, w_o, w_q, w_xo_b = bf(w_branch_a), bf(w_branch_b), bf(w_out), bf(w_xq), bf(w_xo)
    w_gu, w_dn = bf(w_gate_up), bf(w_down)
    seq3 = lambda a: a.reshape(b, s, a.shape[-1])

    def layer(li, xt, final):
        li = jnp.full((1,), li, jnp.int32)
        qa, kat, va, qb, kbt, vb, kbar = _mixer_in(li, xt, s, norm_mix, w_in_r, q_lat_norm, w_uq_r,
                                                   kv_lat_norm, w_ukv_r, tabs)
        n_blocks = s // MOBA_BLOCK
        kbar = kbar[:, :, :n_blocks // kbar.shape[1]].reshape(b, n_blocks, N_PAIRS, LANES)
        kbar = jnp.pad(kbar, ((0, 0), (0, MAX_BLOCKS - n_blocks), (0, 0), (0, 0))).transpose(0, 2, 1, 3)
        oa, ob = _attention(seq3(qa), kat, seq3(va), e_mat, seq3(qb), kbt, seq3(vb), kbar)
        oa, ob = oa.reshape(t, A_WIDTH), ob.reshape(t, A_WIDTH)
        xt = _mixer_out(li, xt, s, oa, ob, mem_kv, norm_mix, w_gates, w_a, w_b, w_o, norm_xattn, w_q, w_xo_b)
        return _ffn(li, xt, norm_ffn, w_gu, w_dn, norm_final, final)

    xt = x.reshape(t, d)
    for li in range(n_layers):
        xt = layer(li, xt, li == n_layers - 1)
    return xt.reshape(b, s, d)
```

```python
import functools
import math

import numpy as np
import jax
import jax.numpy as jnp
from jax import lax
from jax.experimental import pallas as pl
from jax.experimental.pallas import tpu as pltpu

F32 = jnp.float32
BF16 = jnp.bfloat16

EPS = 1e-6
ROPE_THETA = 10000.0
LANES = 128
HALF_LANES = LANES // 2
HEAD_DIM = 64
N_HEADS = 8
N_PAIRS = N_HEADS // 2
A_WIDTH = N_HEADS * HEAD_DIM
MLA_ROPE = 32
MLA_QK = HEAD_DIM + MLA_ROPE
Q_LORA = 384
KV_LORA = 256
MOBA_BLOCK = 256
MOBA_TOPK = 3
MAX_BLOCKS = 16
X_HEADS = 4
X_HEAD_DIM = 128
X_WIDTH = X_HEADS * X_HEAD_DIM
NEG_BIG = -1e30
LOG2E = math.log2(math.e)
VMEM_LIMIT = 56 * 1024 * 1024
TOK_TILE = 1024
ATT_TILE = 512
TAB_W = 5 * LANES


def _resident(shape):
    return pl.BlockSpec(shape, lambda *_: (0,) * len(shape), pipeline_mode=pl.Buffered(1))


def _layer_resident(shape):
    return pl.BlockSpec((None,) + tuple(shape[1:]), lambda *a: (a[-1][0],) + (0,) * (len(shape) - 1),
                        pipeline_mode=pl.Buffered(1))


def _layer_grid(grid, in_specs, out_specs):
    return pltpu.PrefetchScalarGridSpec(num_scalar_prefetch=1, grid=grid, in_specs=in_specs, out_specs=out_specs)


def _params(n_axes):
    return pltpu.CompilerParams(dimension_semantics=("arbitrary",) * n_axes,
                                vmem_limit_bytes=VMEM_LIMIT)


def _rms(x, g):
    return x * lax.rsqrt(jnp.mean(x * x, axis=-1, keepdims=True) + EPS) * g


def _dot(a, b):
    return jnp.dot(a, b, preferred_element_type=F32)


def _dot_nt(a, b):
    return lax.dot_general(a, b, (((1,), (1,)), ((), ())), preferred_element_type=F32)


def _rope_table_body(pos_ref, tab_ref):
    pos = pos_ref[...]
    lane = lax.broadcasted_iota(jnp.int32, pos.shape, 1)
    sub = lane % HALF_LANES
    qtr = HEAD_DIM // 2
    inv = jnp.where(sub < qtr,
                    jnp.exp((-math.log(ROPE_THETA) * (2.0 / HEAD_DIM)) * sub.astype(F32)),
                    jnp.exp((-math.log(ROPE_THETA) * (2.0 / MLA_ROPE)) * (sub - qtr).astype(F32)))
    ang = pos * inv
    cos2, sin2 = jnp.cos(ang), jnp.sin(ang)

    def spread(v):
        r1, r3 = pltpu.roll(v, qtr, axis=1), pltpu.roll(v, 3 * qtr, axis=1)
        tiled = jnp.where(lane < qtr, v, jnp.where(lane < 2 * qtr, r1, jnp.where(
            lane < 3 * qtr, pltpu.roll(v, 2 * qtr, axis=1), r3)))
        return tiled, jnp.where(lane < HALF_LANES, r3, r1)

    first_half = sub < qtr
    is_rope = sub < MLA_ROPE // 2
    for half in range(2):
        cos, sin = (cos2, sin2) if half == 0 else (pltpu.roll(cos2, HALF_LANES, axis=1),
                                                               pltpu.roll(sin2, HALF_LANES, axis=1))
        cos_a, cos_b = spread(cos)
        sin_a, sin_b = spread(sin)
        tab_ref[half, :, 0:LANES] = cos_a
        tab_ref[half, :, LANES:2 * LANES] = jnp.where(first_half, -sin_a, 0.0)
        tab_ref[half, :, 2 * LANES:3 * LANES] = jnp.where(first_half, 0.0, sin_a)
        tab_ref[half, :, 3 * LANES:4 * LANES] = jnp.where(is_rope, cos_b, 1.0)
        tab_ref[half, :, 4 * LANES:5 * LANES] = jnp.where(is_rope, jnp.where(lane < HALF_LANES, -sin_b, sin_b), 0.0)


def _rope_tables(positions):
    t = positions.size
    half = t // 2
    pos = positions.reshape(2, half, 1).astype(F32)
    pos2 = jnp.concatenate([jnp.broadcast_to(pos[0], (half, HALF_LANES)),
                            jnp.broadcast_to(pos[1], (half, HALF_LANES))], axis=1)
    tm = min(1024, half)
    tabs = pl.pallas_call(
        _rope_table_body,
        out_shape=jax.ShapeDtypeStruct((2, half, TAB_W), F32),
        grid=(half // tm,),
        in_specs=[pl.BlockSpec((tm, LANES), lambda i: (i, 0))],
        out_specs=pl.BlockSpec((2, tm, TAB_W), lambda i: (0, i, 0)),
        compiler_params=_params(1),
        name="rope_tables",
    )(pos2)
    return tabs.reshape(t, TAB_W)


MEM_BATCH = 8


def _mem_kv_body(mem_ref, g_ref, w_ref, o_ref):
    nb, m, d = mem_ref.shape
    h = _rms(mem_ref[...].reshape(nb * m, d), g_ref[...]).astype(BF16)
    o_ref[...] = _dot(h, w_ref[...]).astype(BF16).reshape(o_ref.shape)


def _mem_kv(mem, norm_mem, w_xkv):
    b, m, d = mem.shape
    n_layers = w_xkv.shape[0]
    nb = math.gcd(b, MEM_BATCH)
    return pl.pallas_call(
        _mem_kv_body,
        out_shape=jax.ShapeDtypeStruct((n_layers, b, m, 2 * X_WIDTH), BF16),
        grid=(n_layers, b // nb),
        in_specs=[pl.BlockSpec((nb, m, d), lambda l, i: (i, 0, 0)),
                  pl.BlockSpec((None, 1, d), lambda l, i: (l, 0, 0)),
                  pl.BlockSpec((None, d, 2 * X_WIDTH), lambda l, i: (l, 0, 0))],
        out_specs=pl.BlockSpec((None, nb, m, 2 * X_WIDTH), lambda l, i: (l, i, 0, 0)),
        compiler_params=_params(2),
        name="mem_kv",
    )(mem, norm_mem, w_xkv)


_C_QK = 0
_C_VA = 2 * A_WIDTH
_C_LAT = 3 * A_WIDTH
_C_G = _C_LAT + Q_LORA + KV_LORA + LANES
_KB_W = N_HEADS * LANES


def _mixer_in_body(layer_ref, x_ref, g_ref, win_ref, qg_ref, wuq_ref, kvg_ref, wukv_ref, tab_ref,
                   qa_ref, ka_ref, va_ref, qb_ref, kb_ref, vb_ref, kbar_ref):
    h = _rms(x_ref[...], g_ref[...]).astype(BF16)
    cos_a, sa_lo, sa_hi = tab_ref[:, 0:LANES], tab_ref[:, LANES:2 * LANES], tab_ref[:, 2 * LANES:3 * LANES]
    cos_b, sin_b = tab_ref[:, 3 * LANES:4 * LANES], tab_ref[:, 4 * LANES:5 * LANES]

    def rope_a(y):
        return (y * cos_a + pltpu.roll(y, LANES - HEAD_DIM // 2, axis=1) * sa_lo
                + pltpu.roll(y, HEAD_DIM // 2, axis=1) * sa_hi)

    def rope_b(y):
        return y * cos_b + pltpu.roll(y, HALF_LANES, axis=1) * sin_b

    scale_a = HEAD_DIM ** -0.5 * LOG2E
    scale_b = MLA_QK ** -0.5 * LOG2E
    lat = _dot(h, win_ref[:, _C_LAT:_C_G])
    y = _dot(h, win_ref[:, _C_QK:_C_QK + 2 * A_WIDTH])
    q_lat = _rms(lat[:, 0:Q_LORA], qg_ref[...]).astype(BF16)
    kv_lat = _rms(lat[:, Q_LORA:Q_LORA + KV_LORA], kvg_ref[...]).astype(BF16)
    n_blk = h.shape[0] // MOBA_BLOCK
    kbar_ref[...] = jnp.zeros_like(kbar_ref)
    for p in range(N_PAIRS):
        sl = slice(p * LANES, (p + 1) * LANES)
        qa_ref[:, sl] = (rope_a(y[:, sl]) * scale_a).astype(BF16)
        k = rope_a(y[:, A_WIDTH + p * LANES:A_WIDTH + (p + 1) * LANES])
        ka_ref[:, sl] = k.astype(BF16)
        for blk in range(n_blk):
            kbar_ref[blk:blk + 1, sl] = jnp.mean(k[blk * MOBA_BLOCK:(blk + 1) * MOBA_BLOCK], axis=0, keepdims=True)
    yq = _dot(q_lat, wuq_ref[...])
    ykv = _dot(kv_lat, wukv_ref[...])
    va_ref[...] = _dot(h, win_ref[:, _C_VA:_C_VA + A_WIDTH]).astype(BF16)
    k_pe = rope_b(lat[:, Q_LORA + KV_LORA:])
    for hd in range(N_HEADS):
        sl = slice(hd * LANES, (hd + 1) * LANES)
        qb_ref[:, sl] = (rope_b(yq[:, sl]) * scale_b).astype(BF16)
        kb_ref[:, sl] = (ykv[:, sl] + k_pe).astype(BF16)
    vb_ref[...] = ykv[:, _KB_W:].astype(BF16)


KBAR_ROWS = 8


def _mixer_in(layer, x, seq, norm_mix, w_in, q_lat_norm, w_uq, kv_lat_norm, w_ukv, tabs):
    t, d = x.shape
    tm = min(TOK_TILE, seq)
    tps = seq // tm
    b = t // seq
    assert tm // MOBA_BLOCK <= KBAR_ROWS
    tok = lambda w: pl.BlockSpec((tm, w), lambda i, l: (i, 0))
    weights = (norm_mix, w_in, q_lat_norm, w_uq, kv_lat_norm, w_ukv)
    tok_out = lambda w: jax.ShapeDtypeStruct((t, w), BF16)
    return pl.pallas_call(
        _mixer_in_body,
        out_shape=[tok_out(A_WIDTH), tok_out(A_WIDTH), tok_out(A_WIDTH), tok_out(_KB_W), tok_out(_KB_W), tok_out(A_WIDTH),
                   jax.ShapeDtypeStruct((b, tps, KBAR_ROWS, A_WIDTH), F32)],
        grid_spec=_layer_grid((t // tm,), [tok(d)] + [_layer_resident(w.shape) for w in weights] + [tok(TAB_W)],
                              [tok(A_WIDTH), tok(A_WIDTH), tok(A_WIDTH), tok(_KB_W), tok(_KB_W), tok(A_WIDTH),
                               pl.BlockSpec((None, None, KBAR_ROWS, A_WIDTH), lambda i, l: (i // tps, i % tps, 0, 0))]),
        compiler_params=_params(1),
        name="mixer_in",
    )(layer, x, *weights, tabs)


def _softmax_init(tq):
    return (jnp.full((tq, 1), NEG_BIG, F32), jnp.zeros((tq, LANES), F32))


def _softmax_step(s, v_ones, state):
    m, acc = state
    m_new = jnp.maximum(m, jnp.max(s, axis=-1, keepdims=True))
    p = jnp.exp2(s - m_new)
    acc_new = jnp.exp2(m - m_new) * acc + _dot(p.astype(BF16), v_ones)
    return m_new, acc_new


def _values_and_ones(v):
    lane = lax.broadcasted_iota(jnp.int32, v.shape, 1)
    one = jnp.ones_like(v)
    return jnp.where(lane < HEAD_DIM, v, one), jnp.where(lane < HEAD_DIM, one, v)


def _causal(s, q0, k0):
    row = lax.broadcasted_iota(jnp.int32, s.shape, 0) + q0
    col = lax.broadcasted_iota(jnp.int32, s.shape, 1) + k0
    return jnp.where(col <= row, s, NEG_BIG)


def _merge_pair(states):
    (_, acc0), (_, acc1) = states
    lane = lax.broadcasted_iota(jnp.int32, acc0.shape, 1)
    num = jnp.where(lane < HEAD_DIM, acc0, acc1)
    den = pltpu.roll(jnp.where(lane < HEAD_DIM, acc1, acc0), HEAD_DIM, axis=1)
    return num / den


def _moba_queries(q, kbar, qi, n_blocks):
    tq = q.shape[0]
    lane = lax.broadcasted_iota(jnp.int32, q.shape, 1)
    row = lax.broadcasted_iota(jnp.int32, (MAX_BLOCKS, tq), 0)
    col = lax.broadcasted_iota(jnp.int32, (MAX_BLOCKS, tq), 1)
    cur = qi * (tq // MOBA_BLOCK) + col // MOBA_BLOCK
    q_aug = []
    for hd in range(2):
        qh = jnp.where((lane < HEAD_DIM) == (hd == 0), q, jnp.zeros_like(q))
        gate = jnp.where(row < cur, _dot_nt(kbar, qh), -jnp.inf)
        rank = jnp.zeros(gate.shape, jnp.int32)
        for n in range(n_blocks):
            gn = gate[n:n + 1, :]
            beats = (gn > gate) | ((gn == gate) & (row > n))
            rank = rank + beats.astype(jnp.int32)
        keep = ((row < cur) & (rank < MOBA_TOPK)) | (row == cur)
        dropped = jnp.where(keep, 0.0, 1.0)
        dropped = jnp.concatenate([dropped, jnp.zeros((LANES - MAX_BLOCKS, tq), F32)], axis=0)
        q_aug.append(jnp.concatenate([qh, dropped.T.astype(BF16)], axis=1))
    return q_aug


def _attn_body(qa_ref, ka_ref, va_ref, e_ref, qb_ref, kb_ref, vb_ref, kbar_ref, oa_ref, ob_ref):
    tq = ATT_TILE
    seq = ka_ref.shape[0]
    n_blocks = seq // MOBA_BLOCK
    kbar = kbar_ref[...].astype(BF16)

    def tile(qa, qb, j, states, q0, causal):
        rows = slice(j * tq, (j + 1) * tq)
        ka_aug = jnp.concatenate([ka_ref[rows, :], e_ref[rows, :]], axis=1)
        va_ones = _values_and_ones(va_ref[rows, :])
        vb_ones = _values_and_ones(vb_ref[rows, :])
        mask = functools.partial(_causal, q0=q0, k0=j * tq) if causal else (lambda s: s)
        rows_q = qa[0].shape[0]
        mla = []
        for hd in range(2):
            s = _dot_nt(qb[hd], kb_ref[rows, hd * LANES:(hd + 1) * LANES])
            mla.append(_softmax_step(mask(s), vb_ones[hd], states[2 + hd]))
        s_both = _dot_nt(jnp.concatenate(qa, axis=0), ka_aug)
        moba = [_softmax_step(mask(s_both[hd * rows_q:(hd + 1) * rows_q]), va_ones[hd], states[hd])
                for hd in range(2)]
        return tuple(moba + mla)

    tq2 = 2 * tq
    qa_all = [_moba_queries(qa_ref[t * tq2:(t + 1) * tq2, :], kbar, t, n_blocks) for t in range(seq // tq2)]
    for t in range(seq // tq2):
        qrows = slice(t * tq2, (t + 1) * tq2)
        qa = qa_all[t]
        qb = [qb_ref[qrows, hd * LANES:(hd + 1) * LANES] for hd in range(2)]
        states = (_softmax_init(tq2),) * 4
        for j in range(2 * t + 1):
            states = tile(qa, qb, j, states, t * tq2, j == 2 * t)
        low = tuple((m[tq:], acc[tq:]) for m, acc in states)
        low = tile([q[tq:] for q in qa], [q[tq:] for q in qb], 2 * t + 1, low, t * tq2 + tq, True)
        states = tuple((jnp.concatenate([m[:tq], ml], axis=0), jnp.concatenate([acc[:tq], accl], axis=0))
                       for (m, acc), (ml, accl) in zip(states, low))
        oa_ref[qrows, :] = _merge_pair(states[:2]).astype(oa_ref.dtype)
        ob_ref[qrows, :] = _merge_pair(states[2:]).astype(ob_ref.dtype)


def _attention(qa, ka, va, e_mat, qb, kb, vb, kbar):
    b, s, _ = qa.shape
    seq_spec = lambda w: pl.BlockSpec((None, s, w), lambda i, p: (i, 0, p))
    return pl.pallas_call(
        _attn_body,
        out_shape=[jax.ShapeDtypeStruct((b, s, A_WIDTH), BF16)] * 2,
        grid=(b, N_PAIRS),
        in_specs=[seq_spec(LANES), seq_spec(LANES), seq_spec(LANES), _resident(e_mat.shape),
                  seq_spec(2 * LANES), seq_spec(2 * LANES), seq_spec(LANES),
                  pl.BlockSpec((None, None, MAX_BLOCKS, LANES), lambda i, p: (i, p, 0, 0))],
        out_specs=[seq_spec(LANES)] * 2,
        compiler_params=_params(2),
        name="attention",
    )(qa, ka, va, e_mat, qb, kb, vb, kbar)


def _mixer_out_body(layer_ref, x_ref, oa_ref, ob_ref, kv_ref, gm_ref, wg_ref, wa_ref, wb_ref, wo_ref,
                    gx_ref, wxq_ref, wxo_ref, o_ref):
    x = x_ref[...]
    d = x.shape[-1]
    h = _rms(x, gm_ref[...]).astype(BF16)
    merged = (jax.nn.sigmoid(_dot(h, wg_ref[:, :d])) * _dot(oa_ref[...], wa_ref[...])
              + jax.nn.sigmoid(_dot(h, wg_ref[:, d:])) * _dot(ob_ref[...], wb_ref[...]))
    x1 = x + _dot(merged.astype(BF16), wo_ref[...])

    q = _dot(_rms(x1, gx_ref[...]).astype(BF16), wxq_ref[...]).astype(BF16)
    scale = X_HEAD_DIM ** -0.5
    heads = []
    for hd in range(X_HEADS):
        lanes = slice(hd * X_HEAD_DIM, (hd + 1) * X_HEAD_DIM)
        s = _dot_nt(q[:, lanes], kv_ref[:, lanes])
        m = jnp.max(s, axis=-1, keepdims=True)
        p = jnp.exp((s - m) * scale)
        l = jnp.sum(p, axis=-1, keepdims=True)
        v = kv_ref[:, X_WIDTH + hd * X_HEAD_DIM:X_WIDTH + (hd + 1) * X_HEAD_DIM]
        heads.append((_dot(p.astype(BF16), v) * (1.0 / l)).astype(BF16))
    o_ref[...] = x1 + _dot(jnp.concatenate(heads, axis=1), wxo_ref[...])


def _mixer_out(layer, x, seq, oa, ob, mem_kv, norm_mix, w_gates, w_a, w_b, w_out, norm_xattn, w_xq, w_xo):
    t, d = x.shape
    tm = min(TOK_TILE, seq)
    tiles_per_seq = seq // tm
    m = mem_kv.shape[2]
    tok = lambda w: pl.BlockSpec((tm, w), lambda i, l: (i, 0))
    weights = (norm_mix, w_gates, w_a, w_b, w_out, norm_xattn, w_xq, w_xo)
    return pl.pallas_call(
        _mixer_out_body,
        out_shape=jax.ShapeDtypeStruct((t, d), F32),
        grid_spec=_layer_grid(
            (t // tm,),
            [tok(d), tok(A_WIDTH), tok(A_WIDTH),
             pl.BlockSpec((None, None, m, 2 * X_WIDTH), lambda i, l: (l[0], i // tiles_per_seq, 0, 0))]
            + [_layer_resident(w.shape) for w in weights],
            tok(d)),
        compiler_params=_params(1),
        name="mixer_out",
    )(layer, x, oa, ob, mem_kv, *weights)


FF_CHUNK = 1024


def _ffn_body(layer_ref, x_ref, g_ref, wgu_ref, wd_ref, gf_ref, o_ref, *, final):
    x = x_ref[...]
    h = _rms(x, g_ref[...]).astype(BF16)
    d_ff = wd_ref.shape[0]
    y = x
    for c0 in range(0, d_ff, FF_CHUNK):
        c1 = min(c0 + FF_CHUNK, d_ff)
        gt = _dot(h, wgu_ref[:, c0:c1])
        up = _dot(h, wgu_ref[:, d_ff + c0:d_ff + c1])
        a = (gt * jax.nn.sigmoid(gt) * up).astype(BF16)
        y = y + _dot(a, wd_ref[c0:c1, :])
    o_ref[...] = _rms(y, gf_ref[...]) if final else y


def _ffn(layer, x, norm_ffn, w_gate_up, w_down, norm_final, final):
    t, d = x.shape
    tm = min(TOK_TILE, t)
    tok = pl.BlockSpec((tm, d), lambda i, l: (i, 0))
    weights = (norm_ffn, w_gate_up, w_down)
    return pl.pallas_call(
        functools.partial(_ffn_body, final=final),
        out_shape=jax.ShapeDtypeStruct((t, d), F32),
        grid_spec=_layer_grid((t // tm,), [tok] + [_layer_resident(w.shape) for w in weights]
                              + [_resident(norm_final.shape)], tok),
        compiler_params=_params(1),
        name="ffn_final" if final else "ffn",
    )(layer, x, *weights, norm_final)


PREP_ROWS = 256


def _prep_w_in_body(wt_ref, o_ref, og_ref):
    in_w, rows = wt_ref.shape
    k0 = _C_LAT + Q_LORA + KV_LORA
    n_gate_groups = (in_w - k0 - MLA_ROPE) // LANES
    half = MLA_ROPE // 2
    lane = lax.broadcasted_iota(jnp.int32, (rows, LANES), 1)

    def cols(c0):
        n = min(LANES, in_w - c0)
        blk = wt_ref[c0:c0 + n, :]
        if n < LANES:
            blk = jnp.concatenate([blk, jnp.zeros((LANES - n, rows), blk.dtype)], axis=0)
        return blk.T

    for j in range(k0 // LANES):
        o_ref[:, j * LANES:(j + 1) * LANES] = cols(j * LANES).astype(BF16)
    g = cols(k0)
    kpe = jnp.where(lane < half, g, jnp.where((lane >= HALF_LANES) & (lane < HALF_LANES + half),
                                              pltpu.roll(g, HALF_LANES - half, axis=1), 0.0))
    o_ref[:, k0:k0 + LANES] = kpe.astype(BF16)
    a = g
    for j in range(n_gate_groups):
        b = cols(k0 + (j + 1) * LANES)
        shifted = pltpu.roll(jnp.where(lane >= MLA_ROPE, a, b), LANES - MLA_ROPE, axis=1)
        og_ref[:, j * LANES:(j + 1) * LANES] = shifted.astype(BF16)
        a = b


def _prep_w_in(w_in):
    n_layers, d, in_w = w_in.shape
    tr = min(PREP_ROWS, d)
    blk = lambda w: pl.BlockSpec((None, tr, w), lambda l, i: (l, i, 0))
    return pl.pallas_call(
        _prep_w_in_body,
        out_shape=[jax.ShapeDtypeStruct((n_layers, d, _C_G), BF16),
                   jax.ShapeDtypeStruct((n_layers, d, in_w - _C_G + LANES - MLA_ROPE), BF16)],
        grid=(n_layers, d // tr),
        in_specs=[pl.BlockSpec((None, in_w, tr), lambda l, i: (l, 0, i))],
        out_specs=[blk(_C_G), blk(in_w - _C_G + LANES - MLA_ROPE)],
        compiler_params=_params(2),
        name="prep_w_in",
    )(jnp.swapaxes(w_in, 1, 2))


def _mla_head_cols(nope, rope):
    half = MLA_ROPE // 2
    z = lambda n: jnp.zeros(nope.shape[:-1] + (n,), nope.dtype)
    t1, t2 = (z(half), z(half)) if rope is None else (rope[..., :half], rope[..., half:])
    return jnp.concatenate([t1, nope[..., :HEAD_DIM - half], t2, nope[..., HEAD_DIM - half:], z(LANES - MLA_QK)], axis=-1)


def _prep_weights(w_in, w_uq, w_ukv):
    n_layers = w_in.shape[0]
    w_in_r, w_gates = _prep_w_in(w_in)
    uq = w_uq.reshape(n_layers, Q_LORA, N_HEADS, MLA_QK)
    w_uq_r = _mla_head_cols(uq[..., :HEAD_DIM], uq[..., HEAD_DIM:]).reshape(n_layers, Q_LORA, _KB_W).astype(BF16)
    ukv = w_ukv.reshape(n_layers, KV_LORA, N_HEADS, 2 * HEAD_DIM)
    w_k = _mla_head_cols(ukv[..., :HEAD_DIM], None).reshape(n_layers, KV_LORA, _KB_W)
    w_v = ukv[..., HEAD_DIM:].reshape(n_layers, KV_LORA, A_WIDTH)
    w_ukv_r = jnp.concatenate([w_k, w_v], axis=-1).astype(BF16)
    return w_in_r, w_gates, w_uq_r, w_ukv_r


def kernel(x, mem, positions, norm_mix, w_in, q_lat_norm, w_uq, kv_lat_norm, w_ukv, w_branch_a, w_branch_b, w_out, norm_xattn, norm_mem, w_xq, w_xkv, w_xo, norm_ffn, w_gate_up, w_down, norm_final):
    b, s, d = x.shape
    n_layers = w_in.shape[0]
    t = b * s
    assert s % (2 * ATT_TILE) == 0 and ATT_TILE % MOBA_BLOCK == 0 and s // MOBA_BLOCK <= MAX_BLOCKS

    w_in_r, w_gates, w_uq_r, w_ukv_r = _prep_weights(w_in, w_uq, w_ukv)
    bf = lambda w: w.astype(BF16)
    row = lambda g: g.reshape(g.shape[0], 1, g.shape[1])
    tabs = _rope_tables(positions)
    mem_kv = _mem_kv(mem, row(norm_mem), bf(w_xkv))
    blk_of_row = np.arange(s) // MOBA_BLOCK
    e_mat = jnp.asarray(np.where(blk_of_row[:, None] == np.arange(LANES)[None, :], NEG_BIG, 0.0), BF16)
    norm_final = norm_final.reshape(1, d)

    norm_mix, q_lat_norm, kv_lat_norm = row(norm_mix), row(q_lat_norm), row(kv_lat_norm)
    norm_xattn, norm_ffn = row(norm_xattn), row(norm_ffn)
    w_a, w_b, w_o, w_q, w_xo_b = bf(w_branch_a), bf(w_branch_b), bf(w_out), bf(w_xq), bf(w_xo)
    w_gu, w_dn = bf(w_gate_up), bf(w_down)
    seq3 = lambda a: a.reshape(b, s, a.shape[-1])

    def layer(li, xt, final):
        li = jnp.full((1,), li, jnp.int32)
        qa, ka, va, qb, kb, vb, kbar = _mixer_in(li, xt, s, norm_mix, w_in_r, q_lat_norm, w_uq_r,
                                                   kv_lat_norm, w_ukv_r, tabs)
        n_blocks = s // MOBA_BLOCK
        kbar = kbar[:, :, :n_blocks // kbar.shape[1]].reshape(b, n_blocks, N_PAIRS, LANES)
        kbar = jnp.pad(kbar, ((0, 0), (0, MAX_BLOCKS - n_blocks), (0, 0), (0, 0))).transpose(0, 2, 1, 3)
        oa, ob = _attention(seq3(qa), seq3(ka), seq3(va), e_mat, seq3(qb), seq3(kb), seq3(vb), kbar)
        oa, ob = oa.reshape(t, A_WIDTH), ob.reshape(t, A_WIDTH)
        xt = _mixer_out(li, xt, s, oa, ob, mem_kv, norm_mix, w_gates, w_a, w_b, w_o, norm_xattn, w_q, w_xo_b)
        return _ffn(li, xt, norm_ffn, w_gu, w_dn, norm_final, final)

    xt = x.reshape(t, d)
    for li in range(n_layers):
        xt = layer(li, xt, li == n_layers - 1)
    return xt.reshape(b, s, d)
```

```python
import functools
import math

import numpy as np
import jax
import jax.numpy as jnp
from jax import lax
from jax.experimental import pallas as pl
from jax.experimental.pallas import tpu as pltpu

F32 = jnp.float32
BF16 = jnp.bfloat16

EPS = 1e-6
ROPE_THETA = 10000.0
LANES = 128
HALF_LANES = LANES // 2
HEAD_DIM = 64
N_HEADS = 8
N_PAIRS = N_HEADS // 2
A_WIDTH = N_HEADS * HEAD_DIM
MLA_ROPE = 32
MLA_QK = HEAD_DIM + MLA_ROPE
Q_LORA = 384
KV_LORA = 256
MOBA_BLOCK = 256
MOBA_TOPK = 3
MAX_BLOCKS = 16
X_HEADS = 4
X_HEAD_DIM = 128
X_WIDTH = X_HEADS * X_HEAD_DIM
NEG_BIG = -1e30
LOG2E = math.log2(math.e)
VMEM_LIMIT = 56 * 1024 * 1024
TOK_TILE = 1024
ATT_TILE = 512
TAB_W = 5 * LANES


def _resident(shape):
    return pl.BlockSpec(shape, lambda *_: (0,) * len(shape), pipeline_mode=pl.Buffered(1))


def _layer_resident(shape):
    return pl.BlockSpec((None,) + tuple(shape[1:]), lambda *a: (a[-1][0],) + (0,) * (len(shape) - 1),
                        pipeline_mode=pl.Buffered(1))


def _layer_grid(grid, in_specs, out_specs):
    return pltpu.PrefetchScalarGridSpec(num_scalar_prefetch=1, grid=grid, in_specs=in_specs, out_specs=out_specs)


def _params(n_axes):
    return pltpu.CompilerParams(dimension_semantics=("arbitrary",) * n_axes,
                                vmem_limit_bytes=VMEM_LIMIT)


def _rms(x, g):
    return x * lax.rsqrt(jnp.mean(x * x, axis=-1, keepdims=True) + EPS) * g


def _dot(a, b):
    return jnp.dot(a, b, preferred_element_type=F32)


def _dot_nt(a, b):
    return lax.dot_general(a, b, (((1,), (1,)), ((), ())), preferred_element_type=F32)


def _rope_table_body(pos_ref, tab_ref):
    pos = pos_ref[...]
    lane = lax.broadcasted_iota(jnp.int32, pos.shape, 1)
    sub = lane % HALF_LANES
    qtr = HEAD_DIM // 2
    inv = jnp.where(sub < qtr,
                    jnp.exp((-math.log(ROPE_THETA) * (2.0 / HEAD_DIM)) * sub.astype(F32)),
                    jnp.exp((-math.log(ROPE_THETA) * (2.0 / MLA_ROPE)) * (sub - qtr).astype(F32)))
    ang = pos * inv
    cos2, sin2 = jnp.cos(ang), jnp.sin(ang)

    def spread(v):
        r1, r3 = pltpu.roll(v, qtr, axis=1), pltpu.roll(v, 3 * qtr, axis=1)
        tiled = jnp.where(lane < qtr, v, jnp.where(lane < 2 * qtr, r1, jnp.where(
            lane < 3 * qtr, pltpu.roll(v, 2 * qtr, axis=1), r3)))
        return tiled, jnp.where(lane < HALF_LANES, r3, r1)

    first_half = sub < qtr
    is_rope = sub < MLA_ROPE // 2
    for half in range(2):
        cos, sin = (cos2, sin2) if half == 0 else (pltpu.roll(cos2, HALF_LANES, axis=1),
                                                               pltpu.roll(sin2, HALF_LANES, axis=1))
        cos_a, cos_b = spread(cos)
        sin_a, sin_b = spread(sin)
        tab_ref[half, :, 0:LANES] = cos_a
        tab_ref[half, :, LANES:2 * LANES] = jnp.where(first_half, -sin_a, 0.0)
        tab_ref[half, :, 2 * LANES:3 * LANES] = jnp.where(first_half, 0.0, sin_a)
        tab_ref[half, :, 3 * LANES:4 * LANES] = jnp.where(is_rope, cos_b, 1.0)
        tab_ref[half, :, 4 * LANES:5 * LANES] = jnp.where(is_rope, jnp.where(lane < HALF_LANES, -sin_b, sin_b), 0.0)


def _rope_tables(positions):
    t = positions.size
    half = t // 2
    pos = positions.reshape(2, half, 1).astype(F32)
    pos2 = jnp.concatenate([jnp.broadcast_to(pos[0], (half, HALF_LANES)),
                            jnp.broadcast_to(pos[1], (half, HALF_LANES))], axis=1)
    tm = min(1024, half)
    tabs = pl.pallas_call(
        _rope_table_body,
        out_shape=jax.ShapeDtypeStruct((2, half, TAB_W), F32),
        grid=(half // tm,),
        in_specs=[pl.BlockSpec((tm, LANES), lambda i: (i, 0))],
        out_specs=pl.BlockSpec((2, tm, TAB_W), lambda i: (0, i, 0)),
        compiler_params=_params(1),
        name="rope_tables",
    )(pos2)
    return tabs.reshape(t, TAB_W)


MEM_BATCH = 8


def _mem_kv_body(mem_ref, g_ref, w_ref, o_ref):
    nb, m, d = mem_ref.shape
    h = _rms(mem_ref[...].reshape(nb * m, d), g_ref[...]).astype(BF16)
    o_ref[...] = _dot(h, w_ref[...]).astype(BF16).reshape(o_ref.shape)


def _mem_kv(mem, norm_mem, w_xkv):
    b, m, d = mem.shape
    n_layers = w_xkv.shape[0]
    nb = math.gcd(b, MEM_BATCH)
    return pl.pallas_call(
        _mem_kv_body,
        out_shape=jax.ShapeDtypeStruct((n_layers, b, m, 2 * X_WIDTH), BF16),
        grid=(n_layers, b // nb),
        in_specs=[pl.BlockSpec((nb, m, d), lambda l, i: (i, 0, 0)),
                  pl.BlockSpec((None, 1, d), lambda l, i: (l, 0, 0)),
                  pl.BlockSpec((None, d, 2 * X_WIDTH), lambda l, i: (l, 0, 0))],
        out_specs=pl.BlockSpec((None, nb, m, 2 * X_WIDTH), lambda l, i: (l, i, 0, 0)),
        compiler_params=_params(2),
        name="mem_kv",
    )(mem, norm_mem, w_xkv)


_C_QK = 0
_C_VA = 2 * A_WIDTH
_C_LAT = 3 * A_WIDTH
_C_G = _C_LAT + Q_LORA + KV_LORA + LANES
_KB_W = N_HEADS * LANES


def _mixer_in_body(layer_ref, x_ref, g_ref, win_ref, qg_ref, wuq_ref, kvg_ref, wukv_ref, tab_ref,
                   qa_ref, ka_ref, va_ref, qb_ref, kb_ref, vb_ref, kbar_ref, h_ref):
    h = _rms(x_ref[...], g_ref[...]).astype(BF16)
    h_ref[...] = h
    cos_a, sa_lo, sa_hi = tab_ref[:, 0:LANES], tab_ref[:, LANES:2 * LANES], tab_ref[:, 2 * LANES:3 * LANES]
    cos_b, sin_b = tab_ref[:, 3 * LANES:4 * LANES], tab_ref[:, 4 * LANES:5 * LANES]

    def rope_a(y):
        return (y * cos_a + pltpu.roll(y, LANES - HEAD_DIM // 2, axis=1) * sa_lo
                + pltpu.roll(y, HEAD_DIM // 2, axis=1) * sa_hi)

    def rope_b(y):
        return y * cos_b + pltpu.roll(y, HALF_LANES, axis=1) * sin_b

    scale_a = HEAD_DIM ** -0.5 * LOG2E
    scale_b = MLA_QK ** -0.5 * LOG2E
    lat = _dot(h, win_ref[:, _C_LAT:_C_G])
    y = _dot(h, win_ref[:, _C_QK:_C_QK + 2 * A_WIDTH])
    q_lat = _rms(lat[:, 0:Q_LORA], qg_ref[...]).astype(BF16)
    kv_lat = _rms(lat[:, Q_LORA:Q_LORA + KV_LORA], kvg_ref[...]).astype(BF16)
    n_blk = h.shape[0] // MOBA_BLOCK
    kbar_ref[...] = jnp.zeros_like(kbar_ref)
    for p in range(N_PAIRS):
        sl = slice(p * LANES, (p + 1) * LANES)
        qa_ref[:, sl] = (rope_a(y[:, sl]) * scale_a).astype(BF16)
        k = rope_a(y[:, A_WIDTH + p * LANES:A_WIDTH + (p + 1) * LANES])
        ka_ref[:, sl] = k.astype(BF16)
        for blk in range(n_blk):
            kbar_ref[blk:blk + 1, sl] = jnp.mean(k[blk * MOBA_BLOCK:(blk + 1) * MOBA_BLOCK], axis=0, keepdims=True)
    yq = _dot(q_lat, wuq_ref[...])
    ykv = _dot(kv_lat, wukv_ref[...])
    va_ref[...] = _dot(h, win_ref[:, _C_VA:_C_VA + A_WIDTH]).astype(BF16)
    k_pe = rope_b(lat[:, Q_LORA + KV_LORA:])
    for hd in range(N_HEADS):
        sl = slice(hd * LANES, (hd + 1) * LANES)
        qb_ref[:, sl] = (rope_b(yq[:, sl]) * scale_b).astype(BF16)
        kb_ref[:, sl] = (ykv[:, sl] + k_pe).astype(BF16)
    vb_ref[...] = ykv[:, _KB_W:].astype(BF16)


KBAR_ROWS = 8


def _mixer_in(layer, x, seq, norm_mix, w_in, q_lat_norm, w_uq, kv_lat_norm, w_ukv, tabs):
    t, d = x.shape
    tm = min(TOK_TILE, seq)
    tps = seq // tm
    b = t // seq
    assert tm // MOBA_BLOCK <= KBAR_ROWS
    tok = lambda w: pl.BlockSpec((tm, w), lambda i, l: (i, 0))
    weights = (norm_mix, w_in, q_lat_norm, w_uq, kv_lat_norm, w_ukv)
    tok_out = lambda w: jax.ShapeDtypeStruct((t, w), BF16)
    return pl.pallas_call(
        _mixer_in_body,
        out_shape=[tok_out(A_WIDTH), tok_out(A_WIDTH), tok_out(A_WIDTH), tok_out(_KB_W), tok_out(_KB_W), tok_out(A_WIDTH),
                   jax.ShapeDtypeStruct((b, tps, KBAR_ROWS, A_WIDTH), F32), tok_out(d)],
        grid_spec=_layer_grid((t // tm,), [tok(d)] + [_layer_resident(w.shape) for w in weights] + [tok(TAB_W)],
                              [tok(A_WIDTH), tok(A_WIDTH), tok(A_WIDTH), tok(_KB_W), tok(_KB_W), tok(A_WIDTH),
                               pl.BlockSpec((None, None, KBAR_ROWS, A_WIDTH), lambda i, l: (i // tps, i % tps, 0, 0)),
                               tok(d)]),
        compiler_params=_params(1),
        name="mixer_in",
    )(layer, x, *weights, tabs)


def _softmax_init(tq):
    return (jnp.full((tq, 1), NEG_BIG, F32), jnp.zeros((tq, LANES), F32))


def _softmax_step(s, v_ones, state):
    m, acc = state
    m_new = jnp.maximum(m, jnp.max(s, axis=-1, keepdims=True))
    p = jnp.exp2(s - m_new)
    acc_new = jnp.exp2(m - m_new) * acc + _dot(p.astype(BF16), v_ones)
    return m_new, acc_new


def _values_and_ones(v):
    lane = lax.broadcasted_iota(jnp.int32, v.shape, 1)
    one = jnp.ones_like(v)
    return jnp.where(lane < HEAD_DIM, v, one), jnp.where(lane < HEAD_DIM, one, v)


def _causal(s, q0, k0):
    row = lax.broadcasted_iota(jnp.int32, s.shape, 0) + q0
    col = lax.broadcasted_iota(jnp.int32, s.shape, 1) + k0
    return jnp.where(col <= row, s, NEG_BIG)


def _merge_pair(states):
    (_, acc0), (_, acc1) = states
    lane = lax.broadcasted_iota(jnp.int32, acc0.shape, 1)
    num = jnp.where(lane < HEAD_DIM, acc0, acc1)
    den = pltpu.roll(jnp.where(lane < HEAD_DIM, acc1, acc0), HEAD_DIM, axis=1)
    return num / den


def _moba_queries(q, kbar, qi, n_blocks):
    tq = q.shape[0]
    lane = lax.broadcasted_iota(jnp.int32, q.shape, 1)
    row = lax.broadcasted_iota(jnp.int32, (MAX_BLOCKS, tq), 0)
    col = lax.broadcasted_iota(jnp.int32, (MAX_BLOCKS, tq), 1)
    cur = qi * (tq // MOBA_BLOCK) + col // MOBA_BLOCK
    q_aug = []
    for hd in range(2):
        qh = jnp.where((lane < HEAD_DIM) == (hd == 0), q, jnp.zeros_like(q))
        gate = jnp.where(row < cur, _dot_nt(kbar, qh), -jnp.inf)
        rank = jnp.zeros(gate.shape, jnp.int32)
        for n in range(n_blocks):
            gn = gate[n:n + 1, :]
            beats = (gn > gate) | ((gn == gate) & (row > n))
            rank = rank + beats.astype(jnp.int32)
        keep = ((row < cur) & (rank < MOBA_TOPK)) | (row == cur)
        dropped = jnp.where(keep, 0.0, 1.0)
        dropped = jnp.concatenate([dropped, jnp.zeros((LANES - MAX_BLOCKS, tq), F32)], axis=0)
        q_aug.append(jnp.concatenate([qh, dropped.T.astype(BF16)], axis=1))
    return q_aug


def _attn_body(qa_ref, ka_ref, va_ref, e_ref, qb_ref, kb_ref, vb_ref, kbar_ref, oa_ref, ob_ref):
    tq = ATT_TILE
    seq = ka_ref.shape[0]
    n_blocks = seq // MOBA_BLOCK
    kbar = kbar_ref[...].astype(BF16)

    def tile(qa, qb, j, states, q0, causal):
        rows = slice(j * tq, (j + 1) * tq)
        ka_aug = jnp.concatenate([ka_ref[rows, :], e_ref[rows, :]], axis=1)
        va_ones = _values_and_ones(va_ref[rows, :])
        vb_ones = _values_and_ones(vb_ref[rows, :])
        mask = functools.partial(_causal, q0=q0, k0=j * tq) if causal else (lambda s: s)
        rows_q = qa[0].shape[0]
        mla = []
        for hd in range(2):
            s = _dot_nt(qb[hd], kb_ref[rows, hd * LANES:(hd + 1) * LANES])
            mla.append(_softmax_step(mask(s), vb_ones[hd], states[2 + hd]))
        s_both = _dot_nt(jnp.concatenate(qa, axis=0), ka_aug)
        moba = [_softmax_step(mask(s_both[hd * rows_q:(hd + 1) * rows_q]), va_ones[hd], states[hd])
                for hd in range(2)]
        return tuple(moba + mla)

    tq2 = 2 * tq
    qa_all = [_moba_queries(qa_ref[t * tq2:(t + 1) * tq2, :], kbar, t, n_blocks) for t in range(seq // tq2)]
    for t in range(seq // tq2):
        qrows = slice(t * tq2, (t + 1) * tq2)
        qa = qa_all[t]
        qb = [qb_ref[qrows, hd * LANES:(hd + 1) * LANES] for hd in range(2)]
        states = (_softmax_init(tq2),) * 4
        for j in range(2 * t + 1):
            states = tile(qa, qb, j, states, t * tq2, j == 2 * t)
        low = tuple((m[tq:], acc[tq:]) for m, acc in states)
        low = tile([q[tq:] for q in qa], [q[tq:] for q in qb], 2 * t + 1, low, t * tq2 + tq, True)
        states = tuple((jnp.concatenate([m[:tq], ml], axis=0), jnp.concatenate([acc[:tq], accl], axis=0))
                       for (m, acc), (ml, accl) in zip(states, low))
        oa_ref[qrows, :] = _merge_pair(states[:2]).astype(oa_ref.dtype)
        ob_ref[qrows, :] = _merge_pair(states[2:]).astype(ob_ref.dtype)


def _attention(qa, ka, va, e_mat, qb, kb, vb, kbar):
    b, s, _ = qa.shape
    seq_spec = lambda w: pl.BlockSpec((None, s, w), lambda i, p: (i, 0, p))
    return pl.pallas_call(
        _attn_body,
        out_shape=[jax.ShapeDtypeStruct((b, s, A_WIDTH), BF16)] * 2,
        grid=(b, N_PAIRS),
        in_specs=[seq_spec(LANES), seq_spec(LANES), seq_spec(LANES), _resident(e_mat.shape),
                  seq_spec(2 * LANES), seq_spec(2 * LANES), seq_spec(LANES),
                  pl.BlockSpec((None, None, MAX_BLOCKS, LANES), lambda i, p: (i, p, 0, 0))],
        out_specs=[seq_spec(LANES)] * 2,
        compiler_params=_params(2),
        name="attention",
    )(qa, ka, va, e_mat, qb, kb, vb, kbar)


def _mixer_out_body(layer_ref, x_ref, h_ref, oa_ref, ob_ref, kv_ref, wg_ref, wa_ref, wb_ref, wo_ref,
                    gx_ref, wxq_ref, wxo_ref, o_ref):
    x = x_ref[...]
    d = x.shape[-1]
    h = h_ref[...]
    merged = (jax.nn.sigmoid(_dot(h, wg_ref[:, :d])) * _dot(oa_ref[...], wa_ref[...])
              + jax.nn.sigmoid(_dot(h, wg_ref[:, d:])) * _dot(ob_ref[...], wb_ref[...]))
    x1 = x + _dot(merged.astype(BF16), wo_ref[...])

    q = _dot(_rms(x1, gx_ref[...]).astype(BF16), wxq_ref[...]).astype(BF16)
    scale = X_HEAD_DIM ** -0.5
    heads = []
    for hd in range(X_HEADS):
        lanes = slice(hd * X_HEAD_DIM, (hd + 1) * X_HEAD_DIM)
        s = _dot_nt(q[:, lanes], kv_ref[:, lanes])
        m = jnp.max(s, axis=-1, keepdims=True)
        p = jnp.exp((s - m) * scale)
        l = jnp.sum(p, axis=-1, keepdims=True)
        v = kv_ref[:, X_WIDTH + hd * X_HEAD_DIM:X_WIDTH + (hd + 1) * X_HEAD_DIM]
        heads.append((_dot(p.astype(BF16), v) * (1.0 / l)).astype(BF16))
    o_ref[...] = x1 + _dot(jnp.concatenate(heads, axis=1), wxo_ref[...])


def _mixer_out(layer, x, h, seq, oa, ob, mem_kv, w_gates, w_a, w_b, w_out, norm_xattn, w_xq, w_xo):
    t, d = x.shape
    tm = min(TOK_TILE, seq)
    tiles_per_seq = seq // tm
    m = mem_kv.shape[2]
    tok = lambda w: pl.BlockSpec((tm, w), lambda i, l: (i, 0))
    weights = (w_gates, w_a, w_b, w_out, norm_xattn, w_xq, w_xo)
    return pl.pallas_call(
        _mixer_out_body,
        out_shape=jax.ShapeDtypeStruct((t, d), F32),
        grid_spec=_layer_grid(
            (t // tm,),
            [tok(d), tok(d), tok(A_WIDTH), tok(A_WIDTH),
             pl.BlockSpec((None, None, m, 2 * X_WIDTH), lambda i, l: (l[0], i // tiles_per_seq, 0, 0))]
            + [_layer_resident(w.shape) for w in weights],
            tok(d)),
        compiler_params=_params(1),
        name="mixer_out",
    )(layer, x, h, oa, ob, mem_kv, *weights)


FF_CHUNK = 1024


def _ffn_body(layer_ref, x_ref, g_ref, wgu_ref, wd_ref, gf_ref, o_ref, *, final):
    x = x_ref[...]
    h = _rms(x, g_ref[...]).astype(BF16)
    d_ff = wd_ref.shape[0]
    y = x
    for c0 in range(0, d_ff, FF_CHUNK):
        c1 = min(c0 + FF_CHUNK, d_ff)
        gt = _dot(h, wgu_ref[:, c0:c1])
        up = _dot(h, wgu_ref[:, d_ff + c0:d_ff + c1])
        a = (gt * jax.nn.sigmoid(gt) * up).astype(BF16)
        y = y + _dot(a, wd_ref[c0:c1, :])
    o_ref[...] = _rms(y, gf_ref[...]) if final else y


def _ffn(layer, x, norm_ffn, w_gate_up, w_down, norm_final, final):
    t, d = x.shape
    tm = min(TOK_TILE, t)
    tok = pl.BlockSpec((tm, d), lambda i, l: (i, 0))
    weights = (norm_ffn, w_gate_up, w_down)
    return pl.pallas_call(
        functools.partial(_ffn_body, final=final),
        out_shape=jax.ShapeDtypeStruct((t, d), F32),
        grid_spec=_layer_grid((t // tm,), [tok] + [_layer_resident(w.shape) for w in weights]
                              + [_resident(norm_final.shape)], tok),
        compiler_params=_params(1),
        name="ffn_final" if final else "ffn",
    )(layer, x, *weights, norm_final)


PREP_ROWS = 256


def _prep_w_in_body(wt_ref, o_ref, og_ref):
    in_w, rows = wt_ref.shape
    k0 = _C_LAT + Q_LORA + KV_LORA
    n_gate_groups = (in_w - k0 - MLA_ROPE) // LANES
    half = MLA_ROPE // 2
    lane = lax.broadcasted_iota(jnp.int32, (rows, LANES), 1)

    def cols(c0):
        n = min(LANES, in_w - c0)
        blk = wt_ref[c0:c0 + n, :]
        if n < LANES:
            blk = jnp.concatenate([blk, jnp.zeros((LANES - n, rows), blk.dtype)], axis=0)
        return blk.T

    for j in range(k0 // LANES):
        o_ref[:, j * LANES:(j + 1) * LANES] = cols(j * LANES).astype(BF16)
    g = cols(k0)
    kpe = jnp.where(lane < half, g, jnp.where((lane >= HALF_LANES) & (lane < HALF_LANES + half),
                                              pltpu.roll(g, HALF_LANES - half, axis=1), 0.0))
    o_ref[:, k0:k0 + LANES] = kpe.astype(BF16)
    a = g
    for j in range(n_gate_groups):
        b = cols(k0 + (j + 1) * LANES)
        shifted = pltpu.roll(jnp.where(lane >= MLA_ROPE, a, b), LANES - MLA_ROPE, axis=1)
        og_ref[:, j * LANES:(j + 1) * LANES] = shifted.astype(BF16)
        a = b


def _prep_w_in(w_in):
    n_layers, d, in_w = w_in.shape
    tr = min(PREP_ROWS, d)
    blk = lambda w: pl.BlockSpec((None, tr, w), lambda l, i: (l, i, 0))
    return pl.pallas_call(
        _prep_w_in_body,
        out_shape=[jax.ShapeDtypeStruct((n_layers, d, _C_G), BF16),
                   jax.ShapeDtypeStruct((n_layers, d, in_w - _C_G + LANES - MLA_ROPE), BF16)],
        grid=(n_layers, d // tr),
        in_specs=[pl.BlockSpec((None, in_w, tr), lambda l, i: (l, 0, i))],
        out_specs=[blk(_C_G), blk(in_w - _C_G + LANES - MLA_ROPE)],
        compiler_params=_params(2),
        name="prep_w_in",
    )(jnp.swapaxes(w_in, 1, 2))


def _mla_head_cols(nope, rope):
    half = MLA_ROPE // 2
    z = lambda n: jnp.zeros(nope.shape[:-1] + (n,), nope.dtype)
    t1, t2 = (z(half), z(half)) if rope is None else (rope[..., :half], rope[..., half:])
    return jnp.concatenate([t1, nope[..., :HEAD_DIM - half], t2, nope[..., HEAD_DIM - half:], z(LANES - MLA_QK)], axis=-1)


def _prep_weights(w_in, w_uq, w_ukv):
    n_layers = w_in.shape[0]
    w_in_r, w_gates = _prep_w_in(w_in)
    uq = w_uq.reshape(n_layers, Q_LORA, N_HEADS, MLA_QK)
    w_uq_r = _mla_head_cols(uq[..., :HEAD_DIM], uq[..., HEAD_DIM:]).reshape(n_layers, Q_LORA, _KB_W).astype(BF16)
    ukv = w_ukv.reshape(n_layers, KV_LORA, N_HEADS, 2 * HEAD_DIM)
    w_k = _mla_head_cols(ukv[..., :HEAD_DIM], None).reshape(n_layers, KV_LORA, _KB_W)
    w_v = ukv[..., HEAD_DIM:].reshape(n_layers, KV_LORA, A_WIDTH)
    w_ukv_r = jnp.concatenate([w_k, w_v], axis=-1).astype(BF16)
    return w_in_r, w_gates, w_uq_r, w_ukv_r


def kernel(x, mem, positions, norm_mix, w_in, q_lat_norm, w_uq, kv_lat_norm, w_ukv, w_branch_a, w_branch_b, w_out, norm_xattn, norm_mem, w_xq, w_xkv, w_xo, norm_ffn, w_gate_up, w_down, norm_final):
    b, s, d = x.shape
    n_layers = w_in.shape[0]
    t = b * s
    assert s % (2 * ATT_TILE) == 0 and ATT_TILE % MOBA_BLOCK == 0 and s // MOBA_BLOCK <= MAX_BLOCKS

    w_in_r, w_gates, w_uq_r, w_ukv_r = _prep_weights(w_in, w_uq, w_ukv)
    bf = lambda w: w.astype(BF16)
    row = lambda g: g.reshape(g.shape[0], 1, g.shape[1])
    tabs = _rope_tables(positions)
    mem_kv = _mem_kv(mem, row(norm_mem), bf(w_xkv))
    blk_of_row = np.arange(s) // MOBA_BLOCK
    e_mat = jnp.asarray(np.where(blk_of_row[:, None] == np.arange(LANES)[None, :], NEG_BIG, 0.0), BF16)
    norm_final = norm_final.reshape(1, d)

    norm_mix, q_lat_norm, kv_lat_norm = row(norm_mix), row(q_lat_norm), row(kv_lat_norm)
    norm_xattn, norm_ffn = row(norm_xattn), row(norm_ffn)
    w_a, w_b, w_o, w_q, w_xo_b = bf(w_branch_a), bf(w_branch_b), bf(w_out), bf(w_xq), bf(w_xo)
    w_gu, w_dn = bf(w_gate_up), bf(w_down)
    seq3 = lambda a: a.reshape(b, s, a.shape[-1])

    def layer(li, xt, final):
        li = jnp.full((1,), li, jnp.int32)
        qa, ka, va, qb, kb, vb, kbar, h = _mixer_in(li, xt, s, norm_mix, w_in_r, q_lat_norm, w_uq_r,
                                                   kv_lat_norm, w_ukv_r, tabs)
        n_blocks = s // MOBA_BLOCK
        kbar = kbar[:, :, :n_blocks // kbar.shape[1]].reshape(b, n_blocks, N_PAIRS, LANES)
        kbar = jnp.pad(kbar, ((0, 0), (0, MAX_BLOCKS - n_blocks), (0, 0), (0, 0))).transpose(0, 2, 1, 3)
        oa, ob = _attention(seq3(qa), seq3(ka), seq3(va), e_mat, seq3(qb), seq3(kb), seq3(vb), kbar)
        oa, ob = oa.reshape(t, A_WIDTH), ob.reshape(t, A_WIDTH)
        xt = _mixer_out(li, xt, h, s, oa, ob, mem_kv, w_gates, w_a, w_b, w_o, norm_xattn, w_q, w_xo_b)
        return _ffn(li, xt, norm_ffn, w_gu, w_dn, norm_final, final)

    xt = x.reshape(t, d)
    for li in range(n_layers):
        xt = layer(li, xt, li == n_layers - 1)
    return xt.reshape(b, s, d)
```

```python
import functools
import math

import numpy as np
import jax
import jax.numpy as jnp
from jax import lax
from jax.experimental import pallas as pl
from jax.experimental.pallas import tpu as pltpu

F32 = jnp.float32
BF16 = jnp.bfloat16

EPS = 1e-6
ROPE_THETA = 10000.0
LANES = 128
HALF_LANES = LANES // 2
HEAD_DIM = 64
N_HEADS = 8
N_PAIRS = N_HEADS // 2
A_WIDTH = N_HEADS * HEAD_DIM
MLA_ROPE = 32
MLA_QK = HEAD_DIM + MLA_ROPE
Q_LORA = 384
KV_LORA = 256
MOBA_BLOCK = 256
MOBA_TOPK = 3
MAX_BLOCKS = 16
X_HEADS = 4
X_HEAD_DIM = 128
X_WIDTH = X_HEADS * X_HEAD_DIM
NEG_BIG = -1e30
LOG2E = math.log2(math.e)
VMEM_LIMIT = 56 * 1024 * 1024
TOK_TILE = 1024
ATT_TILE = 512
TAB_W = 5 * LANES


def _resident(shape):
    return pl.BlockSpec(shape, lambda *_: (0,) * len(shape), pipeline_mode=pl.Buffered(1))


def _layer_resident(shape):
    return pl.BlockSpec((None,) + tuple(shape[1:]), lambda *a: (a[-1][0],) + (0,) * (len(shape) - 1),
                        pipeline_mode=pl.Buffered(1))


def _layer_grid(grid, in_specs, out_specs):
    return pltpu.PrefetchScalarGridSpec(num_scalar_prefetch=1, grid=grid, in_specs=in_specs, out_specs=out_specs)


def _params(n_axes):
    return pltpu.CompilerParams(dimension_semantics=("arbitrary",) * n_axes,
                                vmem_limit_bytes=VMEM_LIMIT)


def _rms(x, g):
    return x * lax.rsqrt(jnp.mean(x * x, axis=-1, keepdims=True) + EPS) * g


def _dot(a, b):
    return jnp.dot(a, b, preferred_element_type=F32)


def _dot_nt(a, b):
    return lax.dot_general(a, b, (((1,), (1,)), ((), ())), preferred_element_type=F32)


def _rope_table_body(pos_ref, tab_ref):
    pos = pos_ref[...]
    lane = lax.broadcasted_iota(jnp.int32, pos.shape, 1)
    sub = lane % HALF_LANES
    qtr = HEAD_DIM // 2
    inv = jnp.where(sub < qtr,
                    jnp.exp((-math.log(ROPE_THETA) * (2.0 / HEAD_DIM)) * sub.astype(F32)),
                    jnp.exp((-math.log(ROPE_THETA) * (2.0 / MLA_ROPE)) * (sub - qtr).astype(F32)))
    ang = pos * inv
    cos2, sin2 = jnp.cos(ang), jnp.sin(ang)

    def spread(v):
        r1, r3 = pltpu.roll(v, qtr, axis=1), pltpu.roll(v, 3 * qtr, axis=1)
        tiled = jnp.where(lane < qtr, v, jnp.where(lane < 2 * qtr, r1, jnp.where(
            lane < 3 * qtr, pltpu.roll(v, 2 * qtr, axis=1), r3)))
        return tiled, jnp.where(lane < HALF_LANES, r3, r1)

    first_half = sub < qtr
    is_rope = sub < MLA_ROPE // 2
    for half in range(2):
        cos, sin = (cos2, sin2) if half == 0 else (pltpu.roll(cos2, HALF_LANES, axis=1),
                                                               pltpu.roll(sin2, HALF_LANES, axis=1))
        cos_a, cos_b = spread(cos)
        sin_a, sin_b = spread(sin)
        tab_ref[half, :, 0:LANES] = cos_a
        tab_ref[half, :, LANES:2 * LANES] = jnp.where(first_half, -sin_a, 0.0)
        tab_ref[half, :, 2 * LANES:3 * LANES] = jnp.where(first_half, 0.0, sin_a)
        tab_ref[half, :, 3 * LANES:4 * LANES] = jnp.where(is_rope, cos_b, 1.0)
        tab_ref[half, :, 4 * LANES:5 * LANES] = jnp.where(is_rope, jnp.where(lane < HALF_LANES, -sin_b, sin_b), 0.0)


def _rope_tables(positions):
    t = positions.size
    half = t // 2
    pos = positions.reshape(2, half, 1).astype(F32)
    pos2 = jnp.concatenate([jnp.broadcast_to(pos[0], (half, HALF_LANES)),
                            jnp.broadcast_to(pos[1], (half, HALF_LANES))], axis=1)
    tm = min(1024, half)
    tabs = pl.pallas_call(
        _rope_table_body,
        out_shape=jax.ShapeDtypeStruct((2, half, TAB_W), F32),
        grid=(half // tm,),
        in_specs=[pl.BlockSpec((tm, LANES), lambda i: (i, 0))],
        out_specs=pl.BlockSpec((2, tm, TAB_W), lambda i: (0, i, 0)),
        compiler_params=_params(1),
        name="rope_tables",
    )(pos2)
    return tabs.reshape(t, TAB_W)


MEM_BATCH = 8


def _mem_kv_body(mem_ref, g_ref, w_ref, o_ref):
    nb, m, d = mem_ref.shape
    h = _rms(mem_ref[...].reshape(nb * m, d), g_ref[...]).astype(BF16)
    o_ref[...] = _dot(h, w_ref[...]).astype(BF16).reshape(o_ref.shape)


def _mem_kv(mem, norm_mem, w_xkv):
    b, m, d = mem.shape
    n_layers = w_xkv.shape[0]
    nb = math.gcd(b, MEM_BATCH)
    return pl.pallas_call(
        _mem_kv_body,
        out_shape=jax.ShapeDtypeStruct((n_layers, b, m, 2 * X_WIDTH), BF16),
        grid=(n_layers, b // nb),
        in_specs=[pl.BlockSpec((nb, m, d), lambda l, i: (i, 0, 0)),
                  pl.BlockSpec((None, 1, d), lambda l, i: (l, 0, 0)),
                  pl.BlockSpec((None, d, 2 * X_WIDTH), lambda l, i: (l, 0, 0))],
        out_specs=pl.BlockSpec((None, nb, m, 2 * X_WIDTH), lambda l, i: (l, i, 0, 0)),
        compiler_params=_params(2),
        name="mem_kv",
    )(mem, norm_mem, w_xkv)


_C_QK = 0
_C_VA = 2 * A_WIDTH
_C_LAT = 3 * A_WIDTH
_C_G = _C_LAT + Q_LORA + KV_LORA + LANES
_KB_W = N_HEADS * LANES


def _mixer_in_body(layer_ref, x_ref, g_ref, win_ref, qg_ref, wuq_ref, kvg_ref, wukv_ref, tab_ref,
                   qa_ref, ka_ref, va_ref, qb_ref, kb_ref, vb_ref, kbar_ref, h_ref):
    h = _rms(x_ref[...], g_ref[...]).astype(BF16)
    h_ref[...] = h
    cos_a, sa_lo, sa_hi = tab_ref[:, 0:LANES], tab_ref[:, LANES:2 * LANES], tab_ref[:, 2 * LANES:3 * LANES]
    cos_b, sin_b = tab_ref[:, 3 * LANES:4 * LANES], tab_ref[:, 4 * LANES:5 * LANES]

    def rope_a(y):
        return (y * cos_a + pltpu.roll(y, LANES - HEAD_DIM // 2, axis=1) * sa_lo
                + pltpu.roll(y, HEAD_DIM // 2, axis=1) * sa_hi)

    def rope_b(y):
        return y * cos_b + pltpu.roll(y, HALF_LANES, axis=1) * sin_b

    scale_a = HEAD_DIM ** -0.5 * LOG2E
    scale_b = MLA_QK ** -0.5 * LOG2E
    lat = _dot(h, win_ref[:, _C_LAT:_C_G])
    y = _dot(h, win_ref[:, _C_QK:_C_QK + 2 * A_WIDTH])
    q_lat = _rms(lat[:, 0:Q_LORA], qg_ref[...]).astype(BF16)
    kv_lat = _rms(lat[:, Q_LORA:Q_LORA + KV_LORA], kvg_ref[...]).astype(BF16)
    n_blk = h.shape[0] // MOBA_BLOCK
    kbar_ref[...] = jnp.zeros_like(kbar_ref)
    for p in range(N_PAIRS):
        sl = slice(p * LANES, (p + 1) * LANES)
        qa_ref[:, sl] = (rope_a(y[:, sl]) * scale_a).astype(BF16)
        k = rope_a(y[:, A_WIDTH + p * LANES:A_WIDTH + (p + 1) * LANES])
        ka_ref[:, sl] = k.astype(BF16)
        for blk in range(n_blk):
            kbar_ref[blk:blk + 1, sl] = jnp.mean(k[blk * MOBA_BLOCK:(blk + 1) * MOBA_BLOCK], axis=0, keepdims=True)
    yq = _dot(q_lat, wuq_ref[...])
    ykv = _dot(kv_lat, wukv_ref[...])
    va_ref[...] = _dot(h, win_ref[:, _C_VA:_C_VA + A_WIDTH]).astype(BF16)
    k_pe = rope_b(lat[:, Q_LORA + KV_LORA:])
    for hd in range(N_HEADS):
        sl = slice(hd * LANES, (hd + 1) * LANES)
        qb_ref[:, sl] = (rope_b(yq[:, sl]) * scale_b).astype(BF16)
        kb_ref[:, sl] = (ykv[:, sl] + k_pe).astype(BF16)
    vb_ref[...] = ykv[:, _KB_W:].astype(BF16)


KBAR_ROWS = 8


def _mixer_in(layer, x, seq, norm_mix, w_in, q_lat_norm, w_uq, kv_lat_norm, w_ukv, tabs):
    t, d = x.shape
    tm = min(TOK_TILE, seq)
    tps = seq // tm
    b = t // seq
    assert tm // MOBA_BLOCK <= KBAR_ROWS
    tok = lambda w: pl.BlockSpec((tm, w), lambda i, l: (i, 0))
    weights = (norm_mix, w_in, q_lat_norm, w_uq, kv_lat_norm, w_ukv)
    tok_out = lambda w: jax.ShapeDtypeStruct((t, w), BF16)
    return pl.pallas_call(
        _mixer_in_body,
        out_shape=[tok_out(A_WIDTH), tok_out(A_WIDTH), tok_out(A_WIDTH), tok_out(_KB_W), tok_out(_KB_W), tok_out(A_WIDTH),
                   jax.ShapeDtypeStruct((b, tps, KBAR_ROWS, A_WIDTH), F32), tok_out(d)],
        grid_spec=_layer_grid((t // tm,), [tok(d)] + [_layer_resident(w.shape) for w in weights] + [tok(TAB_W)],
                              [tok(A_WIDTH), tok(A_WIDTH), tok(A_WIDTH), tok(_KB_W), tok(_KB_W), tok(A_WIDTH),
                               pl.BlockSpec((None, None, KBAR_ROWS, A_WIDTH), lambda i, l: (i // tps, i % tps, 0, 0)),
                               tok(d)]),
        compiler_params=_params(1),
        name="mixer_in",
    )(layer, x, *weights, tabs)


def _softmax_init(tq):
    return (jnp.full((tq, 1), NEG_BIG, F32), jnp.zeros((tq, LANES), F32))


def _softmax_step(s, v_ones, state):
    m, acc = state
    m_new = jnp.maximum(m, jnp.max(s, axis=-1, keepdims=True))
    p = jnp.exp2(s - m_new)
    acc_new = jnp.exp2(m - m_new) * acc + _dot(p.astype(BF16), v_ones)
    return m_new, acc_new


def _values_and_ones(v):
    lane = lax.broadcasted_iota(jnp.int32, v.shape, 1)
    one = jnp.ones_like(v)
    return jnp.where(lane < HEAD_DIM, v, one), jnp.where(lane < HEAD_DIM, one, v)


def _causal(s, q0, k0):
    row = lax.broadcasted_iota(jnp.int32, s.shape, 0) + q0
    col = lax.broadcasted_iota(jnp.int32, s.shape, 1) + k0
    return jnp.where(col <= row, s, NEG_BIG)


def _merge_pair(states):
    (_, acc0), (_, acc1) = states
    lane = lax.broadcasted_iota(jnp.int32, acc0.shape, 1)
    num = jnp.where(lane < HEAD_DIM, acc0, acc1)
    den = pltpu.roll(jnp.where(lane < HEAD_DIM, acc1, acc0), HEAD_DIM, axis=1)
    return num / den


def _moba_queries(q, kbar, qi, n_blocks):
    tq = q.shape[0]
    lane = lax.broadcasted_iota(jnp.int32, q.shape, 1)
    row = lax.broadcasted_iota(jnp.int32, (MAX_BLOCKS, tq), 0)
    col = lax.broadcasted_iota(jnp.int32, (MAX_BLOCKS, tq), 1)
    cur = qi * (tq // MOBA_BLOCK) + col // MOBA_BLOCK
    q_aug = []
    for hd in range(2):
        qh = jnp.where((lane < HEAD_DIM) == (hd == 0), q, jnp.zeros_like(q))
        gate = jnp.where(row < cur, _dot_nt(kbar, qh), -jnp.inf)
        rank = jnp.zeros(gate.shape, jnp.int32)
        for n in range(n_blocks):
            gn = gate[n:n + 1, :]
            beats = (gn > gate) | ((gn == gate) & (row > n))
            rank = rank + beats.astype(jnp.int32)
        keep = ((row < cur) & (rank < MOBA_TOPK)) | (row == cur)
        dropped = jnp.where(keep, 0.0, 1.0)
        dropped = jnp.concatenate([dropped, jnp.zeros((LANES - MAX_BLOCKS, tq), F32)], axis=0)
        q_aug.append(jnp.concatenate([qh, dropped.T.astype(BF16)], axis=1))
    return q_aug


def _attn_body(qa_ref, ka_ref, va_ref, e_ref, qb_ref, kb_ref, vb_ref, kbar_ref, oa_ref, ob_ref):
    tq = ATT_TILE
    seq = ka_ref.shape[0]
    n_blocks = seq // MOBA_BLOCK
    kbar = kbar_ref[...].astype(BF16)

    def tile(qa, qb, j, states, q0, causal):
        rows = slice(j * tq, (j + 1) * tq)
        ka_aug = jnp.concatenate([ka_ref[rows, :], e_ref[rows, :]], axis=1)
        va_ones = _values_and_ones(va_ref[rows, :])
        vb_ones = _values_and_ones(vb_ref[rows, :])
        mask = functools.partial(_causal, q0=q0, k0=j * tq) if causal else (lambda s: s)
        rows_q = qa[0].shape[0]
        mla = []
        for hd in range(2):
            s = _dot_nt(qb[hd], kb_ref[rows, hd * LANES:(hd + 1) * LANES])
            mla.append(_softmax_step(mask(s), vb_ones[hd], states[2 + hd]))
        s_both = _dot_nt(jnp.concatenate(qa, axis=0), ka_aug)
        moba = [_softmax_step(mask(s_both[hd * rows_q:(hd + 1) * rows_q]), va_ones[hd], states[hd])
                for hd in range(2)]
        return tuple(moba + mla)

    tq2 = 2 * tq
    qa_all = [_moba_queries(qa_ref[t * tq2:(t + 1) * tq2, :], kbar, t, n_blocks) for t in range(seq // tq2)]
    for t in range(seq // tq2):
        qrows = slice(t * tq2, (t + 1) * tq2)
        qa = qa_all[t]
        qb = [qb_ref[qrows, hd * LANES:(hd + 1) * LANES] for hd in range(2)]
        states = (_softmax_init(tq2),) * 4
        for j in range(2 * t + 1):
            states = tile(qa, qb, j, states, t * tq2, j == 2 * t)
        low = tuple((m[tq:], acc[tq:]) for m, acc in states)
        low = tile([q[tq:] for q in qa], [q[tq:] for q in qb], 2 * t + 1, low, t * tq2 + tq, True)
        states = tuple((jnp.concatenate([m[:tq], ml], axis=0), jnp.concatenate([acc[:tq], accl], axis=0))
                       for (m, acc), (ml, accl) in zip(states, low))
        oa_ref[qrows, :] = _merge_pair(states[:2]).astype(oa_ref.dtype)
        ob_ref[qrows, :] = _merge_pair(states[2:]).astype(ob_ref.dtype)


def _attention(qa, ka, va, e_mat, qb, kb, vb, kbar):
    b, s, _ = qa.shape
    seq_spec = lambda w: pl.BlockSpec((None, s, w), lambda i, p: (i, 0, p))
    return pl.pallas_call(
        _attn_body,
        out_shape=[jax.ShapeDtypeStruct((b, s, A_WIDTH), BF16)] * 2,
        grid=(b, N_PAIRS),
        in_specs=[seq_spec(LANES), seq_spec(LANES), seq_spec(LANES), _resident(e_mat.shape),
                  seq_spec(2 * LANES), seq_spec(2 * LANES), seq_spec(LANES),
                  pl.BlockSpec((None, None, MAX_BLOCKS, LANES), lambda i, p: (i, p, 0, 0))],
        out_specs=[seq_spec(LANES)] * 2,
        compiler_params=_params(2),
        name="attention",
    )(qa, ka, va, e_mat, qb, kb, vb, kbar)


def _mixer_out_body(layer_ref, x_ref, h_ref, oa_ref, ob_ref, kv_ref, wg_ref, wa_ref, wb_ref, wo_ref,
                    gx_ref, wxq_ref, wxo_ref, gf_ref, o_ref, hf_ref):
    x = x_ref[...]
    d = x.shape[-1]
    h = h_ref[...]
    merged = (jax.nn.sigmoid(_dot(h, wg_ref[:, :d])) * _dot(oa_ref[...], wa_ref[...])
              + jax.nn.sigmoid(_dot(h, wg_ref[:, d:])) * _dot(ob_ref[...], wb_ref[...]))
    x1 = x + _dot(merged.astype(BF16), wo_ref[...])

    q = _dot(_rms(x1, gx_ref[...]).astype(BF16), wxq_ref[...]).astype(BF16)
    scale = X_HEAD_DIM ** -0.5
    heads = []
    for hd in range(X_HEADS):
        lanes = slice(hd * X_HEAD_DIM, (hd + 1) * X_HEAD_DIM)
        s = _dot_nt(q[:, lanes], kv_ref[:, lanes])
        m = jnp.max(s, axis=-1, keepdims=True)
        p = jnp.exp((s - m) * scale)
        l = jnp.sum(p, axis=-1, keepdims=True)
        v = kv_ref[:, X_WIDTH + hd * X_HEAD_DIM:X_WIDTH + (hd + 1) * X_HEAD_DIM]
        heads.append((_dot(p.astype(BF16), v) * (1.0 / l)).astype(BF16))
    x2 = x1 + _dot(jnp.concatenate(heads, axis=1), wxo_ref[...])
    o_ref[...] = x2
    hf_ref[...] = _rms(x2, gf_ref[...]).astype(BF16)


def _mixer_out(layer, x, h, seq, oa, ob, mem_kv, w_gates, w_a, w_b, w_out, norm_xattn, w_xq, w_xo, norm_ffn):
    t, d = x.shape
    tm = min(TOK_TILE, seq)
    tiles_per_seq = seq // tm
    m = mem_kv.shape[2]
    tok = lambda w: pl.BlockSpec((tm, w), lambda i, l: (i, 0))
    weights = (w_gates, w_a, w_b, w_out, norm_xattn, w_xq, w_xo, norm_ffn)
    return pl.pallas_call(
        _mixer_out_body,
        out_shape=[jax.ShapeDtypeStruct((t, d), F32), jax.ShapeDtypeStruct((t, d), BF16)],
        grid_spec=_layer_grid(
            (t // tm,),
            [tok(d), tok(d), tok(A_WIDTH), tok(A_WIDTH),
             pl.BlockSpec((None, None, m, 2 * X_WIDTH), lambda i, l: (l[0], i // tiles_per_seq, 0, 0))]
            + [_layer_resident(w.shape) for w in weights],
            [tok(d), tok(d)]),
        compiler_params=_params(1),
        name="mixer_out",
    )(layer, x, h, oa, ob, mem_kv, *weights)


FF_CHUNK = 1024


def _ffn_body(layer_ref, x_ref, h_ref, wgu_ref, wd_ref, gf_ref, o_ref, *, final):
    x = x_ref[...]
    h = h_ref[...]
    d_ff = wd_ref.shape[0]
    y = x
    for c0 in range(0, d_ff, FF_CHUNK):
        c1 = min(c0 + FF_CHUNK, d_ff)
        gt = _dot(h, wgu_ref[:, c0:c1])
        up = _dot(h, wgu_ref[:, d_ff + c0:d_ff + c1])
        a = (gt * jax.nn.sigmoid(gt) * up).astype(BF16)
        y = y + _dot(a, wd_ref[c0:c1, :])
    o_ref[...] = _rms(y, gf_ref[...]) if final else y


def _ffn(layer, x, h, w_gate_up, w_down, norm_final, final):
    t, d = x.shape
    tm = min(TOK_TILE, t)
    tok = pl.BlockSpec((tm, d), lambda i, l: (i, 0))
    tok_h = pl.BlockSpec((tm, d), lambda i, l: (i, 0))
    weights = (w_gate_up, w_down)
    return pl.pallas_call(
        functools.partial(_ffn_body, final=final),
        out_shape=jax.ShapeDtypeStruct((t, d), F32),
        grid_spec=_layer_grid((t // tm,), [tok, tok_h] + [_layer_resident(w.shape) for w in weights]
                              + [_resident(norm_final.shape)], tok),
        compiler_params=_params(1),
        name="ffn_final" if final else "ffn",
    )(layer, x, h, *weights, norm_final)


PREP_ROWS = 256


def _prep_w_in_body(wt_ref, o_ref, og_ref):
    in_w, rows = wt_ref.shape
    k0 = _C_LAT + Q_LORA + KV_LORA
    n_gate_groups = (in_w - k0 - MLA_ROPE) // LANES
    half = MLA_ROPE // 2
    lane = lax.broadcasted_iota(jnp.int32, (rows, LANES), 1)

    def cols(c0):
        n = min(LANES, in_w - c0)
        blk = wt_ref[c0:c0 + n, :]
        if n < LANES:
            blk = jnp.concatenate([blk, jnp.zeros((LANES - n, rows), blk.dtype)], axis=0)
        return blk.T

    for j in range(k0 // LANES):
        o_ref[:, j * LANES:(j + 1) * LANES] = cols(j * LANES).astype(BF16)
    g = cols(k0)
    kpe = jnp.where(lane < half, g, jnp.where((lane >= HALF_LANES) & (lane < HALF_LANES + half),
                                              pltpu.roll(g, HALF_LANES - half, axis=1), 0.0))
    o_ref[:, k0:k0 + LANES] = kpe.astype(BF16)
    a = g
    for j in range(n_gate_groups):
        b = cols(k0 + (j + 1) * LANES)
        shifted = pltpu.roll(jnp.where(lane >= MLA_ROPE, a, b), LANES - MLA_ROPE, axis=1)
        og_ref[:, j * LANES:(j + 1) * LANES] = shifted.astype(BF16)
        a = b


def _prep_w_in(w_in):
    n_layers, d, in_w = w_in.shape
    tr = min(PREP_ROWS, d)
    blk = lambda w: pl.BlockSpec((None, tr, w), lambda l, i: (l, i, 0))
    return pl.pallas_call(
        _prep_w_in_body,
        out_shape=[jax.ShapeDtypeStruct((n_layers, d, _C_G), BF16),
                   jax.ShapeDtypeStruct((n_layers, d, in_w - _C_G + LANES - MLA_ROPE), BF16)],
        grid=(n_layers, d // tr),
        in_specs=[pl.BlockSpec((None, in_w, tr), lambda l, i: (l, 0, i))],
        out_specs=[blk(_C_G), blk(in_w - _C_G + LANES - MLA_ROPE)],
        compiler_params=_params(2),
        name="prep_w_in",
    )(jnp.swapaxes(w_in, 1, 2))


def _mla_head_cols(nope, rope):
    half = MLA_ROPE // 2
    z = lambda n: jnp.zeros(nope.shape[:-1] + (n,), nope.dtype)
    t1, t2 = (z(half), z(half)) if rope is None else (rope[..., :half], rope[..., half:])
    return jnp.concatenate([t1, nope[..., :HEAD_DIM - half], t2, nope[..., HEAD_DIM - half:], z(LANES - MLA_QK)], axis=-1)


def _prep_weights(w_in, w_uq, w_ukv):
    n_layers = w_in.shape[0]
    w_in_r, w_gates = _prep_w_in(w_in)
    uq = w_uq.reshape(n_layers, Q_LORA, N_HEADS, MLA_QK)
    w_uq_r = _mla_head_cols(uq[..., :HEAD_DIM], uq[..., HEAD_DIM:]).reshape(n_layers, Q_LORA, _KB_W).astype(BF16)
    ukv = w_ukv.reshape(n_layers, KV_LORA, N_HEADS, 2 * HEAD_DIM)
    w_k = _mla_head_cols(ukv[..., :HEAD_DIM], None).reshape(n_layers, KV_LORA, _KB_W)
    w_v = ukv[..., HEAD_DIM:].reshape(n_layers, KV_LORA, A_WIDTH)
    w_ukv_r = jnp.concatenate([w_k, w_v], axis=-1).astype(BF16)
    return w_in_r, w_gates, w_uq_r, w_ukv_r


def kernel(x, mem, positions, norm_mix, w_in, q_lat_norm, w_uq, kv_lat_norm, w_ukv, w_branch_a, w_branch_b, w_out, norm_xattn, norm_mem, w_xq, w_xkv, w_xo, norm_ffn, w_gate_up, w_down, norm_final):
    b, s, d = x.shape
    n_layers = w_in.shape[0]
    t = b * s
    assert s % (2 * ATT_TILE) == 0 and ATT_TILE % MOBA_BLOCK == 0 and s // MOBA_BLOCK <= MAX_BLOCKS

    w_in_r, w_gates, w_uq_r, w_ukv_r = _prep_weights(w_in, w_uq, w_ukv)
    bf = lambda w: w.astype(BF16)
    row = lambda g: g.reshape(g.shape[0], 1, g.shape[1])
    tabs = _rope_tables(positions)
    mem_kv = _mem_kv(mem, row(norm_mem), bf(w_xkv))
    blk_of_row = np.arange(s) // MOBA_BLOCK
    e_mat = jnp.asarray(np.where(blk_of_row[:, None] == np.arange(LANES)[None, :], NEG_BIG, 0.0), BF16)
    norm_final = norm_final.reshape(1, d)

    norm_mix, q_lat_norm, kv_lat_norm = row(norm_mix), row(q_lat_norm), row(kv_lat_norm)
    norm_xattn, norm_ffn = row(norm_xattn), row(norm_ffn)
    w_a, w_b, w_o, w_q, w_xo_b = bf(w_branch_a), bf(w_branch_b), bf(w_out), bf(w_xq), bf(w_xo)
    w_gu, w_dn = bf(w_gate_up), bf(w_down)
    seq3 = lambda a: a.reshape(b, s, a.shape[-1])

    def layer(li, xt, final):
        li = jnp.full((1,), li, jnp.int32)
        qa, ka, va, qb, kb, vb, kbar, h = _mixer_in(li, xt, s, norm_mix, w_in_r, q_lat_norm, w_uq_r,
                                                   kv_lat_norm, w_ukv_r, tabs)
        n_blocks = s // MOBA_BLOCK
        kbar = kbar[:, :, :n_blocks // kbar.shape[1]].reshape(b, n_blocks, N_PAIRS, LANES)
        kbar = jnp.pad(kbar, ((0, 0), (0, MAX_BLOCKS - n_blocks), (0, 0), (0, 0))).transpose(0, 2, 1, 3)
        oa, ob = _attention(seq3(qa), seq3(ka), seq3(va), e_mat, seq3(qb), seq3(kb), seq3(vb), kbar)
        oa, ob = oa.reshape(t, A_WIDTH), ob.reshape(t, A_WIDTH)
        xt, hf = _mixer_out(li, xt, h, s, oa, ob, mem_kv, w_gates, w_a, w_b, w_o, norm_xattn, w_q, w_xo_b, norm_ffn)
        return _ffn(li, xt, hf, w_gu, w_dn, norm_final, final)

    xt = x.reshape(t, d)
    for li in range(n_layers):
        xt = layer(li, xt, li == n_layers - 1)
    return xt.reshape(b, s, d)
```
